```python
import math
import functools
import jax
import jax.numpy as jnp
from jax import lax
import numpy as np

D_MODEL = 2048
BATCH = 8
SEQ = 2048
DEPTH = 1
DEC_BATCH = 32
DEC_SEQ = 8
PAST_LEN = 16384
PAGE_SIZE = 128

NSA_HEADS = 16
NSA_KV = 4
NSA_DH = 64
CMP_BLOCK = 64
N_SEL = 16
WINDOW = 512
DSA_HEADS = 16
DSA_KV = 4
DSA_DH = 64
IDX_HEADS = 16
IDX_DH = 64
DSA_TOPK_MAX = 256
NUM_BUCKETS = 32
REL_MAX_DIST = 128
N_EXPERTS = 32
TOP_K = 4
D_FF = 2048
SWIGLU_LIMIT = 7.0
SWIGLU_ALPHA = 1.702
MOE_BLOCK = 128
Q_BLOCK = 128
RMS_EPS = 1e-6
NEG = -1e30
NSA_WIDTH = NSA_HEADS * NSA_DH
DSA_WIDTH = DSA_HEADS * DSA_DH
SPLIT_SIZES = (NSA_WIDTH, 2 * NSA_KV * NSA_DH, 2 * NSA_KV * NSA_DH, 2 * NSA_KV * NSA_DH, 3 * NSA_HEADS,
               DSA_WIDTH, 2 * DSA_KV * DSA_DH, IDX_HEADS * IDX_DH, IDX_DH, IDX_HEADS, 2 * D_MODEL)
IN_WIDTH = (NSA_WIDTH + 6 * NSA_KV * NSA_DH + 3 * NSA_HEADS + DSA_WIDTH + 2 * DSA_KV * DSA_DH
            + IDX_HEADS * IDX_DH + IDX_DH + IDX_HEADS + 2 * D_MODEL)

kernel_name = 'hybrid_nsa_dsa_moe_step'


def rms_norm(x, g):
    xf = x.astype(jnp.float32)
    y = xf * lax.rsqrt(jnp.mean(jnp.square(xf), -1, keepdims=True) + RMS_EPS)
    return (y * g.astype(jnp.float32)).astype(x.dtype)


def masked_softmax(s, mask):
    s = jnp.where(mask, s, NEG)
    e = jnp.where(mask, jnp.exp(s - jnp.max(s, -1, keepdims=True)), 0.0)
    return e / jnp.maximum(jnp.sum(e, -1, keepdims=True), 1e-30)


def rel_bucket(dist):
    n = jnp.maximum(dist, 0)
    exact = NUM_BUCKETS // 2
    log_b = exact + (jnp.log(jnp.maximum(n, 1).astype(jnp.float32) / exact)
                     / math.log(REL_MAX_DIST / exact) * (NUM_BUCKETS - exact)).astype(jnp.int32)
    return jnp.where(n < exact, n, jnp.minimum(log_b, NUM_BUCKETS - 1))


def project_heads(h, w_in):
    B, S, _ = h.shape
    z = h @ w_in
    parts, o = [], 0
    for n in SPLIT_SIZES:
        parts.append(z[..., o:o + n])
        o += n
    q_n, cmp, sel, win, g_n, q_d, dkv, q_i, k_i, w_i, g_m = parts
    kvn = (B, S, 2, NSA_KV, NSA_DH)
    return (q_n.reshape(B, S, NSA_HEADS, NSA_DH), cmp.reshape(kvn), sel.reshape(kvn), win.reshape(kvn),
            g_n.reshape(B, S, NSA_HEADS, 3), q_d.reshape(B, S, DSA_HEADS, DSA_DH),
            dkv.reshape(B, S, 2, DSA_KV, DSA_DH), q_i.reshape(B, S, IDX_HEADS, IDX_DH), k_i, w_i, g_m)


def compress(rows, pool):
    n = rows.shape[0] // CMP_BLOCK
    blocks = rows[:n * CMP_BLOCK].reshape(n, CMP_BLOCK, *rows.shape[1:])
    return jnp.einsum('nlcgd,cl->ncgd', blocks, pool)


def nsa_attend(q, gates, q_pos, kv_cmp, fetch_sel, kv_win, w_pos, bias_tab):
    f32 = jnp.float32
    T = q.shape[0]
    hpg = NSA_HEADS // NSA_KV
    scale = NSA_DH ** -0.5
    qg = q.reshape(T, NSA_KV, hpg, NSA_DH)
    btab = bias_tab.astype(f32).reshape(NUM_BUCKETS, NSA_KV, hpg)
    nb = kv_cmp.shape[0]
    dist_c = q_pos[:, None] - (jnp.arange(nb) * CMP_BLOCK + CMP_BLOCK - 1)[None, :]
    s_c = (jnp.einsum('tgqd,ngd->tgqn', qg, kv_cmp[:, 0]).astype(f32) * scale
           + btab[rel_bucket(dist_c)].transpose(0, 2, 3, 1))
    p_c = masked_softmax(s_c, (dist_c >= 0)[:, None, None, :])
    o_c = jnp.einsum('tgqn,ngd->tgqd', p_c.astype(q.dtype), kv_cmp[:, 1])
    cur = q_pos // CMP_BLOCK
    cand = (dist_c >= 0) & (jnp.arange(nb)[None, :] != cur[:, None])
    imp = jnp.where(cand[:, None, :], p_c.sum(2), -1.0)
    top_v, top_i = lax.top_k(imp, min(N_SEL - 1, nb))
    blk = jnp.concatenate([jnp.broadcast_to(cur[:, None, None], (T, NSA_KV, 1)), top_i], -1)
    blk_ok = jnp.concatenate([jnp.ones((T, NSA_KV, 1), bool), top_v >= 0], -1)
    pos_s = blk[..., None] * CMP_BLOCK + jnp.arange(CMP_BLOCK)
    k_s, v_s = fetch_sel(pos_s)
    dist_s = q_pos[:, None, None, None] - pos_s
    b_s = btab[rel_bucket(dist_s), jnp.arange(NSA_KV)[None, :, None, None]]
    s_s = (jnp.einsum('tgqd,tgnkd->tgqnk', qg, k_s).astype(f32) * scale
           + b_s.transpose(0, 1, 4, 2, 3))
    m_s = ((dist_s >= 0) & blk_ok[..., None])[:, :, None]
    n_keys = blk.shape[-1] * CMP_BLOCK
    p_s = masked_softmax(s_s.reshape(T, NSA_KV, hpg, n_keys), m_s.reshape(T, NSA_KV, 1, n_keys))
    o_s = jnp.einsum('tgqm,tgmd->tgqd', p_s.astype(q.dtype), v_s.reshape(T, NSA_KV, n_keys, NSA_DH))
    dist_w = q_pos[:, None] - w_pos[None, :]
    s_w = (jnp.einsum('tgqd,wgd->tgqw', qg, kv_win[:, 0]).astype(f32) * scale
           + btab[rel_bucket(dist_w)].transpose(0, 2, 3, 1))
    m_w = (dist_w >= 0) & (dist_w < WINDOW) & (w_pos >= 0)[None, :]
    p_w = masked_softmax(s_w, m_w[:, None, None, :])
    o_w = jnp.einsum('tgqw,wgd->tgqd', p_w.astype(q.dtype), kv_win[:, 1])
    g = jax.nn.sigmoid(gates.astype(f32)).reshape(T, NSA_KV, hpg, 3)
    o = g[..., 0:1] * o_c + g[..., 1:2] * o_s + g[..., 2:3] * o_w
    return o.reshape(T, NSA_WIDTH).astype(q.dtype)


def indexer_topk(q_i, w_i, k_i, q_pos, topk):
    f32 = jnp.float32
    s = jax.nn.relu(jnp.einsum('thd,sd->ths', q_i, k_i).astype(f32) * IDX_DH ** -0.5)
    score = jnp.einsum('ths,th->ts', s, w_i.astype(f32)) * IDX_HEADS ** -0.5
    score = jnp.where(jnp.arange(k_i.shape[0])[None, :] <= q_pos[:, None], score, NEG)
    return lax.top_k(score, topk)[1]


def dsa_attend(q, q_pos, idx, k_sel, v_sel, bias_tab):
    f32 = jnp.float32
    T = q.shape[0]
    hpg = DSA_HEADS // DSA_KV
    qg = q.reshape(T, DSA_KV, hpg, DSA_DH)
    dist = q_pos[:, None] - idx
    bias = bias_tab.astype(f32).reshape(NUM_BUCKETS, DSA_KV, hpg)[rel_bucket(dist)]
    s = (jnp.einsum('tgqd,tkgd->tgqk', qg, k_sel).astype(f32) * DSA_DH ** -0.5
         + bias.transpose(0, 2, 3, 1))
    p = masked_softmax(s, (dist >= 0)[:, None, None, :])
    o = jnp.einsum('tgqk,tkgd->tgqd', p.astype(q.dtype), v_sel)
    return o.reshape(T, DSA_WIDTH)


def merge_branches(o_n, o_d, g_m, w_up_nsa, w_up_dsa, w_out):
    g_a, g_b = jnp.split(jax.nn.sigmoid(g_m), 2, axis=-1)
    return (g_a * (o_n @ w_up_nsa) + g_b * (o_d @ w_up_dsa)) @ w_out


def mixer_prompt(l, rel_bias, w_in, cmp_pool, w_up_nsa, w_up_dsa, w_out, h):
    B, S, _ = h.shape
    q_n, cmp, sel, win, g_n, q_d, dkv, q_i, k_i, w_i, g_m = project_heads(h, w_in[l])
    pool = cmp_pool[l]
    bias_n, bias_d = rel_bias[:, :NSA_HEADS], rel_bias[:, NSA_HEADS:]
    topk = min(DSA_TOPK_MAX, S // 4)
    garr = jnp.arange(NSA_KV)[None, :, None, None]

    def per_seq(args):
        qn, cm, se, wn, gn, qd, kv, qi, ki, wi = args
        kv_cmp = compress(cm, pool)
        wn_pad = jnp.pad(wn, ((WINDOW, 0), (0, 0), (0, 0), (0, 0)))

        def fetch(pos):
            r = se[pos, :, garr]
            return r[..., 0, :], r[..., 1, :]

        def per_block(i):
            t0 = i * Q_BLOCK
            q_pos = t0 + jnp.arange(Q_BLOCK)
            blk = lambda a: lax.dynamic_slice_in_dim(a, t0, Q_BLOCK, 0)
            kv_win = lax.dynamic_slice_in_dim(wn_pad, t0, WINDOW + Q_BLOCK, 0)
            w_pos = t0 - WINDOW + jnp.arange(WINDOW + Q_BLOCK)
            o_n = nsa_attend(blk(qn), blk(gn), q_pos, kv_cmp, fetch, kv_win, w_pos, bias_n)
            idx = indexer_topk(blk(qi), blk(wi), ki, q_pos, topk)
            kv_sel = kv[idx]
            o_d = dsa_attend(blk(qd), q_pos, idx, kv_sel[:, :, 0], kv_sel[:, :, 1], bias_d)
            return o_n, o_d

        o_n, o_d = lax.map(per_block, jnp.arange(S // Q_BLOCK))
        return o_n.reshape(S, NSA_WIDTH), o_d.reshape(S, DSA_WIDTH)

    o_n, o_d = lax.map(per_seq, (q_n, cmp, sel, win, g_n, q_d, dkv, q_i, k_i, w_i))
    y = merge_branches(o_n, o_d, g_m, w_up_nsa[l], w_up_dsa[l], w_out[l])
    keep = min(WINDOW, S)
    return y, (cmp, sel, win[:, S - keep:], dkv, k_i)


def mixer_sample(l, cache_nsa_cmp, cache_nsa_sel, cache_nsa_win, cache_dsa_kv, cache_dsa_idx, page_table,
                 rel_bias, w_in, cmp_pool, w_up_nsa, w_up_dsa, w_out, h):
    B, T, _ = h.shape
    q_n, cmp, sel, win, g_n, q_d, dkv, q_i, k_i, w_i, g_m = project_heads(h, w_in[l])
    pool = cmp_pool[l]
    bias_n, bias_d = rel_bias[:, :NSA_HEADS], rel_bias[:, NSA_HEADS:]
    past = page_table.shape[1] * PAGE_SIZE
    topk = min(DSA_TOPK_MAX, (past + T) // 4)
    q_pos = past + jnp.arange(T)
    garr = jnp.arange(NSA_KV)[None, :, None, None]

    def per_seq(args):
        qn, cm, se, wn, gn, qd, kv, qi, ki, wi, pt, wbuf = args
        past_cmp = cache_nsa_cmp[l, pt].reshape(past, 2, NSA_KV, NSA_DH)
        kv_cmp = jnp.concatenate([compress(past_cmp, pool), compress(cm, pool)], 0)

        def fetch(pos):
            pc = jnp.minimum(pos, past - 1)
            r_pool = cache_nsa_sel[l, pt[pc // PAGE_SIZE], pc % PAGE_SIZE, :, garr]
            r_new = se[jnp.clip(pos - past, 0, T - 1), :, garr]
            r = jnp.where((pos >= past)[..., None, None], r_new, r_pool)
            return r[..., 0, :], r[..., 1, :]

        kv_win = jnp.concatenate([wbuf, wn], 0)
        w_pos = past - wbuf.shape[0] + jnp.arange(kv_win.shape[0])
        o_n = nsa_attend(qn, gn, q_pos, kv_cmp, fetch, kv_win, w_pos, bias_n)
        k_idx = jnp.concatenate([cache_dsa_idx[l, pt].reshape(past, IDX_DH), ki], 0)
        idx = indexer_topk(qi, wi, k_idx, q_pos, topk)
        pc = jnp.minimum(idx, past - 1)
        kv_sel = jnp.where((idx >= past)[..., None, None, None], kv[jnp.clip(idx - past, 0, T - 1)],
                           cache_dsa_kv[l, pt[pc // PAGE_SIZE], pc % PAGE_SIZE])
        o_d = dsa_attend(qd, q_pos, idx, kv_sel[:, :, 0], kv_sel[:, :, 1], bias_d)
        return o_n, o_d, kv_win[T:]

    o_n, o_d, new_win = lax.map(per_seq, (q_n, cmp, sel, win, g_n, q_d, dkv, q_i, k_i, w_i,
                                          page_table, cache_nsa_win[l]))
    y = merge_branches(o_n, o_d, g_m, w_up_nsa[l], w_up_dsa[l], w_out[l])
    return y, (cmp, sel, new_win, dkv, k_i)


def clamped_swiglu(gu):
    g, lin = jnp.split(gu, 2, axis=-1)
    g = jnp.minimum(g, SWIGLU_LIMIT)
    lin = jnp.clip(lin, -SWIGLU_LIMIT, SWIGLU_LIMIT)
    return g * jax.nn.sigmoid(SWIGLU_ALPHA * g) * (lin + 1)


def moe_ffn(h, l, w_router, b_router, w_gu, b_gu, w_down, b_down):
    T, D = h.shape
    logits = (h @ w_router[l]).astype(jnp.float32) + b_router[l].astype(jnp.float32)
    top_v, top_e = lax.top_k(logits, TOP_K)
    gate = jax.nn.softmax(top_v, axis=-1).reshape(-1)
    flat_e = top_e.reshape(-1)
    n_assign = flat_e.shape[0]
    order = jnp.argsort(flat_e)
    e_sorted = flat_e[order]
    tok_sorted = order // TOP_K
    counts = jnp.bincount(flat_e, length=N_EXPERTS)
    padded = (counts + MOE_BLOCK - 1) // MOE_BLOCK * MOE_BLOCK
    pad_end = jnp.cumsum(padded)
    start = jnp.cumsum(counts) - counts
    dest = (pad_end - padded)[e_sorted] + jnp.arange(n_assign) - start[e_sorted]
    n_blocks = -(-n_assign // MOE_BLOCK) + N_EXPERTS
    row_tok = jnp.zeros((n_blocks * MOE_BLOCK,), jnp.int32).at[dest].set(tok_sorted)
    blk_e = jnp.minimum(jnp.searchsorted(pad_end, jnp.arange(n_blocks) * MOE_BLOCK, side='right'), N_EXPERTS - 1)

    def expert_block(args):
        xb, e = args
        gu = xb @ w_gu[l, e] + b_gu[l, e]
        return clamped_swiglu(gu) @ w_down[l, e] + b_down[l, e]

    out_rows = lax.map(expert_block, (h[row_tok].reshape(n_blocks, MOE_BLOCK, D), blk_e)).reshape(-1, D)
    contrib = out_rows[dest] * gate[order][:, None].astype(h.dtype)
    return jax.ops.segment_sum(contrib, tok_sorted, num_segments=T)


def decoder_layer(x, c, l, mix_fn, w_mod, b_mod, g_pre_mix, g_post_mix, g_pre_ffn, g_post_ffn,
                  w_router, b_router, w_gu, b_gu, w_down, b_down):
    mod = jax.nn.silu(c) @ w_mod[l] + b_mod[l]
    sh1, sc1, ga1, sh2, sc2, ga2 = jnp.split(mod[:, None, :], 6, axis=-1)
    h = rms_norm(x, g_pre_mix[l]) * (1 + sc1) + sh1
    m, state = mix_fn(h)
    x = x + ga1 * rms_norm(m, g_post_mix[l])
    h = rms_norm(x, g_pre_ffn[l]) * (1 + sc2) + sh2
    B, S, D = h.shape
    f = moe_ffn(h.reshape(B * S, D), l, w_router, b_router, w_gu, b_gu, w_down, b_down).reshape(B, S, D)
    x = x + ga2 * rms_norm(f, g_post_ffn[l])
    return x, state


def setup_inputs(seed: int = 0) -> dict:
    key = jax.random.key(seed)
    ks = jax.random.split(key, 32)
    f32 = jnp.float32
    n_pages = PAST_LEN // PAGE_SIZE
    n_phys = (DEC_BATCH * n_pages * 5 + 3) // 4
    win_len = min(WINDOW, PAST_LEN)
    nrm = lambda k, shape, s=1.0: jax.random.normal(k, shape, f32) * s
    page_table = jax.random.permutation(ks[9], n_phys)[:DEC_BATCH * n_pages].reshape(DEC_BATCH, n_pages).astype(jnp.int32)
    return {
        'x_prompt': nrm(ks[0], (BATCH, SEQ, D_MODEL)),
        'x_sample': nrm(ks[1], (DEC_BATCH, DEC_SEQ, D_MODEL)),
        'c_prompt': nrm(ks[2], (BATCH, D_MODEL)),
        'c_sample': nrm(ks[3], (DEC_BATCH, D_MODEL)),
        'cache_nsa_cmp': nrm(ks[4], (DEPTH, n_phys, PAGE_SIZE, 2, NSA_KV, NSA_DH)),
        'cache_nsa_sel': nrm(ks[5], (DEPTH, n_phys, PAGE_SIZE, 2, NSA_KV, NSA_DH)),
        'cache_nsa_win': nrm(ks[6], (DEPTH, DEC_BATCH, win_len, 2, NSA_KV, NSA_DH)),
        'cache_dsa_kv': nrm(ks[7], (DEPTH, n_phys, PAGE_SIZE, 2, DSA_KV, DSA_DH)),
        'cache_dsa_idx': nrm(ks[8], (DEPTH, n_phys, PAGE_SIZE, IDX_DH)),
        'page_table': page_table,
        'rel_bias': nrm(ks[10], (NUM_BUCKETS, NSA_HEADS + DSA_HEADS), 0.5),
        'w_mod': nrm(ks[11], (DEPTH, D_MODEL, 6 * D_MODEL), 0.2 * D_MODEL ** -0.5),
        'b_mod': nrm(ks[12], (DEPTH, 6 * D_MODEL), 0.02),
        'g_pre_mix': 1.0 + nrm(ks[13], (DEPTH, D_MODEL), 0.05),
        'g_post_mix': 1.0 + nrm(ks[14], (DEPTH, D_MODEL), 0.05),
        'g_pre_ffn': 1.0 + nrm(ks[15], (DEPTH, D_MODEL), 0.05),
        'g_post_ffn': 1.0 + nrm(ks[16], (DEPTH, D_MODEL), 0.05),
        'w_in': nrm(ks[17], (DEPTH, D_MODEL, IN_WIDTH), D_MODEL ** -0.5),
        'cmp_pool': (1.0 + nrm(ks[18], (DEPTH, 2, CMP_BLOCK), 0.1)) / CMP_BLOCK,
        'w_up_nsa': nrm(ks[19], (DEPTH, NSA_WIDTH, D_MODEL), NSA_WIDTH ** -0.5),
        'w_up_dsa': nrm(ks[20], (DEPTH, DSA_WIDTH, D_MODEL), DSA_WIDTH ** -0.5),
        'w_out': nrm(ks[21], (DEPTH, D_MODEL, D_MODEL), D_MODEL ** -0.5),
        'w_router': nrm(ks[22], (DEPTH, D_MODEL, N_EXPERTS), D_MODEL ** -0.5),
        'b_router': nrm(ks[23], (DEPTH, N_EXPERTS), 0.01),
        'w_gu': nrm(ks[24], (DEPTH, N_EXPERTS, D_MODEL, 2 * D_FF), D_MODEL ** -0.5),
        'b_gu': nrm(ks[25], (DEPTH, N_EXPERTS, 2 * D_FF), 0.01),
        'w_down': nrm(ks[26], (DEPTH, N_EXPERTS, D_FF, D_MODEL), D_FF ** -0.5),
        'b_down': nrm(ks[27], (DEPTH, N_EXPERTS, D_MODEL), 0.01),
    }


def reference(x_prompt, x_sample, c_prompt, c_sample, cache_nsa_cmp, cache_nsa_sel, cache_nsa_win,
              cache_dsa_kv, cache_dsa_idx, page_table, rel_bias, w_mod, b_mod, g_pre_mix, g_post_mix,
              g_pre_ffn, g_post_ffn, w_in, cmp_pool, w_up_nsa, w_up_dsa, w_out, w_router, b_router,
              w_gu, b_gu, w_down, b_down):
    layer_w = (w_mod, b_mod, g_pre_mix, g_post_mix, g_pre_ffn, g_post_ffn,
               w_router, b_router, w_gu, b_gu, w_down, b_down)
    mix_w = (rel_bias, w_in, cmp_pool, w_up_nsa, w_up_dsa, w_out)
    xp, xs = x_prompt, x_sample
    new_p, new_s = [], []
    for l in range(DEPTH):
        mix_p = functools.partial(mixer_prompt, l, *mix_w)
        mix_s = functools.partial(mixer_sample, l, cache_nsa_cmp, cache_nsa_sel, cache_nsa_win,
                                  cache_dsa_kv, cache_dsa_idx, page_table, *mix_w)
        xp, st_p = decoder_layer(xp, c_prompt, l, mix_p, *layer_w)
        xs, st_s = decoder_layer(xs, c_sample, l, mix_s, *layer_w)
        new_p.append(st_p)
        new_s.append(st_s)
    p_cmp, p_sel, p_win, p_dkv, p_idx = (jnp.stack(a) for a in zip(*new_p))
    s_cmp, s_sel, s_win, s_dkv, s_idx = (jnp.stack(a) for a in zip(*new_s))
    return (xp, xs, p_cmp, p_sel, p_win, p_dkv, p_idx, s_cmp, s_sel, s_win, s_dkv, s_idx)
```

```python
import functools
import math

import jax
import jax.numpy as jnp
import numpy as np
from jax import lax
from jax.experimental import pallas as pl
from jax.experimental.pallas import tpu as pltpu

F32 = jnp.float32
BF16 = jnp.bfloat16

D_MODEL = 2048
NSA_HEADS = 16
NSA_KV = 4
NSA_DH = 64
CMP_BLOCK = 64
N_SEL = 16
WINDOW = 512
DSA_HEADS = 16
DSA_KV = 4
DSA_DH = 64
IDX_HEADS = 16
IDX_DH = 64
DSA_TOPK_MAX = 256
NUM_BUCKETS = 32
REL_MAX_DIST = 128
N_EXPERTS = 32
TOP_K = 4
D_FF = 2048
SWIGLU_LIMIT = 7.0
SWIGLU_ALPHA = 1.702
PAGE_SIZE = 128
Q_BLOCK = 128
RMS_EPS = 1e-6
NEG = -1e30
NSA_WIDTH = NSA_HEADS * NSA_DH
DSA_WIDTH = DSA_HEADS * DSA_DH
KV_W = 2 * NSA_KV * NSA_DH

_SPLIT = (NSA_WIDTH, KV_W, KV_W, KV_W, 3 * NSA_HEADS, DSA_WIDTH, 2 * DSA_KV * DSA_DH,
          IDX_HEADS * IDX_DH, IDX_DH, IDX_HEADS, 2 * D_MODEL)
_OFF = tuple(int(v) for v in np.cumsum((0,) + _SPLIT))

MOE_TM = 256
VMEM_LIMIT = 48 * 1024 * 1024


def _cparams(sem):
    return pltpu.CompilerParams(dimension_semantics=sem, vmem_limit_bytes=VMEM_LIMIT)


def _norm_mod_kernel(x_ref, g_ref, sc_ref, sh_ref, o_ref):
    x = x_ref[0]
    y = x * lax.rsqrt(jnp.mean(x * x, axis=-1, keepdims=True) + RMS_EPS) * g_ref[...]
    o_ref[0] = (y * (1.0 + sc_ref[0]) + sh_ref[0]).astype(o_ref.dtype)


def norm_mod(x, g, scale, shift):
    B, S, D = x.shape
    ts = min(S, 512)
    row = pl.BlockSpec((1, ts, D), lambda b, s: (b, s, 0))
    per_b = pl.BlockSpec((1, 1, D), lambda b, s: (b, 0, 0))
    return pl.pallas_call(
        _norm_mod_kernel,
        out_shape=jax.ShapeDtypeStruct((B, S, D), BF16),
        grid=(B, S // ts),
        in_specs=[row, pl.BlockSpec((1, D), lambda b, s: (0, 0)), per_b, per_b],
        out_specs=row,
        compiler_params=_cparams(("parallel", "parallel")),
        name="norm_mod",
    )(x, g.reshape(1, D), scale.reshape(B, 1, D), shift.reshape(B, 1, D))


def _matmul_kernel(a_ref, w_ref, b_ref, o_ref):
    acc = jnp.dot(a_ref[...], w_ref[...].astype(BF16), preferred_element_type=F32)
    o_ref[...] = (acc + b_ref[...]).astype(o_ref.dtype)


def matmul(a, w, bias=None, out_dtype=F32, tm=512, tn=512):
    M, K = a.shape
    N = w.shape[1]
    tm, tn = min(tm, M), min(tn, N)
    assert M % tm == 0 and N % tn == 0
    if bias is None:
        bias = jnp.zeros((N,), F32)
    return pl.pallas_call(
        _matmul_kernel,
        out_shape=jax.ShapeDtypeStruct((M, N), out_dtype),
        grid=(M // tm, N // tn),
        in_specs=[pl.BlockSpec((tm, K), lambda i, j: (i, 0)),
                  pl.BlockSpec((K, tn), lambda i, j: (0, j)),
                  pl.BlockSpec((1, tn), lambda i, j: (0, j))],
        out_specs=pl.BlockSpec((tm, tn), lambda i, j: (i, j)),
        compiler_params=_cparams(("parallel", "parallel")),
        name="matmul",
    )(a, w, bias.reshape(1, N))


def _merge_kernel(on_ref, od_ref, ga_ref, gb_ref, wn_ref, wd_ref, o_ref):
    a = jnp.dot(on_ref[...], wn_ref[...], preferred_element_type=F32)
    b = jnp.dot(od_ref[...], wd_ref[...], preferred_element_type=F32)
    o_ref[...] = (jax.nn.sigmoid(ga_ref[...]) * a + jax.nn.sigmoid(gb_ref[...]) * b).astype(o_ref.dtype)


def merge_up(o_n, o_d, zg, w_up_nsa, w_up_dsa, tm=512, tn=512):
    M = o_n.shape[0]
    D = w_up_nsa.shape[1]
    tm = min(tm, M)
    nj = D // tn
    return pl.pallas_call(
        _merge_kernel,
        out_shape=jax.ShapeDtypeStruct((M, D), BF16),
        grid=(M // tm, nj),
        in_specs=[pl.BlockSpec((tm, o_n.shape[1]), lambda i, j: (i, 0)),
                  pl.BlockSpec((tm, o_d.shape[1]), lambda i, j: (i, 0)),
                  pl.BlockSpec((tm, tn), lambda i, j: (i, j)),
                  pl.BlockSpec((tm, tn), lambda i, j: (i, j + nj)),
                  pl.BlockSpec((o_n.shape[1], tn), lambda i, j: (0, j)),
                  pl.BlockSpec((o_d.shape[1], tn), lambda i, j: (0, j))],
        out_specs=pl.BlockSpec((tm, tn), lambda i, j: (i, j)),
        compiler_params=_cparams(("parallel", "parallel")),
        name="merge_up",
    )(o_n, o_d, zg, zg, w_up_nsa, w_up_dsa)


def _rms(x, g):
    return x * lax.rsqrt(jnp.mean(x * x, axis=-1, keepdims=True) + RMS_EPS) * g


def _out_proj_kernel(u_ref, x_ref, w_ref, gpost_ref, gpre_ref, ga_ref, sc_ref, sh_ref, wr_ref, br_ref,
                     x1_ref, h_ref, lg_ref):
    m = jnp.dot(u_ref[0], w_ref[...], preferred_element_type=F32)
    x1 = x_ref[0] + ga_ref[0] * _rms(m, gpost_ref[...])
    x1_ref[0] = x1
    h = _rms(x1, gpre_ref[...]) * (1.0 + sc_ref[0]) + sh_ref[0]
    h_ref[0] = h.astype(h_ref.dtype)
    lg_ref[0] = jnp.dot(h, wr_ref[...], preferred_element_type=F32,
                        precision=lax.Precision.HIGHEST) + br_ref[...]


def out_proj(u, x, w_out, g_post, g_pre, gate, scale, shift, w_router, b_router):
    B, S, D = x.shape
    ts = min(S, 256)
    E = w_router.shape[1]
    row = lambda d: pl.BlockSpec((1, ts, d), lambda b, s: (b, s, 0))
    per_b = pl.BlockSpec((1, 1, D), lambda b, s: (b, 0, 0))
    vec = lambda d: pl.BlockSpec((1, d), lambda b, s: (0, 0))
    return pl.pallas_call(
        _out_proj_kernel,
        out_shape=(jax.ShapeDtypeStruct((B, S, D), F32), jax.ShapeDtypeStruct((B, S, D), BF16),
                   jax.ShapeDtypeStruct((B, S, E), F32)),
        grid=(B, S // ts),
        in_specs=[row(D), row(D), pl.BlockSpec((D, D), lambda b, s: (0, 0)), vec(D), vec(D),
                  per_b, per_b, per_b, pl.BlockSpec((D, E), lambda b, s: (0, 0)), vec(E)],
        out_specs=(row(D), row(D), row(E)),
        compiler_params=_cparams(("parallel", "parallel")),
        name="out_proj",
    )(u.reshape(B, S, D), x, w_out, g_post.reshape(1, D), g_pre.reshape(1, D), gate.reshape(B, 1, D),
      scale.reshape(B, 1, D), shift.reshape(B, 1, D), w_router, b_router.reshape(1, E))


def _expert_changed(be_ref, b):
    return jnp.logical_or(b == 0, be_ref[b] != be_ref[jnp.maximum(b - 1, 0)])


def _moe_up_kernel(be_ref, nu_ref, x_ref, wg_ref, wl_ref, bg_ref, bl_ref, o_ref, wg_s, wl_s):
    b = pl.program_id(1)

    @pl.when(_expert_changed(be_ref, b))
    def _():
        wg_s[...] = wg_ref[0].astype(BF16)
        wl_s[...] = wl_ref[0].astype(BF16)

    @pl.when(b < nu_ref[0])
    def _():
        x = x_ref[...]
        g = jnp.dot(x, wg_s[...], preferred_element_type=F32) + bg_ref[0]
        lin = jnp.dot(x, wl_s[...], preferred_element_type=F32) + bl_ref[0]
        g = jnp.minimum(g, SWIGLU_LIMIT)
        lin = jnp.clip(lin, -SWIGLU_LIMIT, SWIGLU_LIMIT)
        o_ref[...] = (g * jax.nn.sigmoid(SWIGLU_ALPHA * g) * (lin + 1.0)).astype(o_ref.dtype)

    @pl.when(b >= nu_ref[0])
    def _():
        o_ref[...] = jnp.zeros_like(o_ref)


def _moe_down_kernel(be_ref, nu_ref, a_ref, w_ref, bias_ref, o_ref, w_s):
    b = pl.program_id(1)

    @pl.when(_expert_changed(be_ref, b))
    def _():
        w_s[...] = w_ref[0].astype(BF16)

    @pl.when(b < nu_ref[0])
    def _():
        o_ref[...] = jnp.dot(a_ref[...], w_s[...], preferred_element_type=F32) + bias_ref[0]

    @pl.when(b >= nu_ref[0])
    def _():
        o_ref[...] = jnp.zeros_like(o_ref)


def moe_experts(xs, blk_e, n_used, w_gu, b_gu, w_down, b_down, tf=512, tn=512):
    R, D = xs.shape
    nb = R // MOE_TM
    E = w_gu.shape[0]
    nf = D_FF // tf
    act = pl.pallas_call(
        _moe_up_kernel,
        out_shape=jax.ShapeDtypeStruct((R, D_FF), BF16),
        grid_spec=pltpu.PrefetchScalarGridSpec(
            num_scalar_prefetch=2,
            grid=(nf, nb),
            in_specs=[pl.BlockSpec((MOE_TM, D), lambda f, b, be, nu: (b, 0)),
                      pl.BlockSpec((1, D, tf), lambda f, b, be, nu: (be[b], 0, f)),
                      pl.BlockSpec((1, D, tf), lambda f, b, be, nu: (be[b], 0, f + nf)),
                      pl.BlockSpec((1, 1, tf), lambda f, b, be, nu: (be[b], 0, f)),
                      pl.BlockSpec((1, 1, tf), lambda f, b, be, nu: (be[b], 0, f + nf))],
            out_specs=pl.BlockSpec((MOE_TM, tf), lambda f, b, be, nu: (b, f)),
            scratch_shapes=[pltpu.VMEM((D, tf), BF16), pltpu.VMEM((D, tf), BF16)]),
        compiler_params=_cparams(("arbitrary", "arbitrary")),
        name="moe_up",
    )(blk_e, n_used, xs, w_gu, w_gu, b_gu.reshape(E, 1, 2 * D_FF), b_gu.reshape(E, 1, 2 * D_FF))
    nn = D // tn
    return pl.pallas_call(
        _moe_down_kernel,
        out_shape=jax.ShapeDtypeStruct((R, D), F32),
        grid_spec=pltpu.PrefetchScalarGridSpec(
            num_scalar_prefetch=2,
            grid=(nn, nb),
            in_specs=[pl.BlockSpec((MOE_TM, D_FF), lambda j, b, be, nu: (b, 0)),
                      pl.BlockSpec((1, D_FF, tn), lambda j, b, be, nu: (be[b], 0, j)),
                      pl.BlockSpec((1, 1, tn), lambda j, b, be, nu: (be[b], 0, j))],
            out_specs=pl.BlockSpec((MOE_TM, tn), lambda j, b, be, nu: (b, j)),
            scratch_shapes=[pltpu.VMEM((D_FF, tn), BF16)]),
        compiler_params=_cparams(("arbitrary", "arbitrary")),
        name="moe_down",
    )(blk_e, n_used, act, w_down, b_down.reshape(E, 1, D))


def _combine_kernel(r_ref, p_ref, x_ref, g_ref, ga_ref, o_ref):
    p = p_ref[0]
    f = r_ref[0, :, 0, :] * p[:, 0:1]
    for k in range(1, TOP_K):
        f = f + r_ref[0, :, k, :] * p[:, k:k + 1]
    o_ref[0] = x_ref[0] + ga_ref[0] * _rms(f, g_ref[...])


def moe_combine(rows, probs, x, g_post, gate):
    B, S, D = x.shape
    ts = min(S, 128)
    return pl.pallas_call(
        _combine_kernel,
        out_shape=jax.ShapeDtypeStruct((B, S, D), F32),
        grid=(B, S // ts),
        in_specs=[pl.BlockSpec((1, ts, TOP_K, D), lambda b, s: (b, s, 0, 0)),
                  pl.BlockSpec((1, ts, TOP_K), lambda b, s: (b, s, 0)),
                  pl.BlockSpec((1, ts, D), lambda b, s: (b, s, 0)),
                  pl.BlockSpec((1, D), lambda b, s: (0, 0)),
                  pl.BlockSpec((1, 1, D), lambda b, s: (b, 0, 0))],
        out_specs=pl.BlockSpec((1, ts, D), lambda b, s: (b, s, 0)),
        compiler_params=_cparams(("parallel", "parallel")),
        name="moe_combine",
    )(rows.reshape(B, S, TOP_K, D), probs.reshape(B, S, TOP_K), x, g_post.reshape(1, D), gate.reshape(B, 1, D))


def moe_route(logits):
    T = logits.shape[0]
    top_v, top_e = lax.top_k(logits, TOP_K)
    probs = jax.nn.softmax(top_v, axis=-1)
    flat_e = top_e.reshape(-1)
    n_assign = flat_e.shape[0]
    order = jnp.argsort(flat_e)
    e_sorted = flat_e[order]
    counts = jnp.bincount(flat_e, length=N_EXPERTS)
    padded = (counts + MOE_TM - 1) // MOE_TM * MOE_TM
    pad_end = jnp.cumsum(padded)
    start = jnp.cumsum(counts) - counts
    dest = (pad_end - padded)[e_sorted] + jnp.arange(n_assign) - start[e_sorted]
    n_blocks = -(-n_assign // MOE_TM) + N_EXPERTS
    row_tok = jnp.zeros((n_blocks * MOE_TM,), jnp.int32).at[dest].set((order // TOP_K).astype(jnp.int32))
    blk_e = jnp.minimum(jnp.searchsorted(pad_end, jnp.arange(n_blocks) * MOE_TM, side='right'),
                        N_EXPERTS - 1).astype(jnp.int32)
    dest_of = jnp.zeros((n_assign,), jnp.int32).at[order].set(dest.astype(jnp.int32))
    n_used = (pad_end[-1] // MOE_TM).astype(jnp.int32).reshape(1)
    return probs, row_tok, blk_e, n_used, dest_of.reshape(T, TOP_K)


def _rel_bucket(dist):
    n = jnp.maximum(dist, 0)
    exact = NUM_BUCKETS // 2
    log_b = exact + (jnp.log(jnp.maximum(n, 1).astype(F32) / exact)
                     / math.log(REL_MAX_DIST / exact) * (NUM_BUCKETS - exact)).astype(jnp.int32)
    return jnp.where(n < exact, n, jnp.minimum(log_b, NUM_BUCKETS - 1))


def _masked_softmax(s, mask):
    s = jnp.where(mask, s, NEG)
    e = jnp.where(mask, jnp.exp(s - jnp.max(s, -1, keepdims=True)), 0.0)
    return e / jnp.maximum(jnp.sum(e, -1, keepdims=True), 1e-30)


def _compress(rows, pool):
    n = rows.shape[0] // CMP_BLOCK
    blocks = rows[:n * CMP_BLOCK].reshape(n, CMP_BLOCK, *rows.shape[1:])
    return jnp.einsum('nlcgd,cl->ncgd', blocks, pool)


def _nsa_attend(q, gates, q_pos, kv_cmp, fetch_sel, kv_win, w_pos, bias_tab):
    T = q.shape[0]
    hpg = NSA_HEADS // NSA_KV
    scale = NSA_DH ** -0.5
    qg = q.reshape(T, NSA_KV, hpg, NSA_DH)
    btab = bias_tab.astype(F32).reshape(NUM_BUCKETS, NSA_KV, hpg)
    nb = kv_cmp.shape[0]
    dist_c = q_pos[:, None] - (jnp.arange(nb) * CMP_BLOCK + CMP_BLOCK - 1)[None, :]
    s_c = (jnp.einsum('tgqd,ngd->tgqn', qg, kv_cmp[:, 0]).astype(F32) * scale
           + btab[_rel_bucket(dist_c)].transpose(0, 2, 3, 1))
    p_c = _masked_softmax(s_c, (dist_c >= 0)[:, None, None, :])
    o_c = jnp.einsum('tgqn,ngd->tgqd', p_c, kv_cmp[:, 1])
    cur = q_pos // CMP_BLOCK
    cand = (dist_c >= 0) & (jnp.arange(nb)[None, :] != cur[:, None])
    imp = jnp.where(cand[:, None, :], p_c.sum(2), -1.0)
    top_v, top_i = lax.top_k(imp, min(N_SEL - 1, nb))
    blk = jnp.concatenate([jnp.broadcast_to(cur[:, None, None], (T, NSA_KV, 1)), top_i], -1)
    blk_ok = jnp.concatenate([jnp.ones((T, NSA_KV, 1), bool), top_v >= 0], -1)
    pos_s = blk[..., None] * CMP_BLOCK + jnp.arange(CMP_BLOCK)
    k_s, v_s = fetch_sel(pos_s)
    dist_s = q_pos[:, None, None, None] - pos_s
    b_s = btab[_rel_bucket(dist_s), jnp.arange(NSA_KV)[None, :, None, None]]
    s_s = (jnp.einsum('tgqd,tgnkd->tgqnk', qg, k_s).astype(F32) * scale + b_s.transpose(0, 1, 4, 2, 3))
    m_s = ((dist_s >= 0) & blk_ok[..., None])[:, :, None]
    n_keys = blk.shape[-1] * CMP_BLOCK
    p_s = _masked_softmax(s_s.reshape(T, NSA_KV, hpg, n_keys), m_s.reshape(T, NSA_KV, 1, n_keys))
    o_s = jnp.einsum('tgqm,tgmd->tgqd', p_s, v_s.reshape(T, NSA_KV, n_keys, NSA_DH))
    dist_w = q_pos[:, None] - w_pos[None, :]
    s_w = (jnp.einsum('tgqd,wgd->tgqw', qg, kv_win[:, 0]).astype(F32) * scale
           + btab[_rel_bucket(dist_w)].transpose(0, 2, 3, 1))
    m_w = (dist_w >= 0) & (dist_w < WINDOW) & (w_pos >= 0)[None, :]
    p_w = _masked_softmax(s_w, m_w[:, None, None, :])
    o_w = jnp.einsum('tgqw,wgd->tgqd', p_w, kv_win[:, 1])
    g = jax.nn.sigmoid(gates.astype(F32)).reshape(T, NSA_KV, hpg, 3)
    o = g[..., 0:1] * o_c + g[..., 1:2] * o_s + g[..., 2:3] * o_w
    return o.reshape(T, NSA_WIDTH)


def _indexer_topk(q_i, w_i, k_i, q_pos, topk):
    s = jax.nn.relu(jnp.einsum('thd,sd->ths', q_i, k_i).astype(F32) * IDX_DH ** -0.5)
    score = jnp.einsum('ths,th->ts', s, w_i.astype(F32)) * IDX_HEADS ** -0.5
    score = jnp.where(jnp.arange(k_i.shape[0])[None, :] <= q_pos[:, None], score, NEG)
    return lax.top_k(score, topk)[1]


def _dsa_attend(q, q_pos, idx, k_sel, v_sel, bias_tab):
    T = q.shape[0]
    hpg = DSA_HEADS // DSA_KV
    qg = q.reshape(T, DSA_KV, hpg, DSA_DH)
    dist = q_pos[:, None] - idx
    bias = bias_tab.astype(F32).reshape(NUM_BUCKETS, DSA_KV, hpg)[_rel_bucket(dist)]
    s = (jnp.einsum('tgqd,tkgd->tgqk', qg, k_sel).astype(F32) * DSA_DH ** -0.5 + bias.transpose(0, 2, 3, 1))
    p = _masked_softmax(s, (dist >= 0)[:, None, None, :])
    o = jnp.einsum('tgqk,tkgd->tgqd', p, v_sel)
    return o.reshape(T, DSA_WIDTH)


def mixer_prompt(rel_bias, pool, q_n, cmp, sel, win, g_n, q_d, dkv, q_i, k_i, w_i):
    B, S = q_n.shape[:2]
    bias_n, bias_d = rel_bias[:, :NSA_HEADS], rel_bias[:, NSA_HEADS:]
    topk = min(DSA_TOPK_MAX, S // 4)
    garr = jnp.arange(NSA_KV)[None, :, None, None]

    def per_seq(args):
        qn, cm, se, wn, gn, qd, kv, qi, ki, wi = args
        kv_cmp = _compress(cm, pool)
        wn_pad = jnp.pad(wn, ((WINDOW, 0), (0, 0), (0, 0), (0, 0)))

        def fetch(pos):
            r = se[pos, :, garr]
            return r[..., 0, :], r[..., 1, :]

        def per_block(i):
            t0 = i * Q_BLOCK
            q_pos = t0 + jnp.arange(Q_BLOCK)
            blk = lambda a: lax.dynamic_slice_in_dim(a, t0, Q_BLOCK, 0)
            kv_win = lax.dynamic_slice_in_dim(wn_pad, t0, WINDOW + Q_BLOCK, 0)
            w_pos = t0 - WINDOW + jnp.arange(WINDOW + Q_BLOCK)
            o_n = _nsa_attend(blk(qn), blk(gn), q_pos, kv_cmp, fetch, kv_win, w_pos, bias_n)
            idx = _indexer_topk(blk(qi), blk(wi), ki, q_pos, topk)
            kv_sel = kv[idx]
            o_d = _dsa_attend(blk(qd), q_pos, idx, kv_sel[:, :, 0], kv_sel[:, :, 1], bias_d)
            return o_n, o_d

        o_n, o_d = lax.map(per_block, jnp.arange(S // Q_BLOCK))
        return o_n.reshape(S, NSA_WIDTH), o_d.reshape(S, DSA_WIDTH)

    return lax.map(per_seq, (q_n, cmp, sel, win, g_n, q_d, dkv, q_i, k_i, w_i))


def mixer_sample(cache_cmp, cache_sel, cache_win, cache_dkv, cache_idx, page_table, rel_bias, pool,
                 q_n, cmp, sel, win, g_n, q_d, dkv, q_i, k_i, w_i):
    B, T = q_n.shape[:2]
    bias_n, bias_d = rel_bias[:, :NSA_HEADS], rel_bias[:, NSA_HEADS:]
    past = page_table.shape[1] * PAGE_SIZE
    topk = min(DSA_TOPK_MAX, (past + T) // 4)
    q_pos = past + jnp.arange(T)
    garr = jnp.arange(NSA_KV)[None, :, None, None]

    def per_seq(args):
        qn, cm, se, wn, gn, qd, kv, qi, ki, wi, pt, wbuf = args
        past_cmp = cache_cmp[pt].reshape(past, 2, NSA_KV, NSA_DH)
        kv_cmp = jnp.concatenate([_compress(past_cmp, pool), _compress(cm, pool)], 0)

        def fetch(pos):
            pc = jnp.minimum(pos, past - 1)
            r_pool = cache_sel[pt[pc // PAGE_SIZE], pc % PAGE_SIZE, :, garr]
            r_new = se[jnp.clip(pos - past, 0, T - 1), :, garr]
            r = jnp.where((pos >= past)[..., None, None], r_new, r_pool)
            return r[..., 0, :], r[..., 1, :]

        kv_win = jnp.concatenate([wbuf, wn], 0)
        w_pos = past - wbuf.shape[0] + jnp.arange(kv_win.shape[0])
        o_n = _nsa_attend(qn, gn, q_pos, kv_cmp, fetch, kv_win, w_pos, bias_n)
        k_idx = jnp.concatenate([cache_idx[pt].reshape(past, IDX_DH), ki], 0)
        idx = _indexer_topk(qi, wi, k_idx, q_pos, topk)
        pc = jnp.minimum(idx, past - 1)
        kv_sel = jnp.where((idx >= past)[..., None, None, None], kv[jnp.clip(idx - past, 0, T - 1)],
                           cache_dkv[pt[pc // PAGE_SIZE], pc % PAGE_SIZE])
        o_d = _dsa_attend(qd, q_pos, idx, kv_sel[:, :, 0], kv_sel[:, :, 1], bias_d)
        return o_n, o_d, kv_win[T:]

    return lax.map(per_seq, (q_n, cmp, sel, win, g_n, q_d, dkv, q_i, k_i, w_i, page_table, cache_win))


def _project(h, w_q, w_kv, w_s, w_g):
    B, S, D = h.shape
    h2 = h.reshape(B * S, D)
    zq = matmul(h2, w_q, out_dtype=F32)
    zkv = matmul(h2, w_kv, out_dtype=F32)
    zs = matmul(h2, w_s, out_dtype=F32)
    zg = matmul(h2, w_g, out_dtype=F32)
    kvn = (B, S, 2, NSA_KV, NSA_DH)
    q_n = zq[:, :NSA_WIDTH].reshape(B, S, NSA_HEADS, NSA_DH)
    q_d = zq[:, NSA_WIDTH:NSA_WIDTH + DSA_WIDTH].reshape(B, S, DSA_HEADS, DSA_DH)
    q_i = zq[:, NSA_WIDTH + DSA_WIDTH:].reshape(B, S, IDX_HEADS, IDX_DH)
    cmp = zkv[:, :KV_W].reshape(kvn)
    sel = zkv[:, KV_W:2 * KV_W].reshape(kvn)
    win = zkv[:, 2 * KV_W:3 * KV_W].reshape(kvn)
    dkv = zkv[:, 3 * KV_W:].reshape(B, S, 2, DSA_KV, DSA_DH)
    k_i = zs[:, :IDX_DH].reshape(B, S, IDX_DH)
    w_i = zs[:, IDX_DH:IDX_DH + IDX_HEADS].reshape(B, S, IDX_HEADS)
    g_n = zs[:, IDX_DH + IDX_HEADS:].reshape(B, S, NSA_HEADS, 3)
    return (q_n, cmp, sel, win, g_n, q_d, dkv, q_i, k_i, w_i), zg


def kernel(x_prompt, x_sample, c_prompt, c_sample, cache_nsa_cmp, cache_nsa_sel, cache_nsa_win, cache_dsa_kv,
           cache_dsa_idx, page_table, rel_bias, w_mod, b_mod, g_pre_mix, g_post_mix, g_pre_ffn, g_post_ffn,
           w_in, cmp_pool, w_up_nsa, w_up_dsa, w_out, w_router, b_router, w_gu, b_gu, w_down, b_down):
    l = 0
    D = D_MODEL
    Bp, Sp, _ = x_prompt.shape
    Bs, Ss, _ = x_sample.shape

    c = jnp.concatenate([c_prompt, c_sample], 0)
    n_c = c.shape[0]
    c_pad = jnp.pad(jax.nn.silu(c), ((0, -n_c % 16), (0, 0))).astype(BF16)
    mod = matmul(c_pad, w_mod[l], b_mod[l], tn=1536)[:n_c]
    mod_p, mod_s = mod[:Bp], mod[Bp:]

    wi = w_in[l]
    seg = lambda k: wi[:, _OFF[k]:_OFF[k + 1]]
    w_q = jnp.concatenate([seg(0), seg(5), seg(7)], 1).astype(BF16)
    w_kv = jnp.concatenate([seg(1), seg(2), seg(3), seg(6)], 1).astype(BF16)
    w_s = jnp.concatenate([seg(8), seg(9), seg(4)], 1).astype(BF16)
    w_g = seg(10).astype(BF16)
    wun, wud, wo = w_up_nsa[l].astype(BF16), w_up_dsa[l].astype(BF16), w_out[l].astype(BF16)

    def mix_front(x, mod_g):
        sh1, sc1 = mod_g[:, :D], mod_g[:, D:2 * D]
        h = norm_mod(x, g_pre_mix[l], sc1, sh1)
        return _project(h, w_q, w_kv, w_s, w_g)

    def mix_back(x, mod_g, o_n, o_d, zg):
        B, S, _ = x.shape
        ga1, sh2, sc2 = mod_g[:, 2 * D:3 * D], mod_g[:, 3 * D:4 * D], mod_g[:, 4 * D:5 * D]
        u = merge_up(o_n.reshape(B * S, -1).astype(BF16), o_d.reshape(B * S, -1).astype(BF16), zg, wun, wud)
        return out_proj(u, x, wo, g_post_mix[l], g_pre_ffn[l], ga1, sc2, sh2, w_router[l], b_router[l])

    parts_p, zg_p = mix_front(x_prompt, mod_p)
    o_n, o_d = mixer_prompt(rel_bias, cmp_pool[l], *parts_p)
    x1_p, h_p, lg_p = mix_back(x_prompt, mod_p, o_n, o_d, zg_p)

    parts_s, zg_s = mix_front(x_sample, mod_s)
    o_n, o_d, new_win = mixer_sample(cache_nsa_cmp[l], cache_nsa_sel[l], cache_nsa_win[l], cache_dsa_kv[l],
                                     cache_dsa_idx[l], page_table, rel_bias, cmp_pool[l], *parts_s)
    x1_s, h_s, lg_s = mix_back(x_sample, mod_s, o_n, o_d, zg_s)

    Tp, Ts = Bp * Sp, Bs * Ss
    h_all = jnp.concatenate([h_p.reshape(Tp, D), h_s.reshape(Ts, D)], 0)
    logits = jnp.concatenate([lg_p.reshape(Tp, N_EXPERTS), lg_s.reshape(Ts, N_EXPERTS)], 0)
    probs, row_tok, blk_e, n_used, dest_of = moe_route(logits)
    out_rows = moe_experts(h_all[row_tok], blk_e, n_used, w_gu[l], b_gu[l], w_down[l], b_down[l])
    picked = out_rows[dest_of]
    y_p = moe_combine(picked[:Tp], probs[:Tp], x1_p, g_post_ffn[l], mod_p[:, 5 * D:])
    y_s = moe_combine(picked[Tp:], probs[Tp:], x1_s, g_post_ffn[l], mod_s[:, 5 * D:])

    st = lambda a: a[None]
    q_n, cmp_p, sel_p, win_p, g_n, q_d, dkv_p, q_i, ki_p, w_i = parts_p
    q_n, cmp_s, sel_s, win_s, g_n, q_d, dkv_s, q_i, ki_s, w_i = parts_s
    keep = min(WINDOW, Sp)
    return (y_p, y_s, st(cmp_p), st(sel_p), st(win_p[:, Sp - keep:]), st(dkv_p), st(ki_p),
            st(cmp_s), st(sel_s), st(new_win), st(dkv_s), st(ki_s))
```

```python
import functools
import math

import jax
import jax.numpy as jnp
import numpy as np
from jax import lax
from jax.experimental import pallas as pl
from jax.experimental.pallas import tpu as pltpu

F32 = jnp.float32
BF16 = jnp.bfloat16

D_MODEL = 2048
NSA_HEADS = 16
NSA_KV = 4
NSA_DH = 64
CMP_BLOCK = 64
N_SEL = 16
WINDOW = 512
DSA_HEADS = 16
DSA_KV = 4
DSA_DH = 64
IDX_HEADS = 16
IDX_DH = 64
DSA_TOPK_MAX = 256
NUM_BUCKETS = 32
REL_MAX_DIST = 128
N_EXPERTS = 32
TOP_K = 4
D_FF = 2048
SWIGLU_LIMIT = 7.0
SWIGLU_ALPHA = 1.702
PAGE_SIZE = 128
Q_BLOCK = 128
RMS_EPS = 1e-6
NEG = -1e30
NSA_WIDTH = NSA_HEADS * NSA_DH
DSA_WIDTH = DSA_HEADS * DSA_DH
KV_W = 2 * NSA_KV * NSA_DH

_SPLIT = (NSA_WIDTH, KV_W, KV_W, KV_W, 3 * NSA_HEADS, DSA_WIDTH, 2 * DSA_KV * DSA_DH,
          IDX_HEADS * IDX_DH, IDX_DH, IDX_HEADS, 2 * D_MODEL)
_OFF = tuple(int(v) for v in np.cumsum((0,) + _SPLIT))

MOE_TM = 256
VMEM_LIMIT = 48 * 1024 * 1024


def _cparams(sem):
    return pltpu.CompilerParams(dimension_semantics=sem, vmem_limit_bytes=VMEM_LIMIT)


def _norm_mod_kernel(x_ref, g_ref, sc_ref, sh_ref, o_ref):
    x = x_ref[0]
    y = x * lax.rsqrt(jnp.mean(x * x, axis=-1, keepdims=True) + RMS_EPS) * g_ref[...]
    o_ref[0] = (y * (1.0 + sc_ref[0]) + sh_ref[0]).astype(o_ref.dtype)


def norm_mod(x, g, scale, shift):
    B, S, D = x.shape
    ts = min(S, 512)
    row = pl.BlockSpec((1, ts, D), lambda b, s: (b, s, 0))
    per_b = pl.BlockSpec((1, 1, D), lambda b, s: (b, 0, 0))
    return pl.pallas_call(
        _norm_mod_kernel,
        out_shape=jax.ShapeDtypeStruct((B, S, D), BF16),
        grid=(B, S // ts),
        in_specs=[row, pl.BlockSpec((1, D), lambda b, s: (0, 0)), per_b, per_b],
        out_specs=row,
        compiler_params=_cparams(("parallel", "parallel")),
        name="norm_mod",
    )(x, g.reshape(1, D), scale.reshape(B, 1, D), shift.reshape(B, 1, D))


def _matmul_kernel(a_ref, w_ref, b_ref, o_ref):
    acc = jnp.dot(a_ref[...], w_ref[...].astype(BF16), preferred_element_type=F32)
    o_ref[...] = (acc + b_ref[...]).astype(o_ref.dtype)


def matmul(a, w, bias=None, out_dtype=F32, tm=512, tn=512):
    M, K = a.shape
    N = w.shape[1]
    tm, tn = min(tm, M), min(tn, N)
    assert M % tm == 0 and N % tn == 0
    if bias is None:
        bias = jnp.zeros((N,), F32)
    return pl.pallas_call(
        _matmul_kernel,
        out_shape=jax.ShapeDtypeStruct((M, N), out_dtype),
        grid=(M // tm, N // tn),
        in_specs=[pl.BlockSpec((tm, K), lambda i, j: (i, 0)),
                  pl.BlockSpec((K, tn), lambda i, j: (0, j)),
                  pl.BlockSpec((1, tn), lambda i, j: (0, j))],
        out_specs=pl.BlockSpec((tm, tn), lambda i, j: (i, j)),
        compiler_params=_cparams(("parallel", "parallel")),
        name="matmul",
    )(a, w, bias.reshape(1, N))


def _merge_kernel(on_ref, od_ref, ga_ref, gb_ref, wn_ref, wd_ref, o_ref):
    a = jnp.dot(on_ref[...], wn_ref[...], preferred_element_type=F32)
    b = jnp.dot(od_ref[...], wd_ref[...], preferred_element_type=F32)
    o_ref[...] = (jax.nn.sigmoid(ga_ref[...]) * a + jax.nn.sigmoid(gb_ref[...]) * b).astype(o_ref.dtype)


def merge_up(o_n, o_d, zg, w_up_nsa, w_up_dsa, tm=512, tn=512):
    M = o_n.shape[0]
    D = w_up_nsa.shape[1]
    tm = min(tm, M)
    nj = D // tn
    return pl.pallas_call(
        _merge_kernel,
        out_shape=jax.ShapeDtypeStruct((M, D), BF16),
        grid=(M // tm, nj),
        in_specs=[pl.BlockSpec((tm, o_n.shape[1]), lambda i, j: (i, 0)),
                  pl.BlockSpec((tm, o_d.shape[1]), lambda i, j: (i, 0)),
                  pl.BlockSpec((tm, tn), lambda i, j: (i, j)),
                  pl.BlockSpec((tm, tn), lambda i, j: (i, j + nj)),
                  pl.BlockSpec((o_n.shape[1], tn), lambda i, j: (0, j)),
                  pl.BlockSpec((o_d.shape[1], tn), lambda i, j: (0, j))],
        out_specs=pl.BlockSpec((tm, tn), lambda i, j: (i, j)),
        compiler_params=_cparams(("parallel", "parallel")),
        name="merge_up",
    )(o_n, o_d, zg, zg, w_up_nsa, w_up_dsa)


def _rms(x, g):
    return x * lax.rsqrt(jnp.mean(x * x, axis=-1, keepdims=True) + RMS_EPS) * g


def _out_proj_kernel(u_ref, x_ref, w_ref, gpost_ref, gpre_ref, ga_ref, sc_ref, sh_ref, wr_ref, br_ref,
                     x1_ref, h_ref, lg_ref):
    m = jnp.dot(u_ref[0], w_ref[...], preferred_element_type=F32)
    x1 = x_ref[0] + ga_ref[0] * _rms(m, gpost_ref[...])
    x1_ref[0] = x1
    h = _rms(x1, gpre_ref[...]) * (1.0 + sc_ref[0]) + sh_ref[0]
    h_ref[0] = h.astype(h_ref.dtype)
    lg_ref[0] = jnp.dot(h, wr_ref[...], preferred_element_type=F32,
                        precision=lax.Precision.HIGHEST) + br_ref[...]


def out_proj(u, x, w_out, g_post, g_pre, gate, scale, shift, w_router, b_router):
    B, S, D = x.shape
    ts = min(S, 256)
    E = w_router.shape[1]
    row = lambda d: pl.BlockSpec((1, ts, d), lambda b, s: (b, s, 0))
    per_b = pl.BlockSpec((1, 1, D), lambda b, s: (b, 0, 0))
    vec = lambda d: pl.BlockSpec((1, d), lambda b, s: (0, 0))
    return pl.pallas_call(
        _out_proj_kernel,
        out_shape=(jax.ShapeDtypeStruct((B, S, D), F32), jax.ShapeDtypeStruct((B, S, D), BF16),
                   jax.ShapeDtypeStruct((B, S, E), F32)),
        grid=(B, S // ts),
        in_specs=[row(D), row(D), pl.BlockSpec((D, D), lambda b, s: (0, 0)), vec(D), vec(D),
                  per_b, per_b, per_b, pl.BlockSpec((D, E), lambda b, s: (0, 0)), vec(E)],
        out_specs=(row(D), row(D), row(E)),
        compiler_params=_cparams(("parallel", "parallel")),
        name="out_proj",
    )(u.reshape(B, S, D), x, w_out, g_post.reshape(1, D), g_pre.reshape(1, D), gate.reshape(B, 1, D),
      scale.reshape(B, 1, D), shift.reshape(B, 1, D), w_router, b_router.reshape(1, E))


def _expert_changed(be_ref, b):
    return jnp.logical_or(b == 0, be_ref[b] != be_ref[jnp.maximum(b - 1, 0)])


def _moe_up_kernel(be_ref, nu_ref, x_ref, wg_ref, wl_ref, bg_ref, bl_ref, o_ref, wg_s, wl_s):
    b = pl.program_id(1)

    @pl.when(_expert_changed(be_ref, b))
    def _():
        wg_s[...] = wg_ref[0].astype(BF16)
        wl_s[...] = wl_ref[0].astype(BF16)

    @pl.when(b < nu_ref[0])
    def _():
        x = x_ref[...]
        g = jnp.dot(x, wg_s[...], preferred_element_type=F32) + bg_ref[0]
        lin = jnp.dot(x, wl_s[...], preferred_element_type=F32) + bl_ref[0]
        g = jnp.minimum(g, SWIGLU_LIMIT)
        lin = jnp.clip(lin, -SWIGLU_LIMIT, SWIGLU_LIMIT)
        o_ref[...] = (g * jax.nn.sigmoid(SWIGLU_ALPHA * g) * (lin + 1.0)).astype(o_ref.dtype)

    @pl.when(b >= nu_ref[0])
    def _():
        o_ref[...] = jnp.zeros_like(o_ref)


def _moe_down_kernel(be_ref, nu_ref, a_ref, w_ref, bias_ref, o_ref, w_s):
    b = pl.program_id(1)

    @pl.when(_expert_changed(be_ref, b))
    def _():
        w_s[...] = w_ref[0].astype(BF16)

    @pl.when(b < nu_ref[0])
    def _():
        o_ref[...] = jnp.dot(a_ref[...], w_s[...], preferred_element_type=F32) + bias_ref[0]

    @pl.when(b >= nu_ref[0])
    def _():
        o_ref[...] = jnp.zeros_like(o_ref)


def moe_experts(xs, blk_e, n_used, w_gu, b_gu, w_down, b_down, tf=512, tn=512):
    R, D = xs.shape
    nb = R // MOE_TM
    E = w_gu.shape[0]
    nf = D_FF // tf
    act = pl.pallas_call(
        _moe_up_kernel,
        out_shape=jax.ShapeDtypeStruct((R, D_FF), BF16),
        grid_spec=pltpu.PrefetchScalarGridSpec(
            num_scalar_prefetch=2,
            grid=(nf, nb),
            in_specs=[pl.BlockSpec((MOE_TM, D), lambda f, b, be, nu: (b, 0)),
                      pl.BlockSpec((1, D, tf), lambda f, b, be, nu: (be[b], 0, f)),
                      pl.BlockSpec((1, D, tf), lambda f, b, be, nu: (be[b], 0, f + nf)),
                      pl.BlockSpec((1, 1, tf), lambda f, b, be, nu: (be[b], 0, f)),
                      pl.BlockSpec((1, 1, tf), lambda f, b, be, nu: (be[b], 0, f + nf))],
            out_specs=pl.BlockSpec((MOE_TM, tf), lambda f, b, be, nu: (b, f)),
            scratch_shapes=[pltpu.VMEM((D, tf), BF16), pltpu.VMEM((D, tf), BF16)]),
        compiler_params=_cparams(("arbitrary", "arbitrary")),
        name="moe_up",
    )(blk_e, n_used, xs, w_gu, w_gu, b_gu.reshape(E, 1, 2 * D_FF), b_gu.reshape(E, 1, 2 * D_FF))
    nn = D // tn
    return pl.pallas_call(
        _moe_down_kernel,
        out_shape=jax.ShapeDtypeStruct((R, D), F32),
        grid_spec=pltpu.PrefetchScalarGridSpec(
            num_scalar_prefetch=2,
            grid=(nn, nb),
            in_specs=[pl.BlockSpec((MOE_TM, D_FF), lambda j, b, be, nu: (b, 0)),
                      pl.BlockSpec((1, D_FF, tn), lambda j, b, be, nu: (be[b], 0, j)),
                      pl.BlockSpec((1, 1, tn), lambda j, b, be, nu: (be[b], 0, j))],
            out_specs=pl.BlockSpec((MOE_TM, tn), lambda j, b, be, nu: (b, j)),
            scratch_shapes=[pltpu.VMEM((D_FF, tn), BF16)]),
        compiler_params=_cparams(("arbitrary", "arbitrary")),
        name="moe_down",
    )(blk_e, n_used, act, w_down, b_down.reshape(E, 1, D))


def _combine_kernel(r_ref, p_ref, x_ref, g_ref, ga_ref, o_ref):
    p = p_ref[0]
    f = r_ref[0, :, 0, :] * p[:, 0:1]
    for k in range(1, TOP_K):
        f = f + r_ref[0, :, k, :] * p[:, k:k + 1]
    o_ref[0] = x_ref[0] + ga_ref[0] * _rms(f, g_ref[...])


def moe_combine(rows, probs, x, g_post, gate):
    B, S, D = x.shape
    ts = min(S, 128)
    return pl.pallas_call(
        _combine_kernel,
        out_shape=jax.ShapeDtypeStruct((B, S, D), F32),
        grid=(B, S // ts),
        in_specs=[pl.BlockSpec((1, ts, TOP_K, D), lambda b, s: (b, s, 0, 0)),
                  pl.BlockSpec((1, ts, TOP_K), lambda b, s: (b, s, 0)),
                  pl.BlockSpec((1, ts, D), lambda b, s: (b, s, 0)),
                  pl.BlockSpec((1, D), lambda b, s: (0, 0)),
                  pl.BlockSpec((1, 1, D), lambda b, s: (b, 0, 0))],
        out_specs=pl.BlockSpec((1, ts, D), lambda b, s: (b, s, 0)),
        compiler_params=_cparams(("parallel", "parallel")),
        name="moe_combine",
    )(rows.reshape(B, S, TOP_K, D), probs.reshape(B, S, TOP_K), x, g_post.reshape(1, D), gate.reshape(B, 1, D))


def moe_route(logits):
    T = logits.shape[0]
    top_v, top_e = lax.top_k(logits, TOP_K)
    probs = jax.nn.softmax(top_v, axis=-1)
    flat_e = top_e.reshape(-1)
    n_assign = flat_e.shape[0]
    order = jnp.argsort(flat_e)
    e_sorted = flat_e[order]
    counts = jnp.bincount(flat_e, length=N_EXPERTS)
    padded = (counts + MOE_TM - 1) // MOE_TM * MOE_TM
    pad_end = jnp.cumsum(padded)
    start = jnp.cumsum(counts) - counts
    dest = (pad_end - padded)[e_sorted] + jnp.arange(n_assign) - start[e_sorted]
    n_blocks = -(-n_assign // MOE_TM) + N_EXPERTS
    row_tok = jnp.zeros((n_blocks * MOE_TM,), jnp.int32).at[dest].set((order // TOP_K).astype(jnp.int32))
    blk_e = jnp.minimum(jnp.searchsorted(pad_end, jnp.arange(n_blocks) * MOE_TM, side='right'),
                        N_EXPERTS - 1).astype(jnp.int32)
    dest_of = jnp.zeros((n_assign,), jnp.int32).at[order].set(dest.astype(jnp.int32))
    n_used = (pad_end[-1] // MOE_TM).astype(jnp.int32).reshape(1)
    return probs, row_tok, blk_e, n_used, dest_of.reshape(T, TOP_K)


def _rel_bucket(dist):
    n = jnp.maximum(dist, 0)
    exact = NUM_BUCKETS // 2
    log_b = exact + (jnp.log(jnp.maximum(n, 1).astype(F32) / exact)
                     / math.log(REL_MAX_DIST / exact) * (NUM_BUCKETS - exact)).astype(jnp.int32)
    return jnp.where(n < exact, n, jnp.minimum(log_b, NUM_BUCKETS - 1))


def _masked_softmax(s, mask):
    s = jnp.where(mask, s, NEG)
    e = jnp.where(mask, jnp.exp(s - jnp.max(s, -1, keepdims=True)), 0.0)
    return e / jnp.maximum(jnp.sum(e, -1, keepdims=True), 1e-30)


def _compress(rows, pool):
    n = rows.shape[0] // CMP_BLOCK
    blocks = rows[:n * CMP_BLOCK].reshape(n, CMP_BLOCK, *rows.shape[1:])
    return jnp.einsum('nlcgd,cl->ncgd', blocks, pool)


def _nsa_attend(q, gates, q_pos, kv_cmp, fetch_sel, kv_win, w_pos, bias_tab):
    T = q.shape[0]
    hpg = NSA_HEADS // NSA_KV
    scale = NSA_DH ** -0.5
    qg = q.reshape(T, NSA_KV, hpg, NSA_DH)
    btab = bias_tab.astype(F32).reshape(NUM_BUCKETS, NSA_KV, hpg)
    nb = kv_cmp.shape[0]
    dist_c = q_pos[:, None] - (jnp.arange(nb) * CMP_BLOCK + CMP_BLOCK - 1)[None, :]
    s_c = (jnp.einsum('tgqd,ngd->tgqn', qg, kv_cmp[:, 0]).astype(F32) * scale
           + btab[_rel_bucket(dist_c)].transpose(0, 2, 3, 1))
    p_c = _masked_softmax(s_c, (dist_c >= 0)[:, None, None, :])
    o_c = jnp.einsum('tgqn,ngd->tgqd', p_c, kv_cmp[:, 1])
    cur = q_pos // CMP_BLOCK
    cand = (dist_c >= 0) & (jnp.arange(nb)[None, :] != cur[:, None])
    imp = jnp.where(cand[:, None, :], p_c.sum(2), -1.0)
    top_v, top_i = lax.top_k(imp, min(N_SEL - 1, nb))
    blk = jnp.concatenate([jnp.broadcast_to(cur[:, None, None], (T, NSA_KV, 1)), top_i], -1)
    blk_ok = jnp.concatenate([jnp.ones((T, NSA_KV, 1), bool), top_v >= 0], -1)
    pos_s = blk[..., None] * CMP_BLOCK + jnp.arange(CMP_BLOCK)
    k_s, v_s = fetch_sel(pos_s)
    dist_s = q_pos[:, None, None, None] - pos_s
    b_s = btab[_rel_bucket(dist_s), jnp.arange(NSA_KV)[None, :, None, None]]
    s_s = (jnp.einsum('tgqd,tgnkd->tgqnk', qg, k_s).astype(F32) * scale + b_s.transpose(0, 1, 4, 2, 3))
    m_s = ((dist_s >= 0) & blk_ok[..., None])[:, :, None]
    n_keys = blk.shape[-1] * CMP_BLOCK
    p_s = _masked_softmax(s_s.reshape(T, NSA_KV, hpg, n_keys), m_s.reshape(T, NSA_KV, 1, n_keys))
    o_s = jnp.einsum('tgqm,tgmd->tgqd', p_s, v_s.reshape(T, NSA_KV, n_keys, NSA_DH))
    dist_w = q_pos[:, None] - w_pos[None, :]
    s_w = (jnp.einsum('tgqd,wgd->tgqw', qg, kv_win[:, 0]).astype(F32) * scale
           + btab[_rel_bucket(dist_w)].transpose(0, 2, 3, 1))
    m_w = (dist_w >= 0) & (dist_w < WINDOW) & (w_pos >= 0)[None, :]
    p_w = _masked_softmax(s_w, m_w[:, None, None, :])
    o_w = jnp.einsum('tgqw,wgd->tgqd', p_w, kv_win[:, 1])
    g = jax.nn.sigmoid(gates.astype(F32)).reshape(T, NSA_KV, hpg, 3)
    o = g[..., 0:1] * o_c + g[..., 1:2] * o_s + g[..., 2:3] * o_w
    return o.reshape(T, NSA_WIDTH)


def _indexer_topk(q_i, w_i, k_i, q_pos, topk):
    s = jax.nn.relu(jnp.einsum('thd,sd->ths', q_i, k_i).astype(F32) * IDX_DH ** -0.5)
    score = jnp.einsum('ths,th->ts', s, w_i.astype(F32)) * IDX_HEADS ** -0.5
    score = jnp.where(jnp.arange(k_i.shape[0])[None, :] <= q_pos[:, None], score, NEG)
    return lax.top_k(score, topk)[1]


def _dsa_attend(q, q_pos, idx, k_sel, v_sel, bias_tab):
    T = q.shape[0]
    hpg = DSA_HEADS // DSA_KV
    qg = q.reshape(T, DSA_KV, hpg, DSA_DH)
    dist = q_pos[:, None] - idx
    bias = bias_tab.astype(F32).reshape(NUM_BUCKETS, DSA_KV, hpg)[_rel_bucket(dist)]
    s = (jnp.einsum('tgqd,tkgd->tgqk', qg, k_sel).astype(F32) * DSA_DH ** -0.5 + bias.transpose(0, 2, 3, 1))
    p = _masked_softmax(s, (dist >= 0)[:, None, None, :])
    o = jnp.einsum('tgqk,tkgd->tgqd', p, v_sel)
    return o.reshape(T, DSA_WIDTH)


TQ = Q_BLOCK
HPG = NSA_HEADS // NSA_KV
INT_MIN = -2 ** 31


def _flash_step(q2, k, v, bias, mask, scale, carry):
    m, l, acc = carry
    nk = k.shape[0]
    s = lax.dot_general(q2, k, (((1,), (1,)), ((), ())), preferred_element_type=F32)
    s = s.reshape(HPG, TQ, nk) * scale + bias
    s = jnp.where(mask[None], s, NEG)
    m_new = jnp.maximum(m, jnp.max(s, axis=-1, keepdims=True))
    p = jnp.where(mask[None], jnp.exp(s - m_new), 0.0)
    alpha = jnp.exp(m - m_new)
    l = alpha * l + jnp.sum(p, axis=-1, keepdims=True)
    pv = jnp.dot(p.reshape(HPG * TQ, nk).astype(BF16), v, preferred_element_type=F32)
    acc = alpha * acc + pv.reshape(HPG, TQ, v.shape[1])
    return m_new, l, acc


def _flash_init(dh):
    return (jnp.full((HPG, TQ, 1), NEG, F32), jnp.zeros((HPG, TQ, 1), F32), jnp.zeros((HPG, TQ, dh), F32))


def _flash_out(carry):
    m, l, acc = carry
    return acc / jnp.maximum(l, 1e-30)


def _causal_blocks(i, q2, k_ref, v_ref, mask_ref, tiles_ref, far_ref, scale):
    dh = k_ref.shape[-1]
    a = lax.broadcasted_iota(jnp.int32, (TQ, TQ), 0)
    b = lax.broadcasted_iota(jnp.int32, (TQ, TQ), 1)

    def blk(ref, jb):
        return ref[pl.ds(pl.multiple_of(jb * TQ, TQ), TQ), :]

    carry = _flash_step(q2, blk(k_ref, i), blk(v_ref, i), tiles_ref[:, 0],
                        jnp.logical_and(mask_ref[i] > 0.5, a >= b), scale, _flash_init(dh))
    jp = jnp.maximum(i - 1, 0)
    carry = _flash_step(q2, blk(k_ref, jp), blk(v_ref, jp), tiles_ref[:, 1],
                        jnp.logical_and(mask_ref[jp] > 0.5, i >= 1), scale, carry)
    far = far_ref[...]

    def body(jb, c):
        return _flash_step(q2, blk(k_ref, jb), blk(v_ref, jb), far, mask_ref[jb] > 0.5, scale, c)

    return lax.fori_loop(0, jnp.maximum(i - 1, 0), body, carry)


def _nsa_prompt_kernel(q_ref, kc_ref, vc_ref, ks_ref, vs_ref, kw_ref, vw_ref, pool_ref, bc_ref, tiles_ref,
                       far_ref, gate_ref, o_ref, kcs, vcs, exp_s, mask_s):
    i = pl.program_id(2)
    S = ks_ref.shape[3]
    nb = S // CMP_BLOCK
    scale = NSA_DH ** -0.5

    @pl.when(i == 0)
    def _():
        kr = kc_ref[0, 0, 0].reshape(nb, CMP_BLOCK, NSA_DH) * pool_ref[0][None]
        vr = vc_ref[0, 0, 0].reshape(nb, CMP_BLOCK, NSA_DH) * pool_ref[1][None]
        kcs[...] = jnp.sum(kr, axis=1).astype(BF16)
        vcs[...] = jnp.sum(vr, axis=1).astype(BF16)
        n_i = lax.broadcasted_iota(jnp.int32, (nb, S), 0)
        s_i = lax.broadcasted_iota(jnp.int32, (nb, S), 1)
        exp_s[...] = jnp.where(s_i // CMP_BLOCK == n_i, 1.0, 0.0).astype(BF16)

    q2 = q_ref[0].reshape(HPG * TQ, NSA_DH)
    t0 = i * TQ

    s_c = lax.dot_general(q2, kcs[...], (((1,), (1,)), ((), ())), preferred_element_type=F32)
    s_c = s_c.reshape(HPG, TQ, nb) * scale + bc_ref[...]
    t_i = t0 + lax.broadcasted_iota(jnp.int32, (TQ, nb), 0)
    n_i = lax.broadcasted_iota(jnp.int32, (TQ, nb), 1)
    vis = t_i >= n_i * CMP_BLOCK + (CMP_BLOCK - 1)
    s_c = jnp.where(vis[None], s_c, NEG)
    e_c = jnp.where(vis[None], jnp.exp(s_c - jnp.max(s_c, axis=-1, keepdims=True)), 0.0)
    p_c = e_c / jnp.maximum(jnp.sum(e_c, axis=-1, keepdims=True), 1e-30)
    o_c = jnp.dot(p_c.reshape(HPG * TQ, nb).astype(BF16), vcs[...], preferred_element_type=F32)

    cur = n_i == t_i // CMP_BLOCK
    work = jnp.where(jnp.logical_and(vis, jnp.logical_not(cur)), jnp.sum(p_c, axis=0), -1.0)
    sel = jnp.where(cur, 1.0, 0.0)
    n_f = n_i.astype(F32)
    for _ in range(min(N_SEL - 1, nb)):
        mx = jnp.max(work, axis=-1, keepdims=True)
        first = jnp.min(jnp.where(work == mx, n_f, float(nb)), axis=-1, keepdims=True)
        pick = n_f == first
        sel = jnp.where(jnp.logical_and(pick, mx >= 0.0), 1.0, sel)
        work = jnp.where(pick, -2.0, work)
    selx = jnp.dot(sel.astype(BF16), exp_s[...], preferred_element_type=F32)
    for jb in range(S // TQ):
        mask_s[jb] = selx[:, jb * TQ:(jb + 1) * TQ]

    o_s = _flash_out(_causal_blocks(i, q2, ks_ref.at[0, 0, 0], vs_ref.at[0, 0, 0], mask_s, tiles_ref, far_ref,
                                    scale))

    a = lax.broadcasted_iota(jnp.int32, (TQ, TQ), 0)
    b = lax.broadcasted_iota(jnp.int32, (TQ, TQ), 1)
    kw, vw = kw_ref.at[0, 0, 0], vw_ref.at[0, 0, 0]
    carry = _flash_init(NSA_DH)
    n_back = WINDOW // TQ
    for k in range(n_back + 1):
        jb = jnp.maximum(i - k, 0)
        ok = i >= k
        if k == 0:
            bias, mask = tiles_ref[:, 0], a >= b
        elif k == 1:
            bias, mask = tiles_ref[:, 1], jnp.broadcast_to(ok, (TQ, TQ))
        elif k < n_back:
            bias, mask = far_ref[...], jnp.broadcast_to(ok, (TQ, TQ))
        else:
            bias, mask = far_ref[...], jnp.logical_and(b > a, ok)
        start = pl.multiple_of(jb * TQ, TQ)
        carry = _flash_step(q2, kw[pl.ds(start, TQ), :], vw[pl.ds(start, TQ), :], bias, mask, scale, carry)
    o_w = _flash_out(carry)

    g = jax.nn.sigmoid(gate_ref[0])
    o = g[..., 0:1] * o_c.reshape(HPG, TQ, NSA_DH) + g[..., 1:2] * o_s + g[..., 2:3] * o_w
    for j in range(HPG):
        o_ref[0, :, j * NSA_DH:(j + 1) * NSA_DH] = o[j].astype(o_ref.dtype)


def _bias_tables(rel_bias, S):
    assert 2 * TQ - (TQ - 1) >= REL_MAX_DIST
    nb = S // CMP_BLOCK
    t = jnp.arange(S)[:, None]
    bc = rel_bias[_rel_bucket(t - (jnp.arange(nb)[None, :] * CMP_BLOCK + CMP_BLOCK - 1))]
    d = jnp.arange(2)[:, None, None] * TQ + jnp.arange(TQ)[None, :, None] - jnp.arange(TQ)[None, None, :]
    tiles = rel_bias[_rel_bucket(d)]
    far = jnp.broadcast_to(rel_bias[NUM_BUCKETS - 1][:, None, None], (rel_bias.shape[1], 1, TQ))
    return bc.transpose(2, 0, 1), tiles.transpose(3, 0, 1, 2), far


def nsa_prompt(q_n, cmp_t, sel_t, win_t, pool, g_n, bias_c, tiles, far):
    B, H, S, dh = q_n.shape
    G = NSA_KV
    nb = S // CMP_BLOCK
    kv = lambda c: pl.BlockSpec((1, 1, 1, S, dh), lambda b, g, i: (b, c, g, 0, 0))
    return pl.pallas_call(
        _nsa_prompt_kernel,
        out_shape=jax.ShapeDtypeStruct((B, S, H * dh), BF16),
        grid=(B, G, S // TQ),
        in_specs=[pl.BlockSpec((1, HPG, TQ, dh), lambda b, g, i: (b, g, i, 0)),
                  kv(0), kv(1), kv(0), kv(1), kv(0), kv(1),
                  pl.BlockSpec((2, CMP_BLOCK, dh), lambda b, g, i: (0, 0, 0)),
                  pl.BlockSpec((HPG, TQ, nb), lambda b, g, i: (g, i, 0)),
                  pl.BlockSpec((HPG, 2, TQ, TQ), lambda b, g, i: (g, 0, 0, 0)),
                  pl.BlockSpec((HPG, 1, TQ), lambda b, g, i: (g, 0, 0)),
                  pl.BlockSpec((1, HPG, TQ, 3), lambda b, g, i: (b, g, i, 0))],
        out_specs=pl.BlockSpec((1, TQ, HPG * dh), lambda b, g, i: (b, i, g)),
        scratch_shapes=[pltpu.VMEM((nb, dh), BF16), pltpu.VMEM((nb, dh), BF16), pltpu.VMEM((nb, S), BF16),
                        pltpu.VMEM((S // TQ, TQ, TQ), F32)],
        compiler_params=_cparams(("parallel", "parallel", "arbitrary")),
        name="nsa_prompt",
    )(q_n, cmp_t, cmp_t, sel_t, sel_t, win_t, win_t,
      jnp.broadcast_to(pool[:, :, None], (2, CMP_BLOCK, dh)), bias_c, tiles, far, g_n)


def _dsa_prompt_kernel(qi_ref, ki_ref, wi_ref, qd_ref, kd_ref, vd_ref, tiles_ref, far_ref, o_ref,
                       key_s, mask_s, *, topk):
    i = pl.program_id(1)
    S = ki_ref.shape[1]
    nkb = S // TQ
    t0 = i * TQ
    a = lax.broadcasted_iota(jnp.int32, (TQ, TQ), 0)
    b = lax.broadcasted_iota(jnp.int32, (TQ, TQ), 1)

    w = wi_ref[0]
    for jb in range(nkb):
        @pl.when(jb <= i)
        def _():
            kk = ki_ref[0, jb * TQ:(jb + 1) * TQ, :]
            acc = jnp.zeros((TQ, TQ), F32)
            for h in range(IDX_HEADS):
                s = lax.dot_general(qi_ref[0, h], kk, (((1,), (1,)), ((), ())), preferred_element_type=F32)
                acc = acc + jnp.maximum(s * IDX_DH ** -0.5, 0.0) * w[:, h:h + 1]
            score = acc * IDX_HEADS ** -0.5
            score = jnp.where(jb * TQ + b <= t0 + a, score, NEG) + 0.0
            bits = pltpu.bitcast(score, jnp.int32)
            key_s[jb] = jnp.where(bits < 0, bits ^ 0x7FFFFFFF, bits)

        @pl.when(jb > i)
        def _():
            bits = pltpu.bitcast(jnp.full((TQ, TQ), NEG, F32), jnp.int32)
            key_s[jb] = bits ^ 0x7FFFFFFF

    def count(pred):
        c = jnp.sum(jnp.where(pred, 1.0, 0.0), axis=0)
        return jnp.sum(c, axis=-1, keepdims=True)[None]

    def bit_body(n, pat):
        cand = pat | lax.shift_left(jnp.int32(1), 31 - n)
        return jnp.where(count(key_s[...] >= (cand ^ INT_MIN)) >= topk, cand, pat)

    thr = lax.fori_loop(0, 32, bit_body, jnp.zeros((1, TQ, 1), jnp.int32)) ^ INT_MIN
    keys = key_s[...]
    need = topk - count(keys > thr)
    pos = (lax.broadcasted_iota(jnp.int32, (nkb, TQ, TQ), 0) * TQ
           + lax.broadcasted_iota(jnp.int32, (nkb, TQ, TQ), 2))
    tie = keys == thr

    def tie_body(n, last):
        cand = last | lax.shift_left(jnp.int32(1), (S - 1).bit_length() - 1 - n)
        return jnp.where(count(jnp.logical_and(tie, pos < cand)) < need, cand, last)

    last = lax.fori_loop(0, (S - 1).bit_length(), tie_body, jnp.zeros((1, TQ, 1), jnp.int32))
    chosen = jnp.logical_or(keys > thr, jnp.logical_and(tie, pos <= last))
    mask_s[...] = jnp.where(chosen, 1.0, 0.0)

    scale = DSA_DH ** -0.5
    for g in range(DSA_KV):
        q2 = qd_ref[0, g * HPG:(g + 1) * HPG].reshape(HPG * TQ, DSA_DH)
        o = _flash_out(_causal_blocks(i, q2, kd_ref.at[0, 0, g], vd_ref.at[0, 0, g], mask_s,
                                      tiles_ref.at[g * HPG:(g + 1) * HPG], far_ref.at[g * HPG:(g + 1) * HPG],
                                      scale))
        for j in range(HPG):
            h = g * HPG + j
            o_ref[0, :, h * DSA_DH:(h + 1) * DSA_DH] = o[j].astype(o_ref.dtype)


def dsa_prompt(q_i, k_i, w_i, q_d, dkv_t, tiles, far):
    B, H, S, dh = q_d.shape
    G = DSA_KV
    topk = min(DSA_TOPK_MAX, S // 4)
    heads = pl.BlockSpec((1, H, TQ, dh), lambda b, i: (b, 0, i, 0))
    kv = lambda c: pl.BlockSpec((1, 1, G, S, dh), lambda b, i: (b, c, 0, 0, 0))
    return pl.pallas_call(
        functools.partial(_dsa_prompt_kernel, topk=topk),
        out_shape=jax.ShapeDtypeStruct((B, S, H * dh), BF16),
        grid=(B, S // TQ),
        in_specs=[heads,
                  pl.BlockSpec((1, S, k_i.shape[2]), lambda b, i: (b, 0, 0)),
                  pl.BlockSpec((1, TQ, H), lambda b, i: (b, i, 0)),
                  heads, kv(0), kv(1),
                  pl.BlockSpec((H, 2, TQ, TQ), lambda b, i: (0, 0, 0, 0)),
                  pl.BlockSpec((H, 1, TQ), lambda b, i: (0, 0, 0))],
        out_specs=pl.BlockSpec((1, TQ, H * dh), lambda b, i: (b, i, 0)),
        scratch_shapes=[pltpu.VMEM((S // TQ, TQ, TQ), jnp.int32), pltpu.VMEM((S // TQ, TQ, TQ), F32)],
        compiler_params=_cparams(("parallel", "arbitrary")),
        name="dsa_prompt",
    )(q_i, k_i, w_i, q_d, dkv_t, dkv_t, tiles, far)


def mixer_prompt(rel_bias, pool, q_n, cmp, sel, win, g_n, q_d, dkv, q_i, k_i, w_i):
    B, S = q_n.shape[:2]
    heads_first = lambda a: a.transpose(0, 2, 1, 3)
    kv_t = lambda a, dt: a.transpose(0, 2, 3, 1, 4).astype(dt)
    bias_c, tiles, far = _bias_tables(rel_bias, S)
    o_n = nsa_prompt(heads_first(q_n).astype(BF16), kv_t(cmp, F32), kv_t(sel, BF16), kv_t(win, BF16), pool,
                     heads_first(g_n), bias_c[:NSA_HEADS], tiles[:NSA_HEADS], far[:NSA_HEADS])
    o_d = dsa_prompt(heads_first(q_i).astype(BF16), k_i.astype(BF16), w_i, heads_first(q_d).astype(BF16),
                     kv_t(dkv, BF16), tiles[NSA_HEADS:], far[NSA_HEADS:])
    return o_n, o_d


def mixer_sample(cache_cmp, cache_sel, cache_win, cache_dkv, cache_idx, page_table, rel_bias, pool,
                 q_n, cmp, sel, win, g_n, q_d, dkv, q_i, k_i, w_i):
    B, T = q_n.shape[:2]
    bias_n, bias_d = rel_bias[:, :NSA_HEADS], rel_bias[:, NSA_HEADS:]
    past = page_table.shape[1] * PAGE_SIZE
    topk = min(DSA_TOPK_MAX, (past + T) // 4)
    q_pos = past + jnp.arange(T)
    garr = jnp.arange(NSA_KV)[None, :, None, None]

    def per_seq(args):
        qn, cm, se, wn, gn, qd, kv, qi, ki, wi, pt, wbuf = args
        past_cmp = cache_cmp[pt].reshape(past, 2, NSA_KV, NSA_DH)
        kv_cmp = jnp.concatenate([_compress(past_cmp, pool), _compress(cm, pool)], 0)

        def fetch(pos):
            pc = jnp.minimum(pos, past - 1)
            r_pool = cache_sel[pt[pc // PAGE_SIZE], pc % PAGE_SIZE, :, garr]
            r_new = se[jnp.clip(pos - past, 0, T - 1), :, garr]
            r = jnp.where((pos >= past)[..., None, None], r_new, r_pool)
            return r[..., 0, :], r[..., 1, :]

        kv_win = jnp.concatenate([wbuf, wn], 0)
        w_pos = past - wbuf.shape[0] + jnp.arange(kv_win.shape[0])
        o_n = _nsa_attend(qn, gn, q_pos, kv_cmp, fetch, kv_win, w_pos, bias_n)
        k_idx = jnp.concatenate([cache_idx[pt].reshape(past, IDX_DH), ki], 0)
        idx = _indexer_topk(qi, wi, k_idx, q_pos, topk)
        pc = jnp.minimum(idx, past - 1)
        kv_sel = jnp.where((idx >= past)[..., None, None, None], kv[jnp.clip(idx - past, 0, T - 1)],
                           cache_dkv[pt[pc // PAGE_SIZE], pc % PAGE_SIZE])
        o_d = _dsa_attend(qd, q_pos, idx, kv_sel[:, :, 0], kv_sel[:, :, 1], bias_d)
        return o_n, o_d, kv_win[T:]

    return lax.map(per_seq, (q_n, cmp, sel, win, g_n, q_d, dkv, q_i, k_i, w_i, page_table, cache_win))


def _project(h, w_q, w_kv, w_s, w_g):
    B, S, D = h.shape
    h2 = h.reshape(B * S, D)
    zq = matmul(h2, w_q, out_dtype=F32)
    zkv = matmul(h2, w_kv, out_dtype=F32)
    zs = matmul(h2, w_s, out_dtype=F32)
    zg = matmul(h2, w_g, out_dtype=F32)
    kvn = (B, S, 2, NSA_KV, NSA_DH)
    q_n = zq[:, :NSA_WIDTH].reshape(B, S, NSA_HEADS, NSA_DH)
    q_d = zq[:, NSA_WIDTH:NSA_WIDTH + DSA_WIDTH].reshape(B, S, DSA_HEADS, DSA_DH)
    q_i = zq[:, NSA_WIDTH + DSA_WIDTH:].reshape(B, S, IDX_HEADS, IDX_DH)
    cmp = zkv[:, :KV_W].reshape(kvn)
    sel = zkv[:, KV_W:2 * KV_W].reshape(kvn)
    win = zkv[:, 2 * KV_W:3 * KV_W].reshape(kvn)
    dkv = zkv[:, 3 * KV_W:].reshape(B, S, 2, DSA_KV, DSA_DH)
    k_i = zs[:, :IDX_DH].reshape(B, S, IDX_DH)
    w_i = zs[:, IDX_DH:IDX_DH + IDX_HEADS].reshape(B, S, IDX_HEADS)
    g_n = zs[:, IDX_DH + IDX_HEADS:].reshape(B, S, NSA_HEADS, 3)
    return (q_n, cmp, sel, win, g_n, q_d, dkv, q_i, k_i, w_i), zg


def kernel(x_prompt, x_sample, c_prompt, c_sample, cache_nsa_cmp, cache_nsa_sel, cache_nsa_win, cache_dsa_kv,
           cache_dsa_idx, page_table, rel_bias, w_mod, b_mod, g_pre_mix, g_post_mix, g_pre_ffn, g_post_ffn,
           w_in, cmp_pool, w_up_nsa, w_up_dsa, w_out, w_router, b_router, w_gu, b_gu, w_down, b_down):
    l = 0
    D = D_MODEL
    Bp, Sp, _ = x_prompt.shape
    Bs, Ss, _ = x_sample.shape

    c = jnp.concatenate([c_prompt, c_sample], 0)
    n_c = c.shape[0]
    c_pad = jnp.pad(jax.nn.silu(c), ((0, -n_c % 16), (0, 0))).astype(BF16)
    mod = matmul(c_pad, w_mod[l], b_mod[l], tn=1536)[:n_c]
    mod_p, mod_s = mod[:Bp], mod[Bp:]

    wi = w_in[l]
    seg = lambda k: wi[:, _OFF[k]:_OFF[k + 1]]
    w_q = jnp.concatenate([seg(0), seg(5), seg(7)], 1).astype(BF16)
    w_kv = jnp.concatenate([seg(1), seg(2), seg(3), seg(6)], 1).astype(BF16)
    w_s = jnp.concatenate([seg(8), seg(9), seg(4)], 1).astype(BF16)
    w_g = seg(10).astype(BF16)
    wun, wud, wo = w_up_nsa[l].astype(BF16), w_up_dsa[l].astype(BF16), w_out[l].astype(BF16)

    def mix_front(x, mod_g):
        sh1, sc1 = mod_g[:, :D], mod_g[:, D:2 * D]
        h = norm_mod(x, g_pre_mix[l], sc1, sh1)
        return _project(h, w_q, w_kv, w_s, w_g)

    def mix_back(x, mod_g, o_n, o_d, zg):
        B, S, _ = x.shape
        ga1, sh2, sc2 = mod_g[:, 2 * D:3 * D], mod_g[:, 3 * D:4 * D], mod_g[:, 4 * D:5 * D]
        u = merge_up(o_n.reshape(B * S, -1).astype(BF16), o_d.reshape(B * S, -1).astype(BF16), zg, wun, wud)
        return out_proj(u, x, wo, g_post_mix[l], g_pre_ffn[l], ga1, sc2, sh2, w_router[l], b_router[l])

    parts_p, zg_p = mix_front(x_prompt, mod_p)
    o_n, o_d = mixer_prompt(rel_bias, cmp_pool[l], *parts_p)
    x1_p, h_p, lg_p = mix_back(x_prompt, mod_p, o_n, o_d, zg_p)

    parts_s, zg_s = mix_front(x_sample, mod_s)
    o_n, o_d, new_win = mixer_sample(cache_nsa_cmp[l], cache_nsa_sel[l], cache_nsa_win[l], cache_dsa_kv[l],
                                     cache_dsa_idx[l], page_table, rel_bias, cmp_pool[l], *parts_s)
    x1_s, h_s, lg_s = mix_back(x_sample, mod_s, o_n, o_d, zg_s)

    Tp, Ts = Bp * Sp, Bs * Ss
    h_all = jnp.concatenate([h_p.reshape(Tp, D), h_s.reshape(Ts, D)], 0)
    logits = jnp.concatenate([lg_p.reshape(Tp, N_EXPERTS), lg_s.reshape(Ts, N_EXPERTS)], 0)
    probs, row_tok, blk_e, n_used, dest_of = moe_route(logits)
    out_rows = moe_experts(h_all[row_tok], blk_e, n_used, w_gu[l], b_gu[l], w_down[l], b_down[l])
    picked = out_rows[dest_of]
    y_p = moe_combine(picked[:Tp], probs[:Tp], x1_p, g_post_ffn[l], mod_p[:, 5 * D:])
    y_s = moe_combine(picked[Tp:], probs[Tp:], x1_s, g_post_ffn[l], mod_s[:, 5 * D:])

    st = lambda a: a[None]
    q_n, cmp_p, sel_p, win_p, g_n, q_d, dkv_p, q_i, ki_p, w_i = parts_p
    q_n, cmp_s, sel_s, win_s, g_n, q_d, dkv_s, q_i, ki_s, w_i = parts_s
    keep = min(WINDOW, Sp)
    return (y_p, y_s, st(cmp_p), st(sel_p), st(win_p[:, Sp - keep:]), st(dkv_p), st(ki_p),
            st(cmp_s), st(sel_s), st(new_win), st(dkv_s), st(ki_s))
```

```python
import functools
import math

import jax
import jax.numpy as jnp
import numpy as np
from jax import lax
from jax.experimental import pallas as pl
from jax.experimental.pallas import tpu as pltpu

F32 = jnp.float32
BF16 = jnp.bfloat16

D_MODEL = 2048
NSA_HEADS = 16
NSA_KV = 4
NSA_DH = 64
CMP_BLOCK = 64
N_SEL = 16
WINDOW = 512
DSA_HEADS = 16
DSA_KV = 4
DSA_DH = 64
IDX_HEADS = 16
IDX_DH = 64
DSA_TOPK_MAX = 256
NUM_BUCKETS = 32
REL_MAX_DIST = 128
N_EXPERTS = 32
TOP_K = 4
D_FF = 2048
SWIGLU_LIMIT = 7.0
SWIGLU_ALPHA = 1.702
PAGE_SIZE = 128
Q_BLOCK = 128
RMS_EPS = 1e-6
NEG = -1e30
NSA_WIDTH = NSA_HEADS * NSA_DH
DSA_WIDTH = DSA_HEADS * DSA_DH
KV_W = 2 * NSA_KV * NSA_DH

_SPLIT = (NSA_WIDTH, KV_W, KV_W, KV_W, 3 * NSA_HEADS, DSA_WIDTH, 2 * DSA_KV * DSA_DH,
          IDX_HEADS * IDX_DH, IDX_DH, IDX_HEADS, 2 * D_MODEL)
_OFF = tuple(int(v) for v in np.cumsum((0,) + _SPLIT))

MOE_TM = 256
VMEM_LIMIT = 48 * 1024 * 1024


def _cparams(sem):
    return pltpu.CompilerParams(dimension_semantics=sem, vmem_limit_bytes=VMEM_LIMIT)


def _norm_mod_kernel(x_ref, g_ref, sc_ref, sh_ref, o_ref):
    x = x_ref[0]
    y = x * lax.rsqrt(jnp.mean(x * x, axis=-1, keepdims=True) + RMS_EPS) * g_ref[...]
    o_ref[0] = (y * (1.0 + sc_ref[0]) + sh_ref[0]).astype(o_ref.dtype)


def norm_mod(x, g, scale, shift):
    B, S, D = x.shape
    ts = min(S, 512)
    row = pl.BlockSpec((1, ts, D), lambda b, s: (b, s, 0))
    per_b = pl.BlockSpec((1, 1, D), lambda b, s: (b, 0, 0))
    return pl.pallas_call(
        _norm_mod_kernel,
        out_shape=jax.ShapeDtypeStruct((B, S, D), BF16),
        grid=(B, S // ts),
        in_specs=[row, pl.BlockSpec((1, D), lambda b, s: (0, 0)), per_b, per_b],
        out_specs=row,
        compiler_params=_cparams(("parallel", "parallel")),
        name="norm_mod",
    )(x, g.reshape(1, D), scale.reshape(B, 1, D), shift.reshape(B, 1, D))


def _matmul_kernel(a_ref, w_ref, b_ref, o_ref):
    acc = jnp.dot(a_ref[...], w_ref[...].astype(BF16), preferred_element_type=F32)
    o_ref[...] = (acc + b_ref[...]).astype(o_ref.dtype)


def matmul(a, w, bias=None, out_dtype=F32, tm=512, tn=512):
    M, K = a.shape
    N = w.shape[1]
    tm, tn = min(tm, M), min(tn, N)
    assert M % tm == 0 and N % tn == 0
    if bias is None:
        bias = jnp.zeros((N,), F32)
    return pl.pallas_call(
        _matmul_kernel,
        out_shape=jax.ShapeDtypeStruct((M, N), out_dtype),
        grid=(M // tm, N // tn),
        in_specs=[pl.BlockSpec((tm, K), lambda i, j: (i, 0)),
                  pl.BlockSpec((K, tn), lambda i, j: (0, j)),
                  pl.BlockSpec((1, tn), lambda i, j: (0, j))],
        out_specs=pl.BlockSpec((tm, tn), lambda i, j: (i, j)),
        compiler_params=_cparams(("parallel", "parallel")),
        name="matmul",
    )(a, w, bias.reshape(1, N))


def _merge_kernel(on_ref, od_ref, ga_ref, gb_ref, wn_ref, wd_ref, o_ref):
    a = jnp.dot(on_ref[...], wn_ref[...], preferred_element_type=F32)
    b = jnp.dot(od_ref[...], wd_ref[...], preferred_element_type=F32)
    o_ref[...] = (jax.nn.sigmoid(ga_ref[...]) * a + jax.nn.sigmoid(gb_ref[...]) * b).astype(o_ref.dtype)


def merge_up(o_n, o_d, zg, w_up_nsa, w_up_dsa, tm=512, tn=512):
    M = o_n.shape[0]
    D = w_up_nsa.shape[1]
    tm = min(tm, M)
    nj = D // tn
    return pl.pallas_call(
        _merge_kernel,
        out_shape=jax.ShapeDtypeStruct((M, D), BF16),
        grid=(M // tm, nj),
        in_specs=[pl.BlockSpec((tm, o_n.shape[1]), lambda i, j: (i, 0)),
                  pl.BlockSpec((tm, o_d.shape[1]), lambda i, j: (i, 0)),
                  pl.BlockSpec((tm, tn), lambda i, j: (i, j)),
                  pl.BlockSpec((tm, tn), lambda i, j: (i, j + nj)),
                  pl.BlockSpec((o_n.shape[1], tn), lambda i, j: (0, j)),
                  pl.BlockSpec((o_d.shape[1], tn), lambda i, j: (0, j))],
        out_specs=pl.BlockSpec((tm, tn), lambda i, j: (i, j)),
        compiler_params=_cparams(("parallel", "parallel")),
        name="merge_up",
    )(o_n, o_d, zg, zg, w_up_nsa, w_up_dsa)


def _rms(x, g):
    return x * lax.rsqrt(jnp.mean(x * x, axis=-1, keepdims=True) + RMS_EPS) * g


def _out_proj_kernel(u_ref, x_ref, w_ref, gpost_ref, gpre_ref, ga_ref, sc_ref, sh_ref, wr_ref, br_ref,
                     x1_ref, h_ref, lg_ref):
    m = jnp.dot(u_ref[0], w_ref[...], preferred_element_type=F32)
    x1 = x_ref[0] + ga_ref[0] * _rms(m, gpost_ref[...])
    x1_ref[0] = x1
    h = _rms(x1, gpre_ref[...]) * (1.0 + sc_ref[0]) + sh_ref[0]
    h_ref[0] = h.astype(h_ref.dtype)
    lg_ref[0] = jnp.dot(h, wr_ref[...], preferred_element_type=F32,
                        precision=lax.Precision.HIGHEST) + br_ref[...]


def out_proj(u, x, w_out, g_post, g_pre, gate, scale, shift, w_router, b_router):
    B, S, D = x.shape
    ts = min(S, 256)
    E = w_router.shape[1]
    row = lambda d: pl.BlockSpec((1, ts, d), lambda b, s: (b, s, 0))
    per_b = pl.BlockSpec((1, 1, D), lambda b, s: (b, 0, 0))
    vec = lambda d: pl.BlockSpec((1, d), lambda b, s: (0, 0))
    return pl.pallas_call(
        _out_proj_kernel,
        out_shape=(jax.ShapeDtypeStruct((B, S, D), F32), jax.ShapeDtypeStruct((B, S, D), BF16),
                   jax.ShapeDtypeStruct((B, S, E), F32)),
        grid=(B, S // ts),
        in_specs=[row(D), row(D), pl.BlockSpec((D, D), lambda b, s: (0, 0)), vec(D), vec(D),
                  per_b, per_b, per_b, pl.BlockSpec((D, E), lambda b, s: (0, 0)), vec(E)],
        out_specs=(row(D), row(D), row(E)),
        compiler_params=_cparams(("parallel", "parallel")),
        name="out_proj",
    )(u.reshape(B, S, D), x, w_out, g_post.reshape(1, D), g_pre.reshape(1, D), gate.reshape(B, 1, D),
      scale.reshape(B, 1, D), shift.reshape(B, 1, D), w_router, b_router.reshape(1, E))


def _expert_changed(be_ref, b):
    return jnp.logical_or(b == 0, be_ref[b] != be_ref[jnp.maximum(b - 1, 0)])


def _moe_up_kernel(be_ref, nu_ref, x_ref, wg_ref, wl_ref, bg_ref, bl_ref, o_ref, wg_s, wl_s):
    b = pl.program_id(1)

    @pl.when(_expert_changed(be_ref, b))
    def _():
        wg_s[...] = wg_ref[0].astype(BF16)
        wl_s[...] = wl_ref[0].astype(BF16)

    @pl.when(b < nu_ref[0])
    def _():
        x = x_ref[...]
        g = jnp.dot(x, wg_s[...], preferred_element_type=F32) + bg_ref[0]
        lin = jnp.dot(x, wl_s[...], preferred_element_type=F32) + bl_ref[0]
        g = jnp.minimum(g, SWIGLU_LIMIT)
        lin = jnp.clip(lin, -SWIGLU_LIMIT, SWIGLU_LIMIT)
        o_ref[...] = (g * jax.nn.sigmoid(SWIGLU_ALPHA * g) * (lin + 1.0)).astype(o_ref.dtype)

    @pl.when(b >= nu_ref[0])
    def _():
        o_ref[...] = jnp.zeros_like(o_ref)


def _moe_down_kernel(be_ref, nu_ref, a_ref, w_ref, bias_ref, o_ref, w_s):
    b = pl.program_id(1)

    @pl.when(_expert_changed(be_ref, b))
    def _():
        w_s[...] = w_ref[0].astype(BF16)

    @pl.when(b < nu_ref[0])
    def _():
        o_ref[...] = jnp.dot(a_ref[...], w_s[...], preferred_element_type=F32) + bias_ref[0]

    @pl.when(b >= nu_ref[0])
    def _():
        o_ref[...] = jnp.zeros_like(o_ref)


def moe_experts(xs, blk_e, n_used, w_gu, b_gu, w_down, b_down, tf=512, tn=512):
    R, D = xs.shape
    nb = R // MOE_TM
    E = w_gu.shape[0]
    nf = D_FF // tf
    act = pl.pallas_call(
        _moe_up_kernel,
        out_shape=jax.ShapeDtypeStruct((R, D_FF), BF16),
        grid_spec=pltpu.PrefetchScalarGridSpec(
            num_scalar_prefetch=2,
            grid=(nf, nb),
            in_specs=[pl.BlockSpec((MOE_TM, D), lambda f, b, be, nu: (b, 0)),
                      pl.BlockSpec((1, D, tf), lambda f, b, be, nu: (be[b], 0, f)),
                      pl.BlockSpec((1, D, tf), lambda f, b, be, nu: (be[b], 0, f + nf)),
                      pl.BlockSpec((1, 1, tf), lambda f, b, be, nu: (be[b], 0, f)),
                      pl.BlockSpec((1, 1, tf), lambda f, b, be, nu: (be[b], 0, f + nf))],
            out_specs=pl.BlockSpec((MOE_TM, tf), lambda f, b, be, nu: (b, f)),
            scratch_shapes=[pltpu.VMEM((D, tf), BF16), pltpu.VMEM((D, tf), BF16)]),
        compiler_params=_cparams(("arbitrary", "arbitrary")),
        name="moe_up",
    )(blk_e, n_used, xs, w_gu, w_gu, b_gu.reshape(E, 1, 2 * D_FF), b_gu.reshape(E, 1, 2 * D_FF))
    nn = D // tn
    return pl.pallas_call(
        _moe_down_kernel,
        out_shape=jax.ShapeDtypeStruct((R, D), F32),
        grid_spec=pltpu.PrefetchScalarGridSpec(
            num_scalar_prefetch=2,
            grid=(nn, nb),
            in_specs=[pl.BlockSpec((MOE_TM, D_FF), lambda j, b, be, nu: (b, 0)),
                      pl.BlockSpec((1, D_FF, tn), lambda j, b, be, nu: (be[b], 0, j)),
                      pl.BlockSpec((1, 1, tn), lambda j, b, be, nu: (be[b], 0, j))],
            out_specs=pl.BlockSpec((MOE_TM, tn), lambda j, b, be, nu: (b, j)),
            scratch_shapes=[pltpu.VMEM((D_FF, tn), BF16)]),
        compiler_params=_cparams(("arbitrary", "arbitrary")),
        name="moe_down",
    )(blk_e, n_used, act, w_down, b_down.reshape(E, 1, D))


def _combine_kernel(r_ref, p_ref, x_ref, g_ref, ga_ref, o_ref):
    p = p_ref[0]
    f = r_ref[0, :, 0, :] * p[:, 0:1]
    for k in range(1, TOP_K):
        f = f + r_ref[0, :, k, :] * p[:, k:k + 1]
    o_ref[0] = x_ref[0] + ga_ref[0] * _rms(f, g_ref[...])


def moe_combine(rows, probs, x, g_post, gate):
    B, S, D = x.shape
    ts = min(S, 128)
    return pl.pallas_call(
        _combine_kernel,
        out_shape=jax.ShapeDtypeStruct((B, S, D), F32),
        grid=(B, S // ts),
        in_specs=[pl.BlockSpec((1, ts, TOP_K, D), lambda b, s: (b, s, 0, 0)),
                  pl.BlockSpec((1, ts, TOP_K), lambda b, s: (b, s, 0)),
                  pl.BlockSpec((1, ts, D), lambda b, s: (b, s, 0)),
                  pl.BlockSpec((1, D), lambda b, s: (0, 0)),
                  pl.BlockSpec((1, 1, D), lambda b, s: (b, 0, 0))],
        out_specs=pl.BlockSpec((1, ts, D), lambda b, s: (b, s, 0)),
        compiler_params=_cparams(("parallel", "parallel")),
        name="moe_combine",
    )(rows.reshape(B, S, TOP_K, D), probs.reshape(B, S, TOP_K), x, g_post.reshape(1, D), gate.reshape(B, 1, D))


def moe_route(logits):
    T = logits.shape[0]
    top_v, top_e = lax.top_k(logits, TOP_K)
    probs = jax.nn.softmax(top_v, axis=-1)
    flat_e = top_e.reshape(-1)
    n_assign = flat_e.shape[0]
    order = jnp.argsort(flat_e)
    e_sorted = flat_e[order]
    counts = jnp.bincount(flat_e, length=N_EXPERTS)
    padded = (counts + MOE_TM - 1) // MOE_TM * MOE_TM
    pad_end = jnp.cumsum(padded)
    start = jnp.cumsum(counts) - counts
    dest = (pad_end - padded)[e_sorted] + jnp.arange(n_assign) - start[e_sorted]
    n_blocks = -(-n_assign // MOE_TM) + N_EXPERTS
    row_tok = jnp.zeros((n_blocks * MOE_TM,), jnp.int32).at[dest].set((order // TOP_K).astype(jnp.int32))
    blk_e = jnp.minimum(jnp.searchsorted(pad_end, jnp.arange(n_blocks) * MOE_TM, side='right'),
                        N_EXPERTS - 1).astype(jnp.int32)
    dest_of = jnp.zeros((n_assign,), jnp.int32).at[order].set(dest.astype(jnp.int32))
    n_used = (pad_end[-1] // MOE_TM).astype(jnp.int32).reshape(1)
    return probs, row_tok, blk_e, n_used, dest_of.reshape(T, TOP_K)


def _rel_bucket(dist):
    n = jnp.maximum(dist, 0)
    exact = NUM_BUCKETS // 2
    log_b = exact + (jnp.log(jnp.maximum(n, 1).astype(F32) / exact)
                     / math.log(REL_MAX_DIST / exact) * (NUM_BUCKETS - exact)).astype(jnp.int32)
    return jnp.where(n < exact, n, jnp.minimum(log_b, NUM_BUCKETS - 1))


def _masked_softmax(s, mask):
    s = jnp.where(mask, s, NEG)
    e = jnp.where(mask, jnp.exp(s - jnp.max(s, -1, keepdims=True)), 0.0)
    return e / jnp.maximum(jnp.sum(e, -1, keepdims=True), 1e-30)


def _compress(rows, pool):
    n = rows.shape[0] // CMP_BLOCK
    blocks = rows[:n * CMP_BLOCK].reshape(n, CMP_BLOCK, *rows.shape[1:])
    return jnp.einsum('nlcgd,cl->ncgd', blocks, pool)


def _nsa_attend(q, gates, q_pos, kv_cmp, fetch_sel, kv_win, w_pos, bias_tab):
    T = q.shape[0]
    hpg = NSA_HEADS // NSA_KV
    scale = NSA_DH ** -0.5
    qg = q.reshape(T, NSA_KV, hpg, NSA_DH)
    btab = bias_tab.astype(F32).reshape(NUM_BUCKETS, NSA_KV, hpg)
    nb = kv_cmp.shape[0]
    dist_c = q_pos[:, None] - (jnp.arange(nb) * CMP_BLOCK + CMP_BLOCK - 1)[None, :]
    s_c = (jnp.einsum('tgqd,ngd->tgqn', qg, kv_cmp[:, 0]).astype(F32) * scale
           + btab[_rel_bucket(dist_c)].transpose(0, 2, 3, 1))
    p_c = _masked_softmax(s_c, (dist_c >= 0)[:, None, None, :])
    o_c = jnp.einsum('tgqn,ngd->tgqd', p_c, kv_cmp[:, 1])
    cur = q_pos // CMP_BLOCK
    cand = (dist_c >= 0) & (jnp.arange(nb)[None, :] != cur[:, None])
    imp = jnp.where(cand[:, None, :], p_c.sum(2), -1.0)
    top_v, top_i = lax.top_k(imp, min(N_SEL - 1, nb))
    blk = jnp.concatenate([jnp.broadcast_to(cur[:, None, None], (T, NSA_KV, 1)), top_i], -1)
    blk_ok = jnp.concatenate([jnp.ones((T, NSA_KV, 1), bool), top_v >= 0], -1)
    pos_s = blk[..., None] * CMP_BLOCK + jnp.arange(CMP_BLOCK)
    k_s, v_s = fetch_sel(pos_s)
    dist_s = q_pos[:, None, None, None] - pos_s
    b_s = btab[_rel_bucket(dist_s), jnp.arange(NSA_KV)[None, :, None, None]]
    s_s = (jnp.einsum('tgqd,tgnkd->tgqnk', qg, k_s).astype(F32) * scale + b_s.transpose(0, 1, 4, 2, 3))
    m_s = ((dist_s >= 0) & blk_ok[..., None])[:, :, None]
    n_keys = blk.shape[-1] * CMP_BLOCK
    p_s = _masked_softmax(s_s.reshape(T, NSA_KV, hpg, n_keys), m_s.reshape(T, NSA_KV, 1, n_keys))
    o_s = jnp.einsum('tgqm,tgmd->tgqd', p_s, v_s.reshape(T, NSA_KV, n_keys, NSA_DH))
    dist_w = q_pos[:, None] - w_pos[None, :]
    s_w = (jnp.einsum('tgqd,wgd->tgqw', qg, kv_win[:, 0]).astype(F32) * scale
           + btab[_rel_bucket(dist_w)].transpose(0, 2, 3, 1))
    m_w = (dist_w >= 0) & (dist_w < WINDOW) & (w_pos >= 0)[None, :]
    p_w = _masked_softmax(s_w, m_w[:, None, None, :])
    o_w = jnp.einsum('tgqw,wgd->tgqd', p_w, kv_win[:, 1])
    g = jax.nn.sigmoid(gates.astype(F32)).reshape(T, NSA_KV, hpg, 3)
    o = g[..., 0:1] * o_c + g[..., 1:2] * o_s + g[..., 2:3] * o_w
    return o.reshape(T, NSA_WIDTH)


def _indexer_topk(q_i, w_i, k_i, q_pos, topk):
    s = jax.nn.relu(jnp.einsum('thd,sd->ths', q_i, k_i).astype(F32) * IDX_DH ** -0.5)
    score = jnp.einsum('ths,th->ts', s, w_i.astype(F32)) * IDX_HEADS ** -0.5
    score = jnp.where(jnp.arange(k_i.shape[0])[None, :] <= q_pos[:, None], score, NEG)
    return lax.top_k(score, topk)[1]


def _dsa_attend(q, q_pos, idx, k_sel, v_sel, bias_tab):
    T = q.shape[0]
    hpg = DSA_HEADS // DSA_KV
    qg = q.reshape(T, DSA_KV, hpg, DSA_DH)
    dist = q_pos[:, None] - idx
    bias = bias_tab.astype(F32).reshape(NUM_BUCKETS, DSA_KV, hpg)[_rel_bucket(dist)]
    s = (jnp.einsum('tgqd,tkgd->tgqk', qg, k_sel).astype(F32) * DSA_DH ** -0.5 + bias.transpose(0, 2, 3, 1))
    p = _masked_softmax(s, (dist >= 0)[:, None, None, :])
    o = jnp.einsum('tgqk,tkgd->tgqd', p, v_sel)
    return o.reshape(T, DSA_WIDTH)


TQ = Q_BLOCK
HPG = NSA_HEADS // NSA_KV
INT_MIN = -2 ** 31


def _flash_step(q2, k, v, bias, mask, scale, carry):
    m, l, acc = carry
    nk = k.shape[0]
    s = lax.dot_general(q2, k, (((1,), (1,)), ((), ())), preferred_element_type=F32)
    s = s.reshape(HPG, TQ, nk) * scale + bias
    s = jnp.where(mask[None], s, NEG)
    m_new = jnp.maximum(m, jnp.max(s, axis=-1, keepdims=True))
    p = jnp.where(mask[None], jnp.exp(s - m_new), 0.0)
    alpha = jnp.exp(m - m_new)
    l = alpha * l + jnp.sum(p, axis=-1, keepdims=True)
    pv = jnp.dot(p.reshape(HPG * TQ, nk).astype(BF16), v, preferred_element_type=F32)
    acc = alpha * acc + pv.reshape(HPG, TQ, v.shape[1])
    return m_new, l, acc


def _flash_init(dh):
    return (jnp.full((HPG, TQ, 1), NEG, F32), jnp.zeros((HPG, TQ, 1), F32), jnp.zeros((HPG, TQ, dh), F32))


def _flash_out(carry):
    m, l, acc = carry
    return acc / jnp.maximum(l, 1e-30)


def _causal_blocks(i, q2, k_ref, v_ref, mask_ref, tiles_ref, far_ref, scale):
    dh = k_ref.shape[-1]
    a = lax.broadcasted_iota(jnp.int32, (TQ, TQ), 0)
    b = lax.broadcasted_iota(jnp.int32, (TQ, TQ), 1)

    def blk(ref, jb):
        return ref[pl.ds(pl.multiple_of(jb * TQ, TQ), TQ), :]

    carry = _flash_step(q2, blk(k_ref, i), blk(v_ref, i), tiles_ref[:, 0],
                        jnp.logical_and(mask_ref[i] > 0.5, a >= b), scale, _flash_init(dh))
    jp = jnp.maximum(i - 1, 0)
    carry = _flash_step(q2, blk(k_ref, jp), blk(v_ref, jp), tiles_ref[:, 1],
                        jnp.logical_and(mask_ref[jp] > 0.5, i >= 1), scale, carry)
    far = far_ref[...]

    def body(jb, c):
        return _flash_step(q2, blk(k_ref, jb), blk(v_ref, jb), far, mask_ref[jb] > 0.5, scale, c)

    return lax.fori_loop(0, jnp.maximum(i - 1, 0), body, carry)


def _nsa_prompt_kernel(q_ref, kc_ref, vc_ref, ks_ref, vs_ref, kw_ref, vw_ref, pool_ref, bc_ref, tiles_ref,
                       far_ref, gate_ref, o_ref, kcs, vcs, exp_s, mask_s):
    i = pl.program_id(2)
    S = ks_ref.shape[3]
    nb = S // CMP_BLOCK
    scale = NSA_DH ** -0.5

    @pl.when(i == 0)
    def _():
        kr = kc_ref[0, 0, 0].reshape(nb, CMP_BLOCK, NSA_DH) * pool_ref[0][None]
        vr = vc_ref[0, 0, 0].reshape(nb, CMP_BLOCK, NSA_DH) * pool_ref[1][None]
        kcs[...] = jnp.sum(kr, axis=1).astype(BF16)
        vcs[...] = jnp.sum(vr, axis=1).astype(BF16)
        n_i = lax.broadcasted_iota(jnp.int32, (nb, S), 0)
        s_i = lax.broadcasted_iota(jnp.int32, (nb, S), 1)
        exp_s[...] = jnp.where(s_i // CMP_BLOCK == n_i, 1.0, 0.0).astype(BF16)

    q2 = q_ref[0].reshape(HPG * TQ, NSA_DH)
    t0 = i * TQ

    s_c = lax.dot_general(q2, kcs[...], (((1,), (1,)), ((), ())), preferred_element_type=F32)
    s_c = s_c.reshape(HPG, TQ, nb) * scale + bc_ref[...]
    t_i = t0 + lax.broadcasted_iota(jnp.int32, (TQ, nb), 0)
    n_i = lax.broadcasted_iota(jnp.int32, (TQ, nb), 1)
    vis = t_i >= n_i * CMP_BLOCK + (CMP_BLOCK - 1)
    s_c = jnp.where(vis[None], s_c, NEG)
    e_c = jnp.where(vis[None], jnp.exp(s_c - jnp.max(s_c, axis=-1, keepdims=True)), 0.0)
    p_c = e_c / jnp.maximum(jnp.sum(e_c, axis=-1, keepdims=True), 1e-30)
    o_c = jnp.dot(p_c.reshape(HPG * TQ, nb).astype(BF16), vcs[...], preferred_element_type=F32)

    cur = n_i == t_i // CMP_BLOCK
    work = jnp.where(jnp.logical_and(vis, jnp.logical_not(cur)), jnp.sum(p_c, axis=0), -1.0)
    sel = jnp.where(cur, 1.0, 0.0)
    n_f = n_i.astype(F32)
    for _ in range(min(N_SEL - 1, nb)):
        mx = jnp.max(work, axis=-1, keepdims=True)
        first = jnp.min(jnp.where(work == mx, n_f, float(nb)), axis=-1, keepdims=True)
        pick = n_f == first
        sel = jnp.where(jnp.logical_and(pick, mx >= 0.0), 1.0, sel)
        work = jnp.where(pick, -2.0, work)
    selx = jnp.dot(sel.astype(BF16), exp_s[...], preferred_element_type=F32)
    for jb in range(S // TQ):
        mask_s[jb] = selx[:, jb * TQ:(jb + 1) * TQ]

    o_s = _flash_out(_causal_blocks(i, q2, ks_ref.at[0, 0, 0], vs_ref.at[0, 0, 0], mask_s, tiles_ref, far_ref,
                                    scale))

    a = lax.broadcasted_iota(jnp.int32, (TQ, TQ), 0)
    b = lax.broadcasted_iota(jnp.int32, (TQ, TQ), 1)
    kw, vw = kw_ref.at[0, 0, 0], vw_ref.at[0, 0, 0]
    carry = _flash_init(NSA_DH)
    n_back = WINDOW // TQ
    for k in range(n_back + 1):
        jb = jnp.maximum(i - k, 0)
        ok = i >= k
        if k == 0:
            bias, mask = tiles_ref[:, 0], a >= b
        elif k == 1:
            bias, mask = tiles_ref[:, 1], jnp.broadcast_to(ok, (TQ, TQ))
        elif k < n_back:
            bias, mask = far_ref[...], jnp.broadcast_to(ok, (TQ, TQ))
        else:
            bias, mask = far_ref[...], jnp.logical_and(b > a, ok)
        start = pl.multiple_of(jb * TQ, TQ)
        carry = _flash_step(q2, kw[pl.ds(start, TQ), :], vw[pl.ds(start, TQ), :], bias, mask, scale, carry)
    o_w = _flash_out(carry)

    g = jax.nn.sigmoid(gate_ref[0])
    o = g[..., 0:1] * o_c.reshape(HPG, TQ, NSA_DH) + g[..., 1:2] * o_s + g[..., 2:3] * o_w
    for j in range(HPG):
        o_ref[0, :, j * NSA_DH:(j + 1) * NSA_DH] = o[j].astype(o_ref.dtype)


def _bias_tables(rel_bias, S):
    assert 2 * TQ - (TQ - 1) >= REL_MAX_DIST
    nb = S // CMP_BLOCK
    t = jnp.arange(S)[:, None]
    bc = rel_bias[_rel_bucket(t - (jnp.arange(nb)[None, :] * CMP_BLOCK + CMP_BLOCK - 1))]
    d = jnp.arange(2)[:, None, None] * TQ + jnp.arange(TQ)[None, :, None] - jnp.arange(TQ)[None, None, :]
    tiles = rel_bias[_rel_bucket(d)]
    far = jnp.broadcast_to(rel_bias[NUM_BUCKETS - 1][:, None, None], (rel_bias.shape[1], 1, TQ))
    return bc.transpose(2, 0, 1), tiles.transpose(3, 0, 1, 2), far


def nsa_prompt(q_n, cmp_t, sel_t, win_t, pool, g_n, bias_c, tiles, far):
    B, H, S, dh = q_n.shape
    G = NSA_KV
    nb = S // CMP_BLOCK
    kv = lambda c: pl.BlockSpec((1, 1, 1, S, dh), lambda b, g, i: (b, c, g, 0, 0))
    return pl.pallas_call(
        _nsa_prompt_kernel,
        out_shape=jax.ShapeDtypeStruct((B, S, H * dh), BF16),
        grid=(B, G, S // TQ),
        in_specs=[pl.BlockSpec((1, HPG, TQ, dh), lambda b, g, i: (b, g, i, 0)),
                  kv(0), kv(1), kv(0), kv(1), kv(0), kv(1),
                  pl.BlockSpec((2, CMP_BLOCK, dh), lambda b, g, i: (0, 0, 0)),
                  pl.BlockSpec((HPG, TQ, nb), lambda b, g, i: (g, i, 0)),
                  pl.BlockSpec((HPG, 2, TQ, TQ), lambda b, g, i: (g, 0, 0, 0)),
                  pl.BlockSpec((HPG, 1, TQ), lambda b, g, i: (g, 0, 0)),
                  pl.BlockSpec((1, HPG, TQ, 3), lambda b, g, i: (b, g, i, 0))],
        out_specs=pl.BlockSpec((1, TQ, HPG * dh), lambda b, g, i: (b, i, g)),
        scratch_shapes=[pltpu.VMEM((nb, dh), BF16), pltpu.VMEM((nb, dh), BF16), pltpu.VMEM((nb, S), BF16),
                        pltpu.VMEM((S // TQ, TQ, TQ), F32)],
        compiler_params=_cparams(("parallel", "parallel", "arbitrary")),
        name="nsa_prompt",
    )(q_n, cmp_t, cmp_t, sel_t, sel_t, win_t, win_t,
      jnp.broadcast_to(pool[:, :, None], (2, CMP_BLOCK, dh)), bias_c, tiles, far, g_n)


def _dsa_prompt_kernel(qi_ref, ki_ref, wi_ref, qd_ref, kd_ref, vd_ref, tiles_ref, far_ref, o_ref,
                       key_s, mask_s, *, topk):
    i = pl.program_id(1)
    S = ki_ref.shape[1]
    nkb = S // TQ
    t0 = i * TQ
    a = lax.broadcasted_iota(jnp.int32, (TQ, TQ), 0)
    b = lax.broadcasted_iota(jnp.int32, (TQ, TQ), 1)

    w = wi_ref[0]
    for jb in range(nkb):
        @pl.when(jb <= i)
        def _():
            kk = ki_ref[0, jb * TQ:(jb + 1) * TQ, :]
            acc = jnp.zeros((TQ, TQ), F32)
            for h in range(IDX_HEADS):
                s = lax.dot_general(qi_ref[0, h], kk, (((1,), (1,)), ((), ())), preferred_element_type=F32)
                acc = acc + jnp.maximum(s * IDX_DH ** -0.5, 0.0) * w[:, h:h + 1]
            score = acc * IDX_HEADS ** -0.5
            score = jnp.where(jb * TQ + b <= t0 + a, score, NEG) + 0.0
            bits = pltpu.bitcast(score, jnp.int32)
            key_s[jb] = jnp.where(bits < 0, bits ^ 0x7FFFFFFF, bits)

        @pl.when(jb > i)
        def _():
            bits = pltpu.bitcast(jnp.full((TQ, TQ), NEG, F32), jnp.int32)
            key_s[jb] = bits ^ 0x7FFFFFFF

    def count(pred):
        c = jnp.sum(jnp.where(pred, 1.0, 0.0), axis=0)
        return jnp.sum(c, axis=-1, keepdims=True)[None]

    def bit_body(n, pat):
        cand = pat | lax.shift_left(jnp.int32(1), 31 - n)
        return jnp.where(count(key_s[...] >= (cand ^ INT_MIN)) >= topk, cand, pat)

    thr = lax.fori_loop(0, 32, bit_body, jnp.zeros((1, TQ, 1), jnp.int32)) ^ INT_MIN
    keys = key_s[...]
    need = topk - count(keys > thr)
    pos = (lax.broadcasted_iota(jnp.int32, (nkb, TQ, TQ), 0) * TQ
           + lax.broadcasted_iota(jnp.int32, (nkb, TQ, TQ), 2))
    tie = keys == thr

    def tie_body(n, last):
        cand = last | lax.shift_left(jnp.int32(1), (S - 1).bit_length() - 1 - n)
        return jnp.where(count(jnp.logical_and(tie, pos < cand)) < need, cand, last)

    last = lax.fori_loop(0, (S - 1).bit_length(), tie_body, jnp.zeros((1, TQ, 1), jnp.int32))
    chosen = jnp.logical_or(keys > thr, jnp.logical_and(tie, pos <= last))
    mask_s[...] = jnp.where(chosen, 1.0, 0.0)

    scale = DSA_DH ** -0.5
    for g in range(DSA_KV):
        q2 = qd_ref[0, g * HPG:(g + 1) * HPG].reshape(HPG * TQ, DSA_DH)
        o = _flash_out(_causal_blocks(i, q2, kd_ref.at[0, 0, g], vd_ref.at[0, 0, g], mask_s,
                                      tiles_ref.at[g * HPG:(g + 1) * HPG], far_ref.at[g * HPG:(g + 1) * HPG],
                                      scale))
        for j in range(HPG):
            h = g * HPG + j
            o_ref[0, :, h * DSA_DH:(h + 1) * DSA_DH] = o[j].astype(o_ref.dtype)


def dsa_prompt(q_i, k_i, w_i, q_d, dkv_t, tiles, far):
    B, H, S, dh = q_d.shape
    G = DSA_KV
    topk = min(DSA_TOPK_MAX, S // 4)
    heads = pl.BlockSpec((1, H, TQ, dh), lambda b, i: (b, 0, i, 0))
    kv = lambda c: pl.BlockSpec((1, 1, G, S, dh), lambda b, i: (b, c, 0, 0, 0))
    return pl.pallas_call(
        functools.partial(_dsa_prompt_kernel, topk=topk),
        out_shape=jax.ShapeDtypeStruct((B, S, H * dh), BF16),
        grid=(B, S // TQ),
        in_specs=[heads,
                  pl.BlockSpec((1, S, k_i.shape[2]), lambda b, i: (b, 0, 0)),
                  pl.BlockSpec((1, TQ, H), lambda b, i: (b, i, 0)),
                  heads, kv(0), kv(1),
                  pl.BlockSpec((H, 2, TQ, TQ), lambda b, i: (0, 0, 0, 0)),
                  pl.BlockSpec((H, 1, TQ), lambda b, i: (0, 0, 0))],
        out_specs=pl.BlockSpec((1, TQ, H * dh), lambda b, i: (b, i, 0)),
        scratch_shapes=[pltpu.VMEM((S // TQ, TQ, TQ), jnp.int32), pltpu.VMEM((S // TQ, TQ, TQ), F32)],
        compiler_params=_cparams(("parallel", "arbitrary")),
        name="dsa_prompt",
    )(q_i, k_i, w_i, q_d, dkv_t, dkv_t, tiles, far)


def mixer_prompt(rel_bias, pool, q_n, cmp, sel, win, g_n, q_d, dkv, q_i, k_i, w_i):
    B, S = q_n.shape[:2]
    heads_first = lambda a: a.transpose(0, 2, 1, 3)
    kv_t = lambda a, dt: a.transpose(0, 2, 3, 1, 4).astype(dt)
    bias_c, tiles, far = _bias_tables(rel_bias, S)
    o_n = nsa_prompt(heads_first(q_n).astype(BF16), kv_t(cmp, F32), kv_t(sel, BF16), kv_t(win, BF16), pool,
                     heads_first(g_n), bias_c[:NSA_HEADS], tiles[:NSA_HEADS], far[:NSA_HEADS])
    o_d = dsa_prompt(heads_first(q_i).astype(BF16), k_i.astype(BF16), w_i, heads_first(q_d).astype(BF16),
                     kv_t(dkv, BF16), tiles[NSA_HEADS:], far[NSA_HEADS:])
    return o_n, o_d


KVH = NSA_KV * NSA_DH
PAGES_PER_STEP = 8


def _pad_groups(q, n_heads):
    R, dh = q.shape
    G = n_heads // HPG
    col_g = lax.broadcasted_iota(jnp.int32, (R, G * dh), 1) // dh
    row_g = (lax.broadcasted_iota(jnp.int32, (R, G * dh), 0) % n_heads) // HPG
    return jnp.where(col_g == row_g, jnp.concatenate([q] * G, axis=1), 0.0).astype(BF16)


def _own_group(o_pad, n_heads):
    R = o_pad.shape[0]
    G = n_heads // HPG
    dh = o_pad.shape[1] // G
    row_g = (lax.broadcasted_iota(jnp.int32, (R, dh), 0) % n_heads) // HPG
    out = o_pad[:, :dh]
    for g in range(1, G):
        out = jnp.where(row_g == g, o_pad[:, g * dh:(g + 1) * dh], out)
    return out


def _order_key(x):
    bits = pltpu.bitcast(x, jnp.int32)
    return jnp.where(bits < 0, bits ^ 0x7FFFFFFF, bits)


def _kth_threshold(keys, topk):
    n, R, L = keys.shape

    def count(pred):
        c = jnp.sum(jnp.where(pred, 1.0, 0.0), axis=0)
        return jnp.sum(c, axis=-1, keepdims=True)[None]

    def bit_body(i, pat):
        cand = pat | lax.shift_left(jnp.int32(1), 31 - i)
        return jnp.where(count(keys >= (cand ^ INT_MIN)) >= topk, cand, pat)

    thr = lax.fori_loop(0, 32, bit_body, jnp.zeros((1, R, 1), jnp.int32)) ^ INT_MIN
    need = topk - count(keys > thr)
    pos = lax.broadcasted_iota(jnp.int32, (n, R, L), 0) * L + lax.broadcasted_iota(jnp.int32, (n, R, L), 2)
    tie = keys == thr
    nbits = (n * L - 1).bit_length()

    def tie_body(i, last):
        cand = last | lax.shift_left(jnp.int32(1), nbits - 1 - i)
        return jnp.where(count(jnp.logical_and(tie, pos < cand)) < need, cand, last)

    last = lax.fori_loop(0, nbits, tie_body, jnp.zeros((1, R, 1), jnp.int32))
    return thr, last


def _cmp_pages_kernel(pt_ref, pw_ref, *refs):
    o_ref = refs[-1]
    bpp = PAGE_SIZE // CMP_BLOCK
    for k, page in enumerate(refs[:-1]):
        x = page[0].reshape(bpp, CMP_BLOCK, page.shape[2]) * pw_ref[...][None]
        o_ref[0, k * bpp:(k + 1) * bpp, :] = jnp.sum(x, axis=1)


def cmp_pages(pt_flat, cache, pool, B):
    n_pages = pt_flat.shape[0] // B
    pgs = min(PAGES_PER_STEP, n_pages)
    bpp = PAGE_SIZE // CMP_BLOCK
    Wd = cache.shape[2]
    pw = jnp.repeat(pool.T, Wd // 2, axis=1)
    page = lambda k: pl.BlockSpec((1, PAGE_SIZE, Wd), lambda b, s, pt: (pt[b * n_pages + s * pgs + k], 0, 0))
    return pl.pallas_call(
        _cmp_pages_kernel,
        out_shape=jax.ShapeDtypeStruct((B, n_pages * bpp, Wd), F32),
        grid_spec=pltpu.PrefetchScalarGridSpec(
            num_scalar_prefetch=1, grid=(B, n_pages // pgs),
            in_specs=[pl.BlockSpec((CMP_BLOCK, Wd), lambda b, s, pt: (0, 0))] + [page(k) for k in range(pgs)],
            out_specs=pl.BlockSpec((1, pgs * bpp, Wd), lambda b, s, pt: (b, s, 0))),
        compiler_params=_cparams(("parallel", "parallel")),
        name="cmp_pages",
    )(pt_flat, pw, *([cache] * pgs))


def _nsa_sample_pre_kernel(q_ref, kvc_ref, wb_ref, new_ref, bc_ref, bw_ref, bn_ref, oc_ref, ow_ref, sel_ref,
                           *, past, n_new):
    H = NSA_HEADS
    R = q_ref.shape[1]
    nb = kvc_ref.shape[1]
    W = wb_ref.shape[1]
    scale = NSA_DH ** -0.5
    nt = (((1,), (1,)), ((), ()))
    qp = _pad_groups(q_ref[0], H)

    kvc = kvc_ref[0]
    s_c = lax.dot_general(qp, kvc[:, :KVH].astype(BF16), nt, preferred_element_type=F32) * scale + bc_ref[...]
    q_pos = past + lax.broadcasted_iota(jnp.int32, (R, nb), 0) // H
    n_i = lax.broadcasted_iota(jnp.int32, (R, nb), 1)
    vis = q_pos >= n_i * CMP_BLOCK + (CMP_BLOCK - 1)
    s_c = jnp.where(vis, s_c, NEG)
    e_c = jnp.where(vis, jnp.exp(s_c - jnp.max(s_c, axis=-1, keepdims=True)), 0.0)
    p_c = e_c / jnp.maximum(jnp.sum(e_c, axis=-1, keepdims=True), 1e-30)
    oc_ref[0] = _own_group(jnp.dot(p_c.astype(BF16), kvc[:, KVH:].astype(BF16), preferred_element_type=F32), H)

    same = (lax.broadcasted_iota(jnp.int32, (R, R), 0) // HPG == lax.broadcasted_iota(jnp.int32, (R, R), 1) // HPG)
    imp = jnp.dot(jnp.where(same, 1.0, 0.0), p_c, preferred_element_type=F32, precision=lax.Precision.HIGHEST)
    cur = n_i == q_pos // CMP_BLOCK
    work = jnp.where(jnp.logical_and(vis, jnp.logical_not(cur)), imp, -1.0)
    sel = jnp.where(cur, 1.0, 0.0)
    n_f = n_i.astype(F32)
    for _ in range(min(N_SEL - 1, nb)):
        mx = jnp.max(work, axis=-1, keepdims=True)
        first = jnp.min(jnp.where(work == mx, n_f, float(nb)), axis=-1, keepdims=True)
        pick = n_f == first
        sel = jnp.where(jnp.logical_and(pick, mx >= 0.0), 1.0, sel)
        work = jnp.where(pick, -2.0, work)
    sel_ref[0] = sel.astype(sel_ref.dtype)

    wb, new = wb_ref[0], new_ref[0]
    s1 = lax.dot_general(qp, wb[:, :KVH].astype(BF16), nt, preferred_element_type=F32) * scale + bw_ref[...]
    i1 = lax.broadcasted_iota(jnp.int32, (R, W), 1)
    d1 = W + lax.broadcasted_iota(jnp.int32, (R, W), 0) // H - i1
    m1 = jnp.logical_and(jnp.logical_and(d1 >= 0, d1 < WINDOW), past - W + i1 >= 0)
    L = new.shape[0]
    s2 = lax.dot_general(qp, new[:, :KVH].astype(BF16), nt, preferred_element_type=F32) * scale + bn_ref[...]
    j2 = lax.broadcasted_iota(jnp.int32, (R, L), 1)
    d2 = lax.broadcasted_iota(jnp.int32, (R, L), 0) // H - j2
    m2 = jnp.logical_and(jnp.logical_and(d2 >= 0, d2 < WINDOW), j2 < n_new)
    s1 = jnp.where(m1, s1, NEG)
    s2 = jnp.where(m2, s2, NEG)
    mx = jnp.maximum(jnp.max(s1, axis=-1, keepdims=True), jnp.max(s2, axis=-1, keepdims=True))
    e1 = jnp.where(m1, jnp.exp(s1 - mx), 0.0)
    e2 = jnp.where(m2, jnp.exp(s2 - mx), 0.0)
    den = jnp.sum(e1, axis=-1, keepdims=True) + jnp.sum(e2, axis=-1, keepdims=True)
    o_w = (jnp.dot(e1.astype(BF16), wb[:, KVH:].astype(BF16), preferred_element_type=F32)
           + jnp.dot(e2.astype(BF16), new[:, KVH:].astype(BF16), preferred_element_type=F32))
    ow_ref[0] = _own_group(o_w / jnp.maximum(den, 1e-30), H)


def nsa_sample_pre(q, kvc, wbuf, new_win, bias_c, bias_w, bias_new, past, n_new):
    B, R, dh = q.shape
    nb, W, L = kvc.shape[1], wbuf.shape[1], new_win.shape[1]
    per_b = lambda *s: pl.BlockSpec((1,) + s, lambda b: (b,) + (0,) * len(s))
    const = lambda *s: pl.BlockSpec(s, lambda b: (0,) * len(s))
    return pl.pallas_call(
        functools.partial(_nsa_sample_pre_kernel, past=past, n_new=n_new),
        out_shape=(jax.ShapeDtypeStruct((B, R, dh), F32), jax.ShapeDtypeStruct((B, R, dh), F32),
                   jax.ShapeDtypeStruct((B, R, nb), BF16)),
        grid=(B,),
        in_specs=[per_b(R, dh), per_b(nb, 2 * KVH), per_b(W, 2 * KVH), per_b(L, 2 * KVH),
                  const(R, nb), const(R, W), const(R, L)],
        out_specs=(per_b(R, dh), per_b(R, dh), per_b(R, nb)),
        compiler_params=_cparams(("parallel",)),
        name="nsa_sample_pre",
    )(q, kvc, wbuf, new_win, bias_c, bias_w, bias_new)


def _index_scores(qi, w, keys, n_heads):
    s = lax.dot_general(qi, keys.astype(BF16), (((1,), (1,)), ((), ())), preferred_element_type=F32)
    v = jnp.maximum(s * IDX_DH ** -0.5, 0.0) * w
    T = qi.shape[0] // n_heads
    return jnp.sum(v.reshape(T, n_heads, keys.shape[0]), axis=1) * n_heads ** -0.5 + 0.0


def _idx_scores_kernel(pt_ref, q_ref, w_ref, *refs):
    o_ref = refs[-1]
    qi = q_ref[0].astype(BF16)
    for k, page in enumerate(refs[:-1]):
        o_ref[0, k] = _index_scores(qi, w_ref[0], page[0], IDX_HEADS)


def idx_scores(pt_flat, q_i, w_i, cache):
    B, R, dh = q_i.shape
    T = R // IDX_HEADS
    n_pages = pt_flat.shape[0] // B
    pgs = min(2 * PAGES_PER_STEP, n_pages)
    page = lambda k: pl.BlockSpec((1, PAGE_SIZE, dh), lambda b, s, pt: (pt[b * n_pages + s * pgs + k], 0, 0))
    return pl.pallas_call(
        _idx_scores_kernel,
        out_shape=jax.ShapeDtypeStruct((B, n_pages, T, PAGE_SIZE), F32),
        grid_spec=pltpu.PrefetchScalarGridSpec(
            num_scalar_prefetch=1, grid=(B, n_pages // pgs),
            in_specs=[pl.BlockSpec((1, R, dh), lambda b, s, pt: (b, 0, 0)),
                      pl.BlockSpec((1, R, 1), lambda b, s, pt: (b, 0, 0))] + [page(k) for k in range(pgs)],
            out_specs=pl.BlockSpec((1, pgs, T, PAGE_SIZE), lambda b, s, pt: (b, s, 0, 0))),
        compiler_params=_cparams(("parallel", "parallel")),
        name="idx_scores",
    )(pt_flat, q_i, w_i, *([cache] * pgs))


def _dsa_sample_pre_kernel(sp_ref, q_ref, w_ref, kn_ref, sn_ref, thr_ref, last_ref, *, topk, n_new):
    T, L = sp_ref.shape[2], sp_ref.shape[3]
    sc = _index_scores(q_ref[0].astype(BF16), w_ref[0], kn_ref[0], IDX_HEADS)
    j = lax.broadcasted_iota(jnp.int32, (T, L), 1)
    t = lax.broadcasted_iota(jnp.int32, (T, L), 0)
    sc = jnp.where(jnp.logical_and(j <= t, j < n_new), sc, NEG)
    sn_ref[0, 0] = sc
    keys = jnp.concatenate([_order_key(sp_ref[0]), _order_key(sc)[None]], axis=0)
    thr, last = _kth_threshold(keys, topk)
    thr_ref[0] = jnp.broadcast_to(thr[0], (T, L))
    last_ref[0] = jnp.broadcast_to(last[0], (T, L))


def dsa_sample_pre(scores_past, q_i, w_i, new_ki, topk, n_new):
    B, n_pages, T, L = scores_past.shape
    R, dh = q_i.shape[1:]
    per_b = lambda *s: pl.BlockSpec((1,) + s, lambda b: (b,) + (0,) * len(s))
    return pl.pallas_call(
        functools.partial(_dsa_sample_pre_kernel, topk=topk, n_new=n_new),
        out_shape=(jax.ShapeDtypeStruct((B, 1, T, L), F32), jax.ShapeDtypeStruct((B, T, L), jnp.int32),
                   jax.ShapeDtypeStruct((B, T, L), jnp.int32)),
        grid=(B,),
        in_specs=[per_b(n_pages, T, L), per_b(R, dh), per_b(R, 1), per_b(L, dh)],
        out_specs=(per_b(1, T, L), per_b(T, L), per_b(T, L)),
        compiler_params=_cparams(("parallel",)),
        name="dsa_sample_pre",
    )(scores_past, q_i, w_i, new_ki)


def _paged_attn_kernel(pt_ref, *refs, mode, pgs, n_pages, n_heads, n_new, dh):
    if mode == "blocks":
        q_ref, new_ref, far_ref, tl_ref, tn_ref, sel_ref, oc_ref, ow_ref, gate_ref = refs[:9]
        rest = refs[9:]
    else:
        q_ref, new_ref, far_ref, tl_ref, tn_ref, sp_ref, sn_ref, thr_ref, last_ref = refs[:9]
        rest = refs[9:]
    pages, o_ref = rest[:pgs], rest[pgs]
    qp_s, m_s, l_s, acc_s = rest[pgs + 1:]
    step, n_steps = pl.program_id(1), pl.num_programs(1)
    R = q_ref.shape[1]
    T = R // n_heads
    L = PAGE_SIZE
    scale = dh ** -0.5

    @pl.when(step == 0)
    def _():
        qp_s[...] = _pad_groups(q_ref[0], n_heads)
        m_s[...] = jnp.full(m_s.shape, NEG, F32)
        l_s[...] = jnp.zeros(l_s.shape, F32)
        acc_s[...] = jnp.zeros(acc_s.shape, F32)

    def update(kv, bias, mask):
        s = lax.dot_general(qp_s[...], kv[:, :KVH].astype(BF16), (((1,), (1,)), ((), ())),
                            preferred_element_type=F32) * scale + bias
        s = jnp.where(mask, s, NEG)
        m_old = m_s[...]
        m_new = jnp.maximum(m_old, jnp.max(s, axis=-1, keepdims=True))
        p = jnp.where(mask, jnp.exp(s - m_new), 0.0)
        alpha = jnp.exp(m_old - m_new)
        l_s[...] = alpha * l_s[...] + jnp.sum(p, axis=-1, keepdims=True)
        acc_s[...] = alpha * acc_s[...] + jnp.dot(p.astype(BF16), kv[:, KVH:].astype(BF16),
                                                  preferred_element_type=F32)
        m_s[...] = m_new

    def rows_of_tokens(x):
        return jnp.broadcast_to(x[:, None, :], (T, n_heads, L)).reshape(R, L)

    def chosen(scores, page):
        key = _order_key(scores)
        pos = page * L + lax.broadcasted_iota(jnp.int32, (T, L), 1)
        thr = thr_ref[0]
        ch = jnp.logical_or(key > thr, jnp.logical_and(key == thr, pos <= last_ref[0]))
        return rows_of_tokens(jnp.where(ch, 1.0, 0.0)) > 0.5

    for k in range(pgs):
        page = step * pgs + k
        bias = jnp.where(page == n_pages - 1, tl_ref[...], far_ref[...])
        if mode == "blocks":
            nb = sel_ref.shape[2]
            blk = page * (L // CMP_BLOCK) + lax.broadcasted_iota(jnp.int32, (nb, L), 1) // CMP_BLOCK
            expand = jnp.where(lax.broadcasted_iota(jnp.int32, (nb, L), 0) == blk, 1.0, 0.0).astype(BF16)
            mask = jnp.dot(sel_ref[0], expand, preferred_element_type=F32) > 0.5
        else:
            mask = chosen(sp_ref[0, k], page)
        update(pages[k][0], bias, mask)

    @pl.when(step == n_steps - 1)
    def _():
        j = lax.broadcasted_iota(jnp.int32, (R, L), 1)
        t = lax.broadcasted_iota(jnp.int32, (R, L), 0) // n_heads
        mask = jnp.logical_and(j <= t, j < n_new)
        if mode != "blocks":
            mask = jnp.logical_and(mask, chosen(sn_ref[0, 0], n_pages))
        update(new_ref[0], tn_ref[...], mask)
        o = _own_group(acc_s[...] / jnp.maximum(l_s[...], 1e-30), n_heads)
        if mode == "blocks":
            g = jax.nn.sigmoid(gate_ref[0])
            o = g[:, 0:1] * oc_ref[0] + g[:, 1:2] * o + g[:, 2:3] * ow_ref[0]
        o_ref[0] = o.astype(o_ref.dtype)


def paged_attn(mode, pt_flat, cache, q, new_rows, far, tile_last, tile_new, extra, n_heads, n_new):
    B, R, dh = q.shape
    n_pages = pt_flat.shape[0] // B
    pgs = min(PAGES_PER_STEP, n_pages)
    L = PAGE_SIZE
    Wd = cache.shape[2]
    per_b = lambda *s: pl.BlockSpec((1,) + s, lambda b, st, pt: (b,) + (0,) * len(s))
    const = lambda *s: pl.BlockSpec(s, lambda b, st, pt: (0,) * len(s))
    page = lambda k: pl.BlockSpec((1, L, Wd), lambda b, st, pt: (pt[b * n_pages + st * pgs + k], 0, 0))
    if mode == "blocks":
        sel = extra[0]
        extra_specs = [per_b(R, sel.shape[2]), per_b(R, dh), per_b(R, dh), per_b(R, 3)]
    else:
        T = R // n_heads
        extra_specs = [pl.BlockSpec((1, pgs, T, L), lambda b, st, pt: (b, st, 0, 0)), per_b(1, T, L),
                       per_b(T, L), per_b(T, L)]
    return pl.pallas_call(
        functools.partial(_paged_attn_kernel, mode=mode, pgs=pgs, n_pages=n_pages, n_heads=n_heads, n_new=n_new,
                          dh=dh),
        out_shape=jax.ShapeDtypeStruct((B, R, dh), BF16),
        grid_spec=pltpu.PrefetchScalarGridSpec(
            num_scalar_prefetch=1, grid=(B, n_pages // pgs),
            in_specs=[per_b(R, dh), per_b(L, Wd), const(R, 1), const(R, L), const(R, L)] + extra_specs
                     + [page(k) for k in range(pgs)],
            out_specs=per_b(R, dh),
            scratch_shapes=[pltpu.VMEM((R, KVH), BF16), pltpu.VMEM((R, 1), F32), pltpu.VMEM((R, 1), F32),
                            pltpu.VMEM((R, KVH), F32)]),
        compiler_params=_cparams(("parallel", "arbitrary")),
        name="paged_attn_" + mode,
    )(pt_flat, q, new_rows, far, tile_last, tile_new, *extra, *([cache] * pgs))


def _sample_bias(rel_bias, head0, n_heads, T, past, W, nb):
    assert PAGE_SIZE + 1 >= REL_MAX_DIST
    R = T * n_heads
    t = (jnp.arange(R) // n_heads)[:, None]
    h = head0 + (jnp.arange(R) % n_heads)[:, None]
    tab = lambda dist: rel_bias[_rel_bucket(dist), h]
    lane = jnp.arange(PAGE_SIZE)[None, :]
    return dict(far=rel_bias[NUM_BUCKETS - 1][h], last=tab(PAGE_SIZE + t - lane), new=tab(t - lane),
                cmp=tab(past + t - (jnp.arange(nb)[None, :] * CMP_BLOCK + CMP_BLOCK - 1)),
                win=tab(W + t - jnp.arange(W)[None, :]))


def mixer_sample(cache_cmp, cache_sel, cache_win, cache_dkv, cache_idx, page_table, rel_bias, pool,
                 q_n, cmp, sel, win, g_n, q_d, dkv, q_i, k_i, w_i):
    B, T = q_n.shape[:2]
    n_pages = page_table.shape[1]
    past = n_pages * PAGE_SIZE
    W = cache_win.shape[1]
    assert T < CMP_BLOCK and past % CMP_BLOCK == 0 and T <= PAGE_SIZE
    topk = min(DSA_TOPK_MAX, (past + T) // 4)
    n_phys = cache_sel.shape[0]
    nb = past // CMP_BLOCK
    pt_flat = page_table.reshape(-1)
    flat = lambda c: c.reshape(c.shape[0], c.shape[1], -1)
    rows = lambda a: a.reshape(B, T * a.shape[2], -1)
    pad_new = lambda a: jnp.pad(a.reshape(B, T, -1), ((0, 0), (0, PAGE_SIZE - T), (0, 0)))
    bn = _sample_bias(rel_bias, 0, NSA_HEADS, T, past, W, nb)
    bd = _sample_bias(rel_bias, NSA_HEADS, DSA_HEADS, T, past, W, nb)

    kvc = cmp_pages(pt_flat, flat(cache_cmp), pool, B)
    o_c, o_w, selm = nsa_sample_pre(rows(q_n), kvc, flat(cache_win), pad_new(win), bn["cmp"], bn["win"], bn["new"],
                                    past, T)
    o_n = paged_attn("blocks", pt_flat, flat(cache_sel), rows(q_n), pad_new(sel), bn["far"], bn["last"], bn["new"],
                     (selm, o_c, o_w, g_n.reshape(B, T * NSA_HEADS, 3)), NSA_HEADS, T)

    qi, wi = rows(q_i), w_i.reshape(B, T * IDX_HEADS, 1)
    sp = idx_scores(pt_flat, qi, wi, cache_idx)
    sn, thr, last = dsa_sample_pre(sp, qi, wi, pad_new(k_i), topk, T)
    o_d = paged_attn("thr", pt_flat, flat(cache_dkv), rows(q_d), pad_new(dkv), bd["far"], bd["last"], bd["new"],
                     (sp, sn, thr, last), DSA_HEADS, T)
    new_win = jnp.concatenate([cache_win[:, T:], win], axis=1)
    return o_n.reshape(B, T, -1), o_d.reshape(B, T, -1), new_win


def _project(h, w_q, w_kv, w_s, w_g):
    B, S, D = h.shape
    h2 = h.reshape(B * S, D)
    zq = matmul(h2, w_q, out_dtype=F32)
    zkv = matmul(h2, w_kv, out_dtype=F32)
    zs = matmul(h2, w_s, out_dtype=F32)
    zg = matmul(h2, w_g, out_dtype=F32)
    kvn = (B, S, 2, NSA_KV, NSA_DH)
    q_n = zq[:, :NSA_WIDTH].reshape(B, S, NSA_HEADS, NSA_DH)
    q_d = zq[:, NSA_WIDTH:NSA_WIDTH + DSA_WIDTH].reshape(B, S, DSA_HEADS, DSA_DH)
    q_i = zq[:, NSA_WIDTH + DSA_WIDTH:].reshape(B, S, IDX_HEADS, IDX_DH)
    cmp = zkv[:, :KV_W].reshape(kvn)
    sel = zkv[:, KV_W:2 * KV_W].reshape(kvn)
    win = zkv[:, 2 * KV_W:3 * KV_W].reshape(kvn)
    dkv = zkv[:, 3 * KV_W:].reshape(B, S, 2, DSA_KV, DSA_DH)
    k_i = zs[:, :IDX_DH].reshape(B, S, IDX_DH)
    w_i = zs[:, IDX_DH:IDX_DH + IDX_HEADS].reshape(B, S, IDX_HEADS)
    g_n = zs[:, IDX_DH + IDX_HEADS:].reshape(B, S, NSA_HEADS, 3)
    return (q_n, cmp, sel, win, g_n, q_d, dkv, q_i, k_i, w_i), zg


def kernel(x_prompt, x_sample, c_prompt, c_sample, cache_nsa_cmp, cache_nsa_sel, cache_nsa_win, cache_dsa_kv,
           cache_dsa_idx, page_table, rel_bias, w_mod, b_mod, g_pre_mix, g_post_mix, g_pre_ffn, g_post_ffn,
           w_in, cmp_pool, w_up_nsa, w_up_dsa, w_out, w_router, b_router, w_gu, b_gu, w_down, b_down):
    l = 0
    D = D_MODEL
    Bp, Sp, _ = x_prompt.shape
    Bs, Ss, _ = x_sample.shape

    c = jnp.concatenate([c_prompt, c_sample], 0)
    n_c = c.shape[0]
    c_pad = jnp.pad(jax.nn.silu(c), ((0, -n_c % 16), (0, 0))).astype(BF16)
    mod = matmul(c_pad, w_mod[l], b_mod[l], tn=1536)[:n_c]
    mod_p, mod_s = mod[:Bp], mod[Bp:]

    wi = w_in[l]
    seg = lambda k: wi[:, _OFF[k]:_OFF[k + 1]]
    w_q = jnp.concatenate([seg(0), seg(5), seg(7)], 1).astype(BF16)
    w_kv = jnp.concatenate([seg(1), seg(2), seg(3), seg(6)], 1).astype(BF16)
    w_s = jnp.concatenate([seg(8), seg(9), seg(4)], 1).astype(BF16)
    w_g = seg(10).astype(BF16)
    wun, wud, wo = w_up_nsa[l].astype(BF16), w_up_dsa[l].astype(BF16), w_out[l].astype(BF16)

    def mix_front(x, mod_g):
        sh1, sc1 = mod_g[:, :D], mod_g[:, D:2 * D]
        h = norm_mod(x, g_pre_mix[l], sc1, sh1)
        return _project(h, w_q, w_kv, w_s, w_g)

    def mix_back(x, mod_g, o_n, o_d, zg):
        B, S, _ = x.shape
        ga1, sh2, sc2 = mod_g[:, 2 * D:3 * D], mod_g[:, 3 * D:4 * D], mod_g[:, 4 * D:5 * D]
        u = merge_up(o_n.reshape(B * S, -1).astype(BF16), o_d.reshape(B * S, -1).astype(BF16), zg, wun, wud)
        return out_proj(u, x, wo, g_post_mix[l], g_pre_ffn[l], ga1, sc2, sh2, w_router[l], b_router[l])

    parts_p, zg_p = mix_front(x_prompt, mod_p)
    o_n, o_d = mixer_prompt(rel_bias, cmp_pool[l], *parts_p)
    x1_p, h_p, lg_p = mix_back(x_prompt, mod_p, o_n, o_d, zg_p)

    parts_s, zg_s = mix_front(x_sample, mod_s)
    o_n, o_d, new_win = mixer_sample(cache_nsa_cmp[l], cache_nsa_sel[l], cache_nsa_win[l], cache_dsa_kv[l],
                                     cache_dsa_idx[l], page_table, rel_bias, cmp_pool[l], *parts_s)
    x1_s, h_s, lg_s = mix_back(x_sample, mod_s, o_n, o_d, zg_s)

    Tp, Ts = Bp * Sp, Bs * Ss
    h_all = jnp.concatenate([h_p.reshape(Tp, D), h_s.reshape(Ts, D)], 0)
    logits = jnp.concatenate([lg_p.reshape(Tp, N_EXPERTS), lg_s.reshape(Ts, N_EXPERTS)], 0)
    probs, row_tok, blk_e, n_used, dest_of = moe_route(logits)
    out_rows = moe_experts(h_all[row_tok], blk_e, n_used, w_gu[l], b_gu[l], w_down[l], b_down[l])
    picked = out_rows[dest_of]
    y_p = moe_combine(picked[:Tp], probs[:Tp], x1_p, g_post_ffn[l], mod_p[:, 5 * D:])
    y_s = moe_combine(picked[Tp:], probs[Tp:], x1_s, g_post_ffn[l], mod_s[:, 5 * D:])

    st = lambda a: a[None]
    q_n, cmp_p, sel_p, win_p, g_n, q_d, dkv_p, q_i, ki_p, w_i = parts_p
    q_n, cmp_s, sel_s, win_s, g_n, q_d, dkv_s, q_i, ki_s, w_i = parts_s
    keep = min(WINDOW, Sp)
    return (y_p, y_s, st(cmp_p), st(sel_p), st(win_p[:, Sp - keep:]), st(dkv_p), st(ki_p),
            st(cmp_s), st(sel_s), st(new_win), st(dkv_s), st(ki_s))
```

```python
import functools
import math

import jax
import jax.numpy as jnp
import numpy as np
from jax import lax
from jax.experimental import pallas as pl
from jax.experimental.pallas import tpu as pltpu

F32 = jnp.float32
BF16 = jnp.bfloat16

D_MODEL = 2048
NSA_HEADS = 16
NSA_KV = 4
NSA_DH = 64
CMP_BLOCK = 64
N_SEL = 16
WINDOW = 512
DSA_HEADS = 16
DSA_KV = 4
DSA_DH = 64
IDX_HEADS = 16
IDX_DH = 64
DSA_TOPK_MAX = 256
NUM_BUCKETS = 32
REL_MAX_DIST = 128
N_EXPERTS = 32
TOP_K = 4
D_FF = 2048
SWIGLU_LIMIT = 7.0
SWIGLU_ALPHA = 1.702
PAGE_SIZE = 128
Q_BLOCK = 128
RMS_EPS = 1e-6
NEG = -1e30
NSA_WIDTH = NSA_HEADS * NSA_DH
DSA_WIDTH = DSA_HEADS * DSA_DH
KV_W = 2 * NSA_KV * NSA_DH

_SPLIT = (NSA_WIDTH, KV_W, KV_W, KV_W, 3 * NSA_HEADS, DSA_WIDTH, 2 * DSA_KV * DSA_DH,
          IDX_HEADS * IDX_DH, IDX_DH, IDX_HEADS, 2 * D_MODEL)
_OFF = tuple(int(v) for v in np.cumsum((0,) + _SPLIT))

MOE_TM = 512
VMEM_LIMIT = 48 * 1024 * 1024


def _cparams(sem):
    return pltpu.CompilerParams(dimension_semantics=sem, vmem_limit_bytes=VMEM_LIMIT)


def _norm_mod_kernel(x_ref, g_ref, sc_ref, sh_ref, o_ref):
    x = x_ref[0]
    y = x * lax.rsqrt(jnp.mean(x * x, axis=-1, keepdims=True) + RMS_EPS) * g_ref[...]
    o_ref[0] = (y * (1.0 + sc_ref[0]) + sh_ref[0]).astype(o_ref.dtype)


def norm_mod(x, g, scale, shift):
    B, S, D = x.shape
    ts = min(S, 512)
    row = pl.BlockSpec((1, ts, D), lambda b, s: (b, s, 0))
    per_b = pl.BlockSpec((1, 1, D), lambda b, s: (b, 0, 0))
    return pl.pallas_call(
        _norm_mod_kernel,
        out_shape=jax.ShapeDtypeStruct((B, S, D), BF16),
        grid=(B, S // ts),
        in_specs=[row, pl.BlockSpec((1, D), lambda b, s: (0, 0)), per_b, per_b],
        out_specs=row,
        compiler_params=_cparams(("parallel", "parallel")),
        name="norm_mod",
    )(x, g.reshape(1, D), scale.reshape(B, 1, D), shift.reshape(B, 1, D))


def _matmul_kernel(a_ref, w_ref, b_ref, o_ref):
    acc = jnp.dot(a_ref[...], w_ref[...].astype(BF16), preferred_element_type=F32)
    o_ref[...] = (acc + b_ref[...]).astype(o_ref.dtype)


def matmul(a, w, bias=None, out_dtype=F32, tm=512, tn=512):
    M, K = a.shape
    N = w.shape[1]
    tm, tn = min(tm, M), min(tn, N)
    assert M % tm == 0 and N % tn == 0
    if bias is None:
        bias = jnp.zeros((N,), F32)
    return pl.pallas_call(
        _matmul_kernel,
        out_shape=jax.ShapeDtypeStruct((M, N), out_dtype),
        grid=(M // tm, N // tn),
        in_specs=[pl.BlockSpec((tm, K), lambda i, j: (i, 0)),
                  pl.BlockSpec((K, tn), lambda i, j: (0, j)),
                  pl.BlockSpec((1, tn), lambda i, j: (0, j))],
        out_specs=pl.BlockSpec((tm, tn), lambda i, j: (i, j)),
        compiler_params=_cparams(("parallel", "parallel")),
        name="matmul",
    )(a, w, bias.reshape(1, N))


def _merge_kernel(on_ref, od_ref, ga_ref, gb_ref, wn_ref, wd_ref, o_ref):
    a = jnp.dot(on_ref[...], wn_ref[...], preferred_element_type=F32)
    b = jnp.dot(od_ref[...], wd_ref[...], preferred_element_type=F32)
    o_ref[...] = (jax.nn.sigmoid(ga_ref[...]) * a + jax.nn.sigmoid(gb_ref[...]) * b).astype(o_ref.dtype)


def merge_up(o_n, o_d, zg, w_up_nsa, w_up_dsa, tm=512, tn=512):
    M = o_n.shape[0]
    D = w_up_nsa.shape[1]
    tm = min(tm, M)
    nj = D // tn
    return pl.pallas_call(
        _merge_kernel,
        out_shape=jax.ShapeDtypeStruct((M, D), BF16),
        grid=(M // tm, nj),
        in_specs=[pl.BlockSpec((tm, o_n.shape[1]), lambda i, j: (i, 0)),
                  pl.BlockSpec((tm, o_d.shape[1]), lambda i, j: (i, 0)),
                  pl.BlockSpec((tm, tn), lambda i, j: (i, j)),
                  pl.BlockSpec((tm, tn), lambda i, j: (i, j + nj)),
                  pl.BlockSpec((o_n.shape[1], tn), lambda i, j: (0, j)),
                  pl.BlockSpec((o_d.shape[1], tn), lambda i, j: (0, j))],
        out_specs=pl.BlockSpec((tm, tn), lambda i, j: (i, j)),
        compiler_params=_cparams(("parallel", "parallel")),
        name="merge_up",
    )(o_n, o_d, zg, zg, w_up_nsa, w_up_dsa)


def _rms(x, g):
    return x * lax.rsqrt(jnp.mean(x * x, axis=-1, keepdims=True) + RMS_EPS) * g


def _out_proj_kernel(u_ref, x_ref, w_ref, gpost_ref, gpre_ref, ga_ref, sc_ref, sh_ref, wr_ref, br_ref,
                     x1_ref, h_ref, lg_ref):
    m = jnp.dot(u_ref[0], w_ref[...], preferred_element_type=F32)
    x1 = x_ref[0] + ga_ref[0] * _rms(m, gpost_ref[...])
    x1_ref[0] = x1
    h = _rms(x1, gpre_ref[...]) * (1.0 + sc_ref[0]) + sh_ref[0]
    h_ref[0] = h.astype(h_ref.dtype)
    lg_ref[0] = jnp.dot(h, wr_ref[...], preferred_element_type=F32,
                        precision=lax.Precision.HIGHEST) + br_ref[...]


def out_proj(u, x, w_out, g_post, g_pre, gate, scale, shift, w_router, b_router):
    B, S, D = x.shape
    ts = min(S, 256)
    E = w_router.shape[1]
    row = lambda d: pl.BlockSpec((1, ts, d), lambda b, s: (b, s, 0))
    per_b = pl.BlockSpec((1, 1, D), lambda b, s: (b, 0, 0))
    vec = lambda d: pl.BlockSpec((1, d), lambda b, s: (0, 0))
    return pl.pallas_call(
        _out_proj_kernel,
        out_shape=(jax.ShapeDtypeStruct((B, S, D), F32), jax.ShapeDtypeStruct((B, S, D), BF16),
                   jax.ShapeDtypeStruct((B, S, E), F32)),
        grid=(B, S // ts),
        in_specs=[row(D), row(D), pl.BlockSpec((D, D), lambda b, s: (0, 0)), vec(D), vec(D),
                  per_b, per_b, per_b, pl.BlockSpec((D, E), lambda b, s: (0, 0)), vec(E)],
        out_specs=(row(D), row(D), row(E)),
        compiler_params=_cparams(("parallel", "parallel")),
        name="out_proj",
    )(u.reshape(B, S, D), x, w_out, g_post.reshape(1, D), g_pre.reshape(1, D), gate.reshape(B, 1, D),
      scale.reshape(B, 1, D), shift.reshape(B, 1, D), w_router, b_router.reshape(1, E))


def _expert_changed(be_ref, b):
    return jnp.logical_or(b == 0, be_ref[b] != be_ref[jnp.maximum(b - 1, 0)])


def _moe_up_kernel(be_ref, nu_ref, x_ref, wg_ref, wl_ref, bg_ref, bl_ref, o_ref, wg_s, wl_s):
    b = pl.program_id(1)

    @pl.when(_expert_changed(be_ref, b))
    def _():
        wg_s[...] = wg_ref[0].astype(BF16)
        wl_s[...] = wl_ref[0].astype(BF16)

    @pl.when(b < nu_ref[0])
    def _():
        x = x_ref[...]
        g = jnp.dot(x, wg_s[...], preferred_element_type=F32) + bg_ref[0]
        lin = jnp.dot(x, wl_s[...], preferred_element_type=F32) + bl_ref[0]
        g = jnp.minimum(g, SWIGLU_LIMIT)
        lin = jnp.clip(lin, -SWIGLU_LIMIT, SWIGLU_LIMIT)
        o_ref[...] = (g * jax.nn.sigmoid(SWIGLU_ALPHA * g) * (lin + 1.0)).astype(o_ref.dtype)

    @pl.when(b >= nu_ref[0])
    def _():
        o_ref[...] = jnp.zeros_like(o_ref)


def _moe_down_kernel(be_ref, nu_ref, a_ref, w_ref, bias_ref, o_ref, w_s):
    b = pl.program_id(1)

    @pl.when(_expert_changed(be_ref, b))
    def _():
        w_s[...] = w_ref[0].astype(BF16)

    @pl.when(b < nu_ref[0])
    def _():
        o_ref[...] = jnp.dot(a_ref[...], w_s[...], preferred_element_type=F32) + bias_ref[0]

    @pl.when(b >= nu_ref[0])
    def _():
        o_ref[...] = jnp.zeros_like(o_ref)


def moe_experts(xs, blk_e, n_used, w_gu, b_gu, w_down, b_down, tf=512, tn=512):
    R, D = xs.shape
    nb = R // MOE_TM
    E = w_gu.shape[0]
    nf = D_FF // tf
    act = pl.pallas_call(
        _moe_up_kernel,
        out_shape=jax.ShapeDtypeStruct((R, D_FF), BF16),
        grid_spec=pltpu.PrefetchScalarGridSpec(
            num_scalar_prefetch=2,
            grid=(nf, nb),
            in_specs=[pl.BlockSpec((MOE_TM, D), lambda f, b, be, nu: (b, 0)),
                      pl.BlockSpec((1, D, tf), lambda f, b, be, nu: (be[b], 0, f)),
                      pl.BlockSpec((1, D, tf), lambda f, b, be, nu: (be[b], 0, f + nf)),
                      pl.BlockSpec((1, 1, tf), lambda f, b, be, nu: (be[b], 0, f)),
                      pl.BlockSpec((1, 1, tf), lambda f, b, be, nu: (be[b], 0, f + nf))],
            out_specs=pl.BlockSpec((MOE_TM, tf), lambda f, b, be, nu: (b, f)),
            scratch_shapes=[pltpu.VMEM((D, tf), BF16), pltpu.VMEM((D, tf), BF16)]),
        compiler_params=_cparams(("arbitrary", "arbitrary")),
        name="moe_up",
    )(blk_e, n_used, xs, w_gu, w_gu, b_gu.reshape(E, 1, 2 * D_FF), b_gu.reshape(E, 1, 2 * D_FF))
    nn = D // tn
    return pl.pallas_call(
        _moe_down_kernel,
        out_shape=jax.ShapeDtypeStruct((R, D), F32),
        grid_spec=pltpu.PrefetchScalarGridSpec(
            num_scalar_prefetch=2,
            grid=(nn, nb),
            in_specs=[pl.BlockSpec((MOE_TM, D_FF), lambda j, b, be, nu: (b, 0)),
                      pl.BlockSpec((1, D_FF, tn), lambda j, b, be, nu: (be[b], 0, j)),
                      pl.BlockSpec((1, 1, tn), lambda j, b, be, nu: (be[b], 0, j))],
            out_specs=pl.BlockSpec((MOE_TM, tn), lambda j, b, be, nu: (b, j)),
            scratch_shapes=[pltpu.VMEM((D_FF, tn), BF16)]),
        compiler_params=_cparams(("arbitrary", "arbitrary")),
        name="moe_down",
    )(blk_e, n_used, act, w_down, b_down.reshape(E, 1, D))


def _combine_kernel(p_ref, x_ref, g_ref, ga_ref, *refs):
    o_ref = refs[-1]
    p = p_ref[0]
    f = refs[0][0] * p[:, 0:1]
    for k in range(1, TOP_K):
        f = f + refs[k][0] * p[:, k:k + 1]
    o_ref[0] = x_ref[0] + ga_ref[0] * _rms(f, g_ref[...])


def moe_combine(rows, probs, x, g_post, gate):
    B, S, D = x.shape
    ts = min(S, 256)
    row = pl.BlockSpec((1, ts, D), lambda b, s: (b, s, 0))
    return pl.pallas_call(
        _combine_kernel,
        out_shape=jax.ShapeDtypeStruct((B, S, D), F32),
        grid=(B, S // ts),
        in_specs=[pl.BlockSpec((1, ts, TOP_K), lambda b, s: (b, s, 0)), row,
                  pl.BlockSpec((1, D), lambda b, s: (0, 0)),
                  pl.BlockSpec((1, 1, D), lambda b, s: (b, 0, 0))] + [row] * TOP_K,
        out_specs=row,
        compiler_params=_cparams(("parallel", "parallel")),
        name="moe_combine",
    )(probs.reshape(B, S, TOP_K), x, g_post.reshape(1, D), gate.reshape(B, 1, D),
      *[r.reshape(B, S, D) for r in rows])


def moe_route(logits):
    T = logits.shape[0]
    top_v, top_e = lax.top_k(logits, TOP_K)
    probs = jax.nn.softmax(top_v, axis=-1)
    flat_e = top_e.reshape(-1)
    n_assign = flat_e.shape[0]
    order = jnp.argsort(flat_e)
    e_sorted = flat_e[order]
    counts = jnp.bincount(flat_e, length=N_EXPERTS)
    padded = (counts + MOE_TM - 1) // MOE_TM * MOE_TM
    pad_end = jnp.cumsum(padded)
    start = jnp.cumsum(counts) - counts
    dest = (pad_end - padded)[e_sorted] + jnp.arange(n_assign) - start[e_sorted]
    n_blocks = -(-n_assign // MOE_TM) + N_EXPERTS
    row_tok = jnp.zeros((n_blocks * MOE_TM,), jnp.int32).at[dest].set((order // TOP_K).astype(jnp.int32))
    blk_e = jnp.minimum(jnp.searchsorted(pad_end, jnp.arange(n_blocks) * MOE_TM, side='right'),
                        N_EXPERTS - 1).astype(jnp.int32)
    dest_of = jnp.zeros((n_assign,), jnp.int32).at[order].set(dest.astype(jnp.int32))
    n_used = (pad_end[-1] // MOE_TM).astype(jnp.int32).reshape(1)
    return probs, row_tok, blk_e, n_used, dest_of.reshape(T, TOP_K)


def _rel_bucket(dist):
    n = jnp.maximum(dist, 0)
    exact = NUM_BUCKETS // 2
    log_b = exact + (jnp.log(jnp.maximum(n, 1).astype(F32) / exact)
                     / math.log(REL_MAX_DIST / exact) * (NUM_BUCKETS - exact)).astype(jnp.int32)
    return jnp.where(n < exact, n, jnp.minimum(log_b, NUM_BUCKETS - 1))


def _masked_softmax(s, mask):
    s = jnp.where(mask, s, NEG)
    e = jnp.where(mask, jnp.exp(s - jnp.max(s, -1, keepdims=True)), 0.0)
    return e / jnp.maximum(jnp.sum(e, -1, keepdims=True), 1e-30)


def _compress(rows, pool):
    n = rows.shape[0] // CMP_BLOCK
    blocks = rows[:n * CMP_BLOCK].reshape(n, CMP_BLOCK, *rows.shape[1:])
    return jnp.einsum('nlcgd,cl->ncgd', blocks, pool)


def _nsa_attend(q, gates, q_pos, kv_cmp, fetch_sel, kv_win, w_pos, bias_tab):
    T = q.shape[0]
    hpg = NSA_HEADS // NSA_KV
    scale = NSA_DH ** -0.5
    qg = q.reshape(T, NSA_KV, hpg, NSA_DH)
    btab = bias_tab.astype(F32).reshape(NUM_BUCKETS, NSA_KV, hpg)
    nb = kv_cmp.shape[0]
    dist_c = q_pos[:, None] - (jnp.arange(nb) * CMP_BLOCK + CMP_BLOCK - 1)[None, :]
    s_c = (jnp.einsum('tgqd,ngd->tgqn', qg, kv_cmp[:, 0]).astype(F32) * scale
           + btab[_rel_bucket(dist_c)].transpose(0, 2, 3, 1))
    p_c = _masked_softmax(s_c, (dist_c >= 0)[:, None, None, :])
    o_c = jnp.einsum('tgqn,ngd->tgqd', p_c, kv_cmp[:, 1])
    cur = q_pos // CMP_BLOCK
    cand = (dist_c >= 0) & (jnp.arange(nb)[None, :] != cur[:, None])
    imp = jnp.where(cand[:, None, :], p_c.sum(2), -1.0)
    top_v, top_i = lax.top_k(imp, min(N_SEL - 1, nb))
    blk = jnp.concatenate([jnp.broadcast_to(cur[:, None, None], (T, NSA_KV, 1)), top_i], -1)
    blk_ok = jnp.concatenate([jnp.ones((T, NSA_KV, 1), bool), top_v >= 0], -1)
    pos_s = blk[..., None] * CMP_BLOCK + jnp.arange(CMP_BLOCK)
    k_s, v_s = fetch_sel(pos_s)
    dist_s = q_pos[:, None, None, None] - pos_s
    b_s = btab[_rel_bucket(dist_s), jnp.arange(NSA_KV)[None, :, None, None]]
    s_s = (jnp.einsum('tgqd,tgnkd->tgqnk', qg, k_s).astype(F32) * scale + b_s.transpose(0, 1, 4, 2, 3))
    m_s = ((dist_s >= 0) & blk_ok[..., None])[:, :, None]
    n_keys = blk.shape[-1] * CMP_BLOCK
    p_s = _masked_softmax(s_s.reshape(T, NSA_KV, hpg, n_keys), m_s.reshape(T, NSA_KV, 1, n_keys))
    o_s = jnp.einsum('tgqm,tgmd->tgqd', p_s, v_s.reshape(T, NSA_KV, n_keys, NSA_DH))
    dist_w = q_pos[:, None] - w_pos[None, :]
    s_w = (jnp.einsum('tgqd,wgd->tgqw', qg, kv_win[:, 0]).astype(F32) * scale
           + btab[_rel_bucket(dist_w)].transpose(0, 2, 3, 1))
    m_w = (dist_w >= 0) & (dist_w < WINDOW) & (w_pos >= 0)[None, :]
    p_w = _masked_softmax(s_w, m_w[:, None, None, :])
    o_w = jnp.einsum('tgqw,wgd->tgqd', p_w, kv_win[:, 1])
    g = jax.nn.sigmoid(gates.astype(F32)).reshape(T, NSA_KV, hpg, 3)
    o = g[..., 0:1] * o_c + g[..., 1:2] * o_s + g[..., 2:3] * o_w
    return o.reshape(T, NSA_WIDTH)


def _indexer_topk(q_i, w_i, k_i, q_pos, topk):
    s = jax.nn.relu(jnp.einsum('thd,sd->ths', q_i, k_i).astype(F32) * IDX_DH ** -0.5)
    score = jnp.einsum('ths,th->ts', s, w_i.astype(F32)) * IDX_HEADS ** -0.5
    score = jnp.where(jnp.arange(k_i.shape[0])[None, :] <= q_pos[:, None], score, NEG)
    return lax.top_k(score, topk)[1]


def _dsa_attend(q, q_pos, idx, k_sel, v_sel, bias_tab):
    T = q.shape[0]
    hpg = DSA_HEADS // DSA_KV
    qg = q.reshape(T, DSA_KV, hpg, DSA_DH)
    dist = q_pos[:, None] - idx
    bias = bias_tab.astype(F32).reshape(NUM_BUCKETS, DSA_KV, hpg)[_rel_bucket(dist)]
    s = (jnp.einsum('tgqd,tkgd->tgqk', qg, k_sel).astype(F32) * DSA_DH ** -0.5 + bias.transpose(0, 2, 3, 1))
    p = _masked_softmax(s, (dist >= 0)[:, None, None, :])
    o = jnp.einsum('tgqk,tkgd->tgqd', p, v_sel)
    return o.reshape(T, DSA_WIDTH)


TQ = Q_BLOCK
HPG = NSA_HEADS // NSA_KV
INT_MIN = -2 ** 31


def _flash_step(q2, k, v, bias, mask, scale, carry):
    m, l, acc = carry
    nk = k.shape[0]
    s = lax.dot_general(q2, k, (((1,), (1,)), ((), ())), preferred_element_type=F32)
    s = s.reshape(HPG, TQ, nk) * scale + bias
    s = jnp.where(mask[None], s, NEG)
    m_new = jnp.maximum(m, jnp.max(s, axis=-1, keepdims=True))
    p = jnp.where(mask[None], jnp.exp(s - m_new), 0.0)
    alpha = jnp.exp(m - m_new)
    l = alpha * l + jnp.sum(p, axis=-1, keepdims=True)
    pv = jnp.dot(p.reshape(HPG * TQ, nk).astype(BF16), v, preferred_element_type=F32)
    acc = alpha * acc + pv.reshape(HPG, TQ, v.shape[1])
    return m_new, l, acc


def _flash_init(dh):
    return (jnp.full((HPG, TQ, 1), NEG, F32), jnp.zeros((HPG, TQ, 1), F32), jnp.zeros((HPG, TQ, dh), F32))


def _flash_out(carry):
    m, l, acc = carry
    return acc / jnp.maximum(l, 1e-30)


def _causal_blocks(i, q2, k_ref, v_ref, mask_ref, tiles_ref, far_ref, scale):
    dh = k_ref.shape[-1]
    a = lax.broadcasted_iota(jnp.int32, (TQ, TQ), 0)
    b = lax.broadcasted_iota(jnp.int32, (TQ, TQ), 1)

    def blk(ref, jb):
        return ref[pl.ds(pl.multiple_of(jb * TQ, TQ), TQ), :]

    carry = _flash_step(q2, blk(k_ref, i), blk(v_ref, i), tiles_ref[:, 0],
                        jnp.logical_and(mask_ref[i] > 0.5, a >= b), scale, _flash_init(dh))
    jp = jnp.maximum(i - 1, 0)
    carry = _flash_step(q2, blk(k_ref, jp), blk(v_ref, jp), tiles_ref[:, 1],
                        jnp.logical_and(mask_ref[jp] > 0.5, i >= 1), scale, carry)
    far = far_ref[...]

    def body(jb, c):
        return _flash_step(q2, blk(k_ref, jb), blk(v_ref, jb), far, mask_ref[jb] > 0.5, scale, c)

    return lax.fori_loop(0, jnp.maximum(i - 1, 0), body, carry)


def _nsa_prompt_kernel(q_ref, kc_ref, vc_ref, ks_ref, vs_ref, kw_ref, vw_ref, pool_ref, bc_ref, tiles_ref,
                       far_ref, gate_ref, o_ref, kcs, vcs, exp_s, mask_s):
    i = pl.program_id(2)
    S = ks_ref.shape[3]
    nb = S // CMP_BLOCK
    scale = NSA_DH ** -0.5

    @pl.when(i == 0)
    def _():
        kr = kc_ref[0, 0, 0].reshape(nb, CMP_BLOCK, NSA_DH) * pool_ref[0][None]
        vr = vc_ref[0, 0, 0].reshape(nb, CMP_BLOCK, NSA_DH) * pool_ref[1][None]
        kcs[...] = jnp.sum(kr, axis=1).astype(BF16)
        vcs[...] = jnp.sum(vr, axis=1).astype(BF16)
        n_i = lax.broadcasted_iota(jnp.int32, (nb, S), 0)
        s_i = lax.broadcasted_iota(jnp.int32, (nb, S), 1)
        exp_s[...] = jnp.where(s_i // CMP_BLOCK == n_i, 1.0, 0.0).astype(BF16)

    q2 = q_ref[0].reshape(HPG * TQ, NSA_DH)
    t0 = i * TQ

    s_c = lax.dot_general(q2, kcs[...], (((1,), (1,)), ((), ())), preferred_element_type=F32)
    s_c = s_c.reshape(HPG, TQ, nb) * scale + bc_ref[...]
    t_i = t0 + lax.broadcasted_iota(jnp.int32, (TQ, nb), 0)
    n_i = lax.broadcasted_iota(jnp.int32, (TQ, nb), 1)
    vis = t_i >= n_i * CMP_BLOCK + (CMP_BLOCK - 1)
    s_c = jnp.where(vis[None], s_c, NEG)
    e_c = jnp.where(vis[None], jnp.exp(s_c - jnp.max(s_c, axis=-1, keepdims=True)), 0.0)
    p_c = e_c / jnp.maximum(jnp.sum(e_c, axis=-1, keepdims=True), 1e-30)
    o_c = jnp.dot(p_c.reshape(HPG * TQ, nb).astype(BF16), vcs[...], preferred_element_type=F32)

    cur = n_i == t_i // CMP_BLOCK
    work = jnp.where(jnp.logical_and(vis, jnp.logical_not(cur)), jnp.sum(p_c, axis=0), -1.0)
    sel = jnp.where(cur, 1.0, 0.0)
    n_f = n_i.astype(F32)
    for _ in range(min(N_SEL - 1, nb)):
        mx = jnp.max(work, axis=-1, keepdims=True)
        first = jnp.min(jnp.where(work == mx, n_f, float(nb)), axis=-1, keepdims=True)
        pick = n_f == first
        sel = jnp.where(jnp.logical_and(pick, mx >= 0.0), 1.0, sel)
        work = jnp.where(pick, -2.0, work)
    selx = jnp.dot(sel.astype(BF16), exp_s[...], preferred_element_type=F32)
    for jb in range(S // TQ):
        mask_s[jb] = selx[:, jb * TQ:(jb + 1) * TQ]

    o_s = _flash_out(_causal_blocks(i, q2, ks_ref.at[0, 0, 0], vs_ref.at[0, 0, 0], mask_s, tiles_ref, far_ref,
                                    scale))

    a = lax.broadcasted_iota(jnp.int32, (TQ, TQ), 0)
    b = lax.broadcasted_iota(jnp.int32, (TQ, TQ), 1)
    kw, vw = kw_ref.at[0, 0, 0], vw_ref.at[0, 0, 0]
    carry = _flash_init(NSA_DH)
    n_back = WINDOW // TQ
    for k in range(n_back + 1):
        jb = jnp.maximum(i - k, 0)
        ok = i >= k
        if k == 0:
            bias, mask = tiles_ref[:, 0], a >= b
        elif k == 1:
            bias, mask = tiles_ref[:, 1], jnp.broadcast_to(ok, (TQ, TQ))
        elif k < n_back:
            bias, mask = far_ref[...], jnp.broadcast_to(ok, (TQ, TQ))
        else:
            bias, mask = far_ref[...], jnp.logical_and(b > a, ok)
        start = pl.multiple_of(jb * TQ, TQ)
        carry = _flash_step(q2, kw[pl.ds(start, TQ), :], vw[pl.ds(start, TQ), :], bias, mask, scale, carry)
    o_w = _flash_out(carry)

    g = jax.nn.sigmoid(gate_ref[0])
    o = g[..., 0:1] * o_c.reshape(HPG, TQ, NSA_DH) + g[..., 1:2] * o_s + g[..., 2:3] * o_w
    for j in range(HPG):
        o_ref[0, :, j * NSA_DH:(j + 1) * NSA_DH] = o[j].astype(o_ref.dtype)


def _bias_tables(rel_bias, S):
    assert 2 * TQ - (TQ - 1) >= REL_MAX_DIST
    nb = S // CMP_BLOCK
    t = jnp.arange(S)[:, None]
    bc = rel_bias[_rel_bucket(t - (jnp.arange(nb)[None, :] * CMP_BLOCK + CMP_BLOCK - 1))]
    d = jnp.arange(2)[:, None, None] * TQ + jnp.arange(TQ)[None, :, None] - jnp.arange(TQ)[None, None, :]
    tiles = rel_bias[_rel_bucket(d)]
    far = jnp.broadcast_to(rel_bias[NUM_BUCKETS - 1][:, None, None], (rel_bias.shape[1], 1, TQ))
    return bc.transpose(2, 0, 1), tiles.transpose(3, 0, 1, 2), far


def nsa_prompt(q_n, cmp_t, sel_t, win_t, pool, g_n, bias_c, tiles, far):
    B, H, S, dh = q_n.shape
    G = NSA_KV
    nb = S // CMP_BLOCK
    kv = lambda c: pl.BlockSpec((1, 1, 1, S, dh), lambda b, g, i: (b, c, g, 0, 0))
    return pl.pallas_call(
        _nsa_prompt_kernel,
        out_shape=jax.ShapeDtypeStruct((B, S, H * dh), BF16),
        grid=(B, G, S // TQ),
        in_specs=[pl.BlockSpec((1, HPG, TQ, dh), lambda b, g, i: (b, g, i, 0)),
                  kv(0), kv(1), kv(0), kv(1), kv(0), kv(1),
                  pl.BlockSpec((2, CMP_BLOCK, dh), lambda b, g, i: (0, 0, 0)),
                  pl.BlockSpec((HPG, TQ, nb), lambda b, g, i: (g, i, 0)),
                  pl.BlockSpec((HPG, 2, TQ, TQ), lambda b, g, i: (g, 0, 0, 0)),
                  pl.BlockSpec((HPG, 1, TQ), lambda b, g, i: (g, 0, 0)),
                  pl.BlockSpec((1, HPG, TQ, 3), lambda b, g, i: (b, g, i, 0))],
        out_specs=pl.BlockSpec((1, TQ, HPG * dh), lambda b, g, i: (b, i, g)),
        scratch_shapes=[pltpu.VMEM((nb, dh), BF16), pltpu.VMEM((nb, dh), BF16), pltpu.VMEM((nb, S), BF16),
                        pltpu.VMEM((S // TQ, TQ, TQ), F32)],
        compiler_params=_cparams(("parallel", "parallel", "arbitrary")),
        name="nsa_prompt",
    )(q_n, cmp_t, cmp_t, sel_t, sel_t, win_t, win_t,
      jnp.broadcast_to(pool[:, :, None], (2, CMP_BLOCK, dh)), bias_c, tiles, far, g_n)


def _dsa_prompt_kernel(qi_ref, ki_ref, wi_ref, qd_ref, kd_ref, vd_ref, tiles_ref, far_ref, o_ref,
                       key_s, mask_s, *, topk):
    i = pl.program_id(1)
    S = ki_ref.shape[1]
    nkb = S // TQ
    t0 = i * TQ
    a = lax.broadcasted_iota(jnp.int32, (TQ, TQ), 0)
    b = lax.broadcasted_iota(jnp.int32, (TQ, TQ), 1)

    w = wi_ref[0]
    for jb in range(nkb):
        @pl.when(jb <= i)
        def _():
            kk = ki_ref[0, jb * TQ:(jb + 1) * TQ, :]
            acc = jnp.zeros((TQ, TQ), F32)
            for h in range(IDX_HEADS):
                s = lax.dot_general(qi_ref[0, h], kk, (((1,), (1,)), ((), ())), preferred_element_type=F32)
                acc = acc + jnp.maximum(s * IDX_DH ** -0.5, 0.0) * w[:, h:h + 1]
            score = acc * IDX_HEADS ** -0.5
            score = jnp.where(jb * TQ + b <= t0 + a, score, NEG) + 0.0
            bits = pltpu.bitcast(score, jnp.int32)
            key_s[jb] = jnp.where(bits < 0, bits ^ 0x7FFFFFFF, bits)

        @pl.when(jb > i)
        def _():
            bits = pltpu.bitcast(jnp.full((TQ, TQ), NEG, F32), jnp.int32)
            key_s[jb] = bits ^ 0x7FFFFFFF

    def count(pred):
        c = jnp.sum(jnp.where(pred, 1.0, 0.0), axis=0)
        return jnp.sum(c, axis=-1, keepdims=True)[None]

    def bit_body(n, pat):
        cand = pat | lax.shift_left(jnp.int32(1), 31 - n)
        return jnp.where(count(key_s[...] >= (cand ^ INT_MIN)) >= topk, cand, pat)

    thr = lax.fori_loop(0, 32, bit_body, jnp.zeros((1, TQ, 1), jnp.int32)) ^ INT_MIN
    keys = key_s[...]
    need = topk - count(keys > thr)
    pos = (lax.broadcasted_iota(jnp.int32, (nkb, TQ, TQ), 0) * TQ
           + lax.broadcasted_iota(jnp.int32, (nkb, TQ, TQ), 2))
    tie = keys == thr

    def tie_body(n, last):
        cand = last | lax.shift_left(jnp.int32(1), (S - 1).bit_length() - 1 - n)
        return jnp.where(count(jnp.logical_and(tie, pos < cand)) < need, cand, last)

    last = lax.fori_loop(0, (S - 1).bit_length(), tie_body, jnp.zeros((1, TQ, 1), jnp.int32))
    chosen = jnp.logical_or(keys > thr, jnp.logical_and(tie, pos <= last))
    mask_s[...] = jnp.where(chosen, 1.0, 0.0)

    scale = DSA_DH ** -0.5
    for g in range(DSA_KV):
        q2 = qd_ref[0, g * HPG:(g + 1) * HPG].reshape(HPG * TQ, DSA_DH)
        o = _flash_out(_causal_blocks(i, q2, kd_ref.at[0, 0, g], vd_ref.at[0, 0, g], mask_s,
                                      tiles_ref.at[g * HPG:(g + 1) * HPG], far_ref.at[g * HPG:(g + 1) * HPG],
                                      scale))
        for j in range(HPG):
            h = g * HPG + j
            o_ref[0, :, h * DSA_DH:(h + 1) * DSA_DH] = o[j].astype(o_ref.dtype)


def dsa_prompt(q_i, k_i, w_i, q_d, dkv_t, tiles, far):
    B, H, S, dh = q_d.shape
    G = DSA_KV
    topk = min(DSA_TOPK_MAX, S // 4)
    heads = pl.BlockSpec((1, H, TQ, dh), lambda b, i: (b, 0, i, 0))
    kv = lambda c: pl.BlockSpec((1, 1, G, S, dh), lambda b, i: (b, c, 0, 0, 0))
    return pl.pallas_call(
        functools.partial(_dsa_prompt_kernel, topk=topk),
        out_shape=jax.ShapeDtypeStruct((B, S, H * dh), BF16),
        grid=(B, S // TQ),
        in_specs=[heads,
                  pl.BlockSpec((1, S, k_i.shape[2]), lambda b, i: (b, 0, 0)),
                  pl.BlockSpec((1, TQ, H), lambda b, i: (b, i, 0)),
                  heads, kv(0), kv(1),
                  pl.BlockSpec((H, 2, TQ, TQ), lambda b, i: (0, 0, 0, 0)),
                  pl.BlockSpec((H, 1, TQ), lambda b, i: (0, 0, 0))],
        out_specs=pl.BlockSpec((1, TQ, H * dh), lambda b, i: (b, i, 0)),
        scratch_shapes=[pltpu.VMEM((S // TQ, TQ, TQ), jnp.int32), pltpu.VMEM((S // TQ, TQ, TQ), F32)],
        compiler_params=_cparams(("parallel", "arbitrary")),
        name="dsa_prompt",
    )(q_i, k_i, w_i, q_d, dkv_t, dkv_t, tiles, far)


def mixer_prompt(rel_bias, pool, q_n, cmp, sel, win, g_n, q_d, dkv, q_i, k_i, w_i):
    B, S = q_n.shape[:2]
    heads_first = lambda a: a.transpose(0, 2, 1, 3)
    kv_t = lambda a, dt: a.transpose(0, 2, 3, 1, 4).astype(dt)
    bias_c, tiles, far = _bias_tables(rel_bias, S)
    o_n = nsa_prompt(heads_first(q_n).astype(BF16), kv_t(cmp, F32), kv_t(sel, BF16), kv_t(win, BF16), pool,
                     heads_first(g_n), bias_c[:NSA_HEADS], tiles[:NSA_HEADS], far[:NSA_HEADS])
    o_d = dsa_prompt(heads_first(q_i).astype(BF16), k_i.astype(BF16), w_i, heads_first(q_d).astype(BF16),
                     kv_t(dkv, BF16), tiles[NSA_HEADS:], far[NSA_HEADS:])
    return o_n, o_d


KVH = NSA_KV * NSA_DH
PAGES_PER_STEP = 8


def _pad_groups(q, n_heads):
    R, dh = q.shape
    G = n_heads // HPG
    col_g = lax.broadcasted_iota(jnp.int32, (R, G * dh), 1) // dh
    row_g = (lax.broadcasted_iota(jnp.int32, (R, G * dh), 0) % n_heads) // HPG
    return jnp.where(col_g == row_g, jnp.concatenate([q] * G, axis=1), 0.0).astype(BF16)


def _own_group(o_pad, n_heads):
    R = o_pad.shape[0]
    G = n_heads // HPG
    dh = o_pad.shape[1] // G
    row_g = (lax.broadcasted_iota(jnp.int32, (R, dh), 0) % n_heads) // HPG
    out = o_pad[:, :dh]
    for g in range(1, G):
        out = jnp.where(row_g == g, o_pad[:, g * dh:(g + 1) * dh], out)
    return out


def _order_key(x):
    bits = pltpu.bitcast(x, jnp.int32)
    return jnp.where(bits < 0, bits ^ 0x7FFFFFFF, bits)


def _kth_threshold(keys, topk):
    n, R, L = keys.shape

    def count(pred):
        c = jnp.sum(jnp.where(pred, 1.0, 0.0), axis=0)
        return jnp.sum(c, axis=-1, keepdims=True)[None]

    def bit_body(i, pat):
        cand = pat | lax.shift_left(jnp.int32(1), 31 - i)
        return jnp.where(count(keys >= (cand ^ INT_MIN)) >= topk, cand, pat)

    thr = lax.fori_loop(0, 32, bit_body, jnp.zeros((1, R, 1), jnp.int32)) ^ INT_MIN
    need = topk - count(keys > thr)
    pos = lax.broadcasted_iota(jnp.int32, (n, R, L), 0) * L + lax.broadcasted_iota(jnp.int32, (n, R, L), 2)
    tie = keys == thr
    nbits = (n * L - 1).bit_length()

    def tie_body(i, last):
        cand = last | lax.shift_left(jnp.int32(1), nbits - 1 - i)
        return jnp.where(count(jnp.logical_and(tie, pos < cand)) < need, cand, last)

    last = lax.fori_loop(0, nbits, tie_body, jnp.zeros((1, R, 1), jnp.int32))
    return thr, last


def _cmp_pages_kernel(pt_ref, pw_ref, *refs):
    o_ref = refs[-1]
    acc_k = acc_v = None
    for k, page in enumerate(refs[:-1]):
        x = page[0]
        yk = jnp.dot(x[:KVH], pw_ref[k, 0], preferred_element_type=F32, precision=lax.Precision.HIGHEST)
        yv = jnp.dot(x[KVH:], pw_ref[k, 1], preferred_element_type=F32, precision=lax.Precision.HIGHEST)
        acc_k = yk if acc_k is None else acc_k + yk
        acc_v = yv if acc_v is None else acc_v + yv
    o_ref[0, 0, :KVH, :] = acc_k
    o_ref[0, 0, KVH:, :] = acc_v


def cmp_pages(pt_flat, cache_t, pool, B):
    n_pages = pt_flat.shape[0] // B
    pgs = min(PAGES_PER_STEP, n_pages)
    bpp = PAGE_SIZE // CMP_BLOCK
    Wd = cache_t.shape[1]
    row = jnp.arange(PAGE_SIZE)
    col = jnp.arange(pgs * bpp)
    place = (col[None, None, :] == (jnp.arange(pgs)[:, None, None] * bpp + (row // CMP_BLOCK)[None, :, None]))
    pw = jnp.where(place[:, None], pool[:, row % CMP_BLOCK][None, :, :, None], 0.0)
    page = lambda k: pl.BlockSpec((1, Wd, PAGE_SIZE), lambda b, s, pt: (pt[b * n_pages + s * pgs + k], 0, 0))
    out = pl.pallas_call(
        _cmp_pages_kernel,
        out_shape=jax.ShapeDtypeStruct((B, n_pages // pgs, Wd, pgs * bpp), F32),
        grid_spec=pltpu.PrefetchScalarGridSpec(
            num_scalar_prefetch=1, grid=(B, n_pages // pgs),
            in_specs=[pl.BlockSpec((pgs, 2, PAGE_SIZE, pgs * bpp), lambda b, s, pt: (0, 0, 0, 0))]
                     + [page(k) for k in range(pgs)],
            out_specs=pl.BlockSpec((1, 1, Wd, pgs * bpp), lambda b, s, pt: (b, s, 0, 0))),
        compiler_params=_cparams(("parallel", "parallel")),
        name="cmp_pages",
    )(pt_flat, pw, *([cache_t] * pgs))
    return out.transpose(0, 2, 1, 3).reshape(B, Wd, n_pages * bpp)


def _nsa_sample_pre_kernel(q_ref, kvc_ref, wb_ref, new_ref, bc_ref, bw_ref, bn_ref, oc_ref, ow_ref, sel_ref,
                           *, past, n_new):
    H = NSA_HEADS
    R = q_ref.shape[1]
    nb = kvc_ref.shape[2]
    W = wb_ref.shape[2]
    scale = NSA_DH ** -0.5
    nt = (((1,), (1,)), ((), ()))
    qp = _pad_groups(q_ref[0], H)

    kvc = kvc_ref[0]
    s_c = jnp.dot(qp, kvc[:KVH].astype(BF16), preferred_element_type=F32) * scale + bc_ref[...]
    q_pos = past + lax.broadcasted_iota(jnp.int32, (R, nb), 0) // H
    n_i = lax.broadcasted_iota(jnp.int32, (R, nb), 1)
    vis = q_pos >= n_i * CMP_BLOCK + (CMP_BLOCK - 1)
    s_c = jnp.where(vis, s_c, NEG)
    e_c = jnp.where(vis, jnp.exp(s_c - jnp.max(s_c, axis=-1, keepdims=True)), 0.0)
    p_c = e_c / jnp.maximum(jnp.sum(e_c, axis=-1, keepdims=True), 1e-30)
    oc_ref[0] = _own_group(lax.dot_general(p_c.astype(BF16), kvc[KVH:].astype(BF16), nt,
                                           preferred_element_type=F32), H)

    same = (lax.broadcasted_iota(jnp.int32, (R, R), 0) // HPG == lax.broadcasted_iota(jnp.int32, (R, R), 1) // HPG)
    imp = jnp.dot(jnp.where(same, 1.0, 0.0), p_c, preferred_element_type=F32, precision=lax.Precision.HIGHEST)
    cur = n_i == q_pos // CMP_BLOCK
    work = jnp.where(jnp.logical_and(vis, jnp.logical_not(cur)), imp, -1.0)
    sel = jnp.where(cur, 1.0, 0.0)
    n_f = n_i.astype(F32)
    for _ in range(min(N_SEL - 1, nb)):
        mx = jnp.max(work, axis=-1, keepdims=True)
        first = jnp.min(jnp.where(work == mx, n_f, float(nb)), axis=-1, keepdims=True)
        pick = n_f == first
        sel = jnp.where(jnp.logical_and(pick, mx >= 0.0), 1.0, sel)
        work = jnp.where(pick, -2.0, work)
    sel_ref[0] = sel.astype(sel_ref.dtype)

    wb, new = wb_ref[0], new_ref[0]
    s1 = jnp.dot(qp, wb[:KVH].astype(BF16), preferred_element_type=F32) * scale + bw_ref[...]
    i1 = lax.broadcasted_iota(jnp.int32, (R, W), 1)
    d1 = W + lax.broadcasted_iota(jnp.int32, (R, W), 0) // H - i1
    m1 = jnp.logical_and(jnp.logical_and(d1 >= 0, d1 < WINDOW), past - W + i1 >= 0)
    L = new.shape[0]
    s2 = lax.dot_general(qp, new[:, :KVH].astype(BF16), nt, preferred_element_type=F32) * scale + bn_ref[...]
    j2 = lax.broadcasted_iota(jnp.int32, (R, L), 1)
    d2 = lax.broadcasted_iota(jnp.int32, (R, L), 0) // H - j2
    m2 = jnp.logical_and(jnp.logical_and(d2 >= 0, d2 < WINDOW), j2 < n_new)
    s1 = jnp.where(m1, s1, NEG)
    s2 = jnp.where(m2, s2, NEG)
    mx = jnp.maximum(jnp.max(s1, axis=-1, keepdims=True), jnp.max(s2, axis=-1, keepdims=True))
    e1 = jnp.where(m1, jnp.exp(s1 - mx), 0.0)
    e2 = jnp.where(m2, jnp.exp(s2 - mx), 0.0)
    den = jnp.sum(e1, axis=-1, keepdims=True) + jnp.sum(e2, axis=-1, keepdims=True)
    o_w = (lax.dot_general(e1.astype(BF16), wb[KVH:].astype(BF16), nt, preferred_element_type=F32)
           + jnp.dot(e2.astype(BF16), new[:, KVH:].astype(BF16), preferred_element_type=F32))
    ow_ref[0] = _own_group(o_w / jnp.maximum(den, 1e-30), H)


def nsa_sample_pre(q, kvc, wbuf, new_win, bias_c, bias_w, bias_new, past, n_new):
    B, R, dh = q.shape
    nb, W, L = kvc.shape[2], wbuf.shape[2], new_win.shape[1]
    per_b = lambda *s: pl.BlockSpec((1,) + s, lambda b: (b,) + (0,) * len(s))
    const = lambda *s: pl.BlockSpec(s, lambda b: (0,) * len(s))
    return pl.pallas_call(
        functools.partial(_nsa_sample_pre_kernel, past=past, n_new=n_new),
        out_shape=(jax.ShapeDtypeStruct((B, R, dh), F32), jax.ShapeDtypeStruct((B, R, dh), F32),
                   jax.ShapeDtypeStruct((B, R, nb), BF16)),
        grid=(B,),
        in_specs=[per_b(R, dh), per_b(2 * KVH, nb), per_b(2 * KVH, W), per_b(L, 2 * KVH),
                  const(R, nb), const(R, W), const(R, L)],
        out_specs=(per_b(R, dh), per_b(R, dh), per_b(R, nb)),
        compiler_params=_cparams(("parallel",)),
        name="nsa_sample_pre",
    )(q, kvc, wbuf, new_win, bias_c, bias_w, bias_new)


def _index_scores(qi, w, keys, n_heads, feature_major):
    contract = (((1,), (0,)), ((), ())) if feature_major else (((1,), (1,)), ((), ()))
    s = lax.dot_general(qi, keys.astype(BF16), contract, preferred_element_type=F32)
    v = jnp.maximum(s * IDX_DH ** -0.5, 0.0) * w
    T = qi.shape[0] // n_heads
    return jnp.sum(v.reshape(T, n_heads, s.shape[1]), axis=1) * n_heads ** -0.5 + 0.0


def _idx_scores_kernel(pt_ref, q_ref, w_ref, *refs):
    o_ref = refs[-1]
    qi = q_ref[0].astype(BF16)
    for k, page in enumerate(refs[:-1]):
        o_ref[0, k] = _index_scores(qi, w_ref[0], page[0], IDX_HEADS, True)


def idx_scores(pt_flat, q_i, w_i, cache_t):
    B, R, dh = q_i.shape
    T = R // IDX_HEADS
    n_pages = pt_flat.shape[0] // B
    pgs = min(2 * PAGES_PER_STEP, n_pages)
    page = lambda k: pl.BlockSpec((1, dh, PAGE_SIZE), lambda b, s, pt: (pt[b * n_pages + s * pgs + k], 0, 0))
    return pl.pallas_call(
        _idx_scores_kernel,
        out_shape=jax.ShapeDtypeStruct((B, n_pages, T, PAGE_SIZE), F32),
        grid_spec=pltpu.PrefetchScalarGridSpec(
            num_scalar_prefetch=1, grid=(B, n_pages // pgs),
            in_specs=[pl.BlockSpec((1, R, dh), lambda b, s, pt: (b, 0, 0)),
                      pl.BlockSpec((1, R, 1), lambda b, s, pt: (b, 0, 0))] + [page(k) for k in range(pgs)],
            out_specs=pl.BlockSpec((1, pgs, T, PAGE_SIZE), lambda b, s, pt: (b, s, 0, 0))),
        compiler_params=_cparams(("parallel", "parallel")),
        name="idx_scores",
    )(pt_flat, q_i, w_i, *([cache_t] * pgs))


def _dsa_sample_pre_kernel(sp_ref, q_ref, w_ref, kn_ref, sn_ref, thr_ref, last_ref, *, topk, n_new):
    T, L = sp_ref.shape[2], sp_ref.shape[3]
    sc = _index_scores(q_ref[0].astype(BF16), w_ref[0], kn_ref[0], IDX_HEADS, False)
    j = lax.broadcasted_iota(jnp.int32, (T, L), 1)
    t = lax.broadcasted_iota(jnp.int32, (T, L), 0)
    sc = jnp.where(jnp.logical_and(j <= t, j < n_new), sc, NEG)
    sn_ref[0, 0] = sc
    keys = jnp.concatenate([_order_key(sp_ref[0]), _order_key(sc)[None]], axis=0)
    thr, last = _kth_threshold(keys, topk)
    thr_ref[0] = jnp.broadcast_to(thr[0], (T, L))
    last_ref[0] = jnp.broadcast_to(last[0], (T, L))


def dsa_sample_pre(scores_past, q_i, w_i, new_ki, topk, n_new):
    B, n_pages, T, L = scores_past.shape
    R, dh = q_i.shape[1:]
    per_b = lambda *s: pl.BlockSpec((1,) + s, lambda b: (b,) + (0,) * len(s))
    return pl.pallas_call(
        functools.partial(_dsa_sample_pre_kernel, topk=topk, n_new=n_new),
        out_shape=(jax.ShapeDtypeStruct((B, 1, T, L), F32), jax.ShapeDtypeStruct((B, T, L), jnp.int32),
                   jax.ShapeDtypeStruct((B, T, L), jnp.int32)),
        grid=(B,),
        in_specs=[per_b(n_pages, T, L), per_b(R, dh), per_b(R, 1), per_b(L, dh)],
        out_specs=(per_b(1, T, L), per_b(T, L), per_b(T, L)),
        compiler_params=_cparams(("parallel",)),
        name="dsa_sample_pre",
    )(scores_past, q_i, w_i, new_ki)


def _paged_attn_kernel(pt_ref, *refs, mode, pgs, n_pages, n_heads, n_new, dh):
    if mode == "blocks":
        q_ref, new_ref, far_ref, tl_ref, tn_ref, sel_ref, oc_ref, ow_ref, gate_ref = refs[:9]
        rest = refs[9:]
    else:
        q_ref, new_ref, far_ref, tl_ref, tn_ref, sp_ref, sn_ref, thr_ref, last_ref = refs[:9]
        rest = refs[9:]
    pages, o_ref = rest[:pgs], rest[pgs]
    qp_s, m_s, l_s, acc_s = rest[pgs + 1:]
    step, n_steps = pl.program_id(1), pl.num_programs(1)
    R = q_ref.shape[1]
    T = R // n_heads
    L = PAGE_SIZE
    scale = dh ** -0.5

    @pl.when(step == 0)
    def _():
        qp_s[...] = _pad_groups(q_ref[0], n_heads)
        m_s[...] = jnp.full(m_s.shape, NEG, F32)
        l_s[...] = jnp.zeros(l_s.shape, F32)
        acc_s[...] = jnp.zeros(acc_s.shape, F32)

    def update(kv, feature_major, bias, mask):
        nt = (((1,), (1,)), ((), ()))
        if feature_major:
            s = jnp.dot(qp_s[...], kv[:KVH].astype(BF16), preferred_element_type=F32)
        else:
            s = lax.dot_general(qp_s[...], kv[:, :KVH].astype(BF16), nt, preferred_element_type=F32)
        s = jnp.where(mask, s * scale + bias, NEG)
        m_old = m_s[...]
        m_new = jnp.maximum(m_old, jnp.max(s, axis=-1, keepdims=True))
        p = jnp.where(mask, jnp.exp(s - m_new), 0.0)
        alpha = jnp.exp(m_old - m_new)
        l_s[...] = alpha * l_s[...] + jnp.sum(p, axis=-1, keepdims=True)
        if feature_major:
            pv = lax.dot_general(p.astype(BF16), kv[KVH:].astype(BF16), nt, preferred_element_type=F32)
        else:
            pv = jnp.dot(p.astype(BF16), kv[:, KVH:].astype(BF16), preferred_element_type=F32)
        acc_s[...] = alpha * acc_s[...] + pv
        m_s[...] = m_new

    def rows_of_tokens(x):
        return jnp.broadcast_to(x[:, None, :], (T, n_heads, L)).reshape(R, L)

    def chosen(scores, page):
        key = _order_key(scores)
        pos = page * L + lax.broadcasted_iota(jnp.int32, (T, L), 1)
        thr = thr_ref[0]
        ch = jnp.logical_or(key > thr, jnp.logical_and(key == thr, pos <= last_ref[0]))
        return rows_of_tokens(jnp.where(ch, 1.0, 0.0)) > 0.5

    for k in range(pgs):
        page = step * pgs + k
        bias = jnp.where(page == n_pages - 1, tl_ref[...], far_ref[...])
        if mode == "blocks":
            nb = sel_ref.shape[2]
            blk = page * (L // CMP_BLOCK) + lax.broadcasted_iota(jnp.int32, (nb, L), 1) // CMP_BLOCK
            expand = jnp.where(lax.broadcasted_iota(jnp.int32, (nb, L), 0) == blk, 1.0, 0.0).astype(BF16)
            mask = jnp.dot(sel_ref[0], expand, preferred_element_type=F32) > 0.5
        else:
            mask = chosen(sp_ref[0, k], page)
        update(pages[k][0], True, bias, mask)

    @pl.when(step == n_steps - 1)
    def _():
        j = lax.broadcasted_iota(jnp.int32, (R, L), 1)
        t = lax.broadcasted_iota(jnp.int32, (R, L), 0) // n_heads
        mask = jnp.logical_and(j <= t, j < n_new)
        if mode != "blocks":
            mask = jnp.logical_and(mask, chosen(sn_ref[0, 0], n_pages))
        update(new_ref[0], False, tn_ref[...], mask)
        o = _own_group(acc_s[...] / jnp.maximum(l_s[...], 1e-30), n_heads)
        if mode == "blocks":
            g = jax.nn.sigmoid(gate_ref[0])
            o = g[:, 0:1] * oc_ref[0] + g[:, 1:2] * o + g[:, 2:3] * ow_ref[0]
        o_ref[0] = o.astype(o_ref.dtype)


def paged_attn(mode, pt_flat, cache_t, q, new_rows, far, tile_last, tile_new, extra, n_heads, n_new):
    B, R, dh = q.shape
    n_pages = pt_flat.shape[0] // B
    pgs = min(PAGES_PER_STEP, n_pages)
    L = PAGE_SIZE
    Wd = cache_t.shape[1]
    per_b = lambda *s: pl.BlockSpec((1,) + s, lambda b, st, pt: (b,) + (0,) * len(s))
    const = lambda *s: pl.BlockSpec(s, lambda b, st, pt: (0,) * len(s))
    page = lambda k: pl.BlockSpec((1, Wd, L), lambda b, st, pt: (pt[b * n_pages + st * pgs + k], 0, 0))
    if mode == "blocks":
        sel = extra[0]
        extra_specs = [per_b(R, sel.shape[2]), per_b(R, dh), per_b(R, dh), per_b(R, 3)]
    else:
        T = R // n_heads
        extra_specs = [pl.BlockSpec((1, pgs, T, L), lambda b, st, pt: (b, st, 0, 0)), per_b(1, T, L),
                       per_b(T, L), per_b(T, L)]
    return pl.pallas_call(
        functools.partial(_paged_attn_kernel, mode=mode, pgs=pgs, n_pages=n_pages, n_heads=n_heads, n_new=n_new,
                          dh=dh),
        out_shape=jax.ShapeDtypeStruct((B, R, dh), BF16),
        grid_spec=pltpu.PrefetchScalarGridSpec(
            num_scalar_prefetch=1, grid=(B, n_pages // pgs),
            in_specs=[per_b(R, dh), per_b(L, Wd), const(R, 1), const(R, L), const(R, L)] + extra_specs
                     + [page(k) for k in range(pgs)],
            out_specs=per_b(R, dh),
            scratch_shapes=[pltpu.VMEM((R, KVH), BF16), pltpu.VMEM((R, 1), F32), pltpu.VMEM((R, 1), F32),
                            pltpu.VMEM((R, KVH), F32)]),
        compiler_params=_cparams(("parallel", "arbitrary")),
        name="paged_attn_" + mode,
    )(pt_flat, q, new_rows, far, tile_last, tile_new, *extra, *([cache_t] * pgs))


def _sample_bias(rel_bias, head0, n_heads, T, past, W, nb):
    assert PAGE_SIZE + 1 >= REL_MAX_DIST
    R = T * n_heads
    t = (jnp.arange(R) // n_heads)[:, None]
    h = head0 + (jnp.arange(R) % n_heads)[:, None]
    tab = lambda dist: rel_bias[_rel_bucket(dist), h]
    lane = jnp.arange(PAGE_SIZE)[None, :]
    return dict(far=rel_bias[NUM_BUCKETS - 1][h], last=tab(PAGE_SIZE + t - lane), new=tab(t - lane),
                cmp=tab(past + t - (jnp.arange(nb)[None, :] * CMP_BLOCK + CMP_BLOCK - 1)),
                win=tab(W + t - jnp.arange(W)[None, :]))


def mixer_sample(cache_cmp, cache_sel, cache_win, cache_dkv, cache_idx, page_table, rel_bias, pool,
                 q_n, cmp, sel, win, g_n, q_d, dkv, q_i, k_i, w_i):
    B, T = q_n.shape[:2]
    n_pages = page_table.shape[1]
    past = n_pages * PAGE_SIZE
    W = cache_win.shape[1]
    assert T < CMP_BLOCK and past % CMP_BLOCK == 0 and T <= PAGE_SIZE
    topk = min(DSA_TOPK_MAX, (past + T) // 4)
    n_phys = cache_sel.shape[0]
    nb = past // CMP_BLOCK
    pt_flat = page_table.reshape(-1)
    flat = lambda c: jnp.moveaxis(c, 1, -1).reshape(c.shape[0], -1, c.shape[1])
    rows = lambda a: a.reshape(B, T * a.shape[2], -1)
    pad_new = lambda a: jnp.pad(a.reshape(B, T, -1), ((0, 0), (0, PAGE_SIZE - T), (0, 0)))
    bn = _sample_bias(rel_bias, 0, NSA_HEADS, T, past, W, nb)
    bd = _sample_bias(rel_bias, NSA_HEADS, DSA_HEADS, T, past, W, nb)

    kvc = cmp_pages(pt_flat, flat(cache_cmp), pool, B)
    o_c, o_w, selm = nsa_sample_pre(rows(q_n), kvc, flat(cache_win), pad_new(win), bn["cmp"], bn["win"], bn["new"],
                                    past, T)
    o_n = paged_attn("blocks", pt_flat, flat(cache_sel), rows(q_n), pad_new(sel), bn["far"], bn["last"], bn["new"],
                     (selm, o_c, o_w, g_n.reshape(B, T * NSA_HEADS, 3)), NSA_HEADS, T)

    qi, wi = rows(q_i), w_i.reshape(B, T * IDX_HEADS, 1)
    sp = idx_scores(pt_flat, qi, wi, flat(cache_idx))
    sn, thr, last = dsa_sample_pre(sp, qi, wi, pad_new(k_i), topk, T)
    o_d = paged_attn("thr", pt_flat, flat(cache_dkv), rows(q_d), pad_new(dkv), bd["far"], bd["last"], bd["new"],
                     (sp, sn, thr, last), DSA_HEADS, T)
    new_win = jnp.concatenate([cache_win[:, T:], win], axis=1)
    return o_n.reshape(B, T, -1), o_d.reshape(B, T, -1), new_win


def _project(h, w_q, w_kv, w_s, w_g):
    B, S, D = h.shape
    h2 = h.reshape(B * S, D)
    zq = matmul(h2, w_q, out_dtype=F32)
    zkv = matmul(h2, w_kv, out_dtype=F32)
    zs = matmul(h2, w_s, out_dtype=F32)
    zg = matmul(h2, w_g, out_dtype=F32)
    kvn = (B, S, 2, NSA_KV, NSA_DH)
    q_n = zq[:, :NSA_WIDTH].reshape(B, S, NSA_HEADS, NSA_DH)
    q_d = zq[:, NSA_WIDTH:NSA_WIDTH + DSA_WIDTH].reshape(B, S, DSA_HEADS, DSA_DH)
    q_i = zq[:, NSA_WIDTH + DSA_WIDTH:].reshape(B, S, IDX_HEADS, IDX_DH)
    cmp = zkv[:, :KV_W].reshape(kvn)
    sel = zkv[:, KV_W:2 * KV_W].reshape(kvn)
    win = zkv[:, 2 * KV_W:3 * KV_W].reshape(kvn)
    dkv = zkv[:, 3 * KV_W:].reshape(B, S, 2, DSA_KV, DSA_DH)
    k_i = zs[:, :IDX_DH].reshape(B, S, IDX_DH)
    w_i = zs[:, IDX_DH:IDX_DH + IDX_HEADS].reshape(B, S, IDX_HEADS)
    g_n = zs[:, IDX_DH + IDX_HEADS:].reshape(B, S, NSA_HEADS, 3)
    return (q_n, cmp, sel, win, g_n, q_d, dkv, q_i, k_i, w_i), zg


def kernel(x_prompt, x_sample, c_prompt, c_sample, cache_nsa_cmp, cache_nsa_sel, cache_nsa_win, cache_dsa_kv,
           cache_dsa_idx, page_table, rel_bias, w_mod, b_mod, g_pre_mix, g_post_mix, g_pre_ffn, g_post_ffn,
           w_in, cmp_pool, w_up_nsa, w_up_dsa, w_out, w_router, b_router, w_gu, b_gu, w_down, b_down):
    l = 0
    D = D_MODEL
    Bp, Sp, _ = x_prompt.shape
    Bs, Ss, _ = x_sample.shape

    c = jnp.concatenate([c_prompt, c_sample], 0)
    n_c = c.shape[0]
    c_pad = jnp.pad(jax.nn.silu(c), ((0, -n_c % 16), (0, 0))).astype(BF16)
    mod = matmul(c_pad, w_mod[l], b_mod[l], tn=1536)[:n_c]
    mod_p, mod_s = mod[:Bp], mod[Bp:]

    wi = w_in[l]
    seg = lambda k: wi[:, _OFF[k]:_OFF[k + 1]]
    w_q = jnp.concatenate([seg(0), seg(5), seg(7)], 1).astype(BF16)
    w_kv = jnp.concatenate([seg(1), seg(2), seg(3), seg(6)], 1).astype(BF16)
    w_s = jnp.concatenate([seg(8), seg(9), seg(4)], 1).astype(BF16)
    w_g = seg(10).astype(BF16)
    wun, wud, wo = w_up_nsa[l].astype(BF16), w_up_dsa[l].astype(BF16), w_out[l].astype(BF16)

    def mix_front(x, mod_g):
        sh1, sc1 = mod_g[:, :D], mod_g[:, D:2 * D]
        h = norm_mod(x, g_pre_mix[l], sc1, sh1)
        return _project(h, w_q, w_kv, w_s, w_g)

    def mix_back(x, mod_g, o_n, o_d, zg):
        B, S, _ = x.shape
        ga1, sh2, sc2 = mod_g[:, 2 * D:3 * D], mod_g[:, 3 * D:4 * D], mod_g[:, 4 * D:5 * D]
        u = merge_up(o_n.reshape(B * S, -1).astype(BF16), o_d.reshape(B * S, -1).astype(BF16), zg, wun, wud)
        return out_proj(u, x, wo, g_post_mix[l], g_pre_ffn[l], ga1, sc2, sh2, w_router[l], b_router[l])

    parts_p, zg_p = mix_front(x_prompt, mod_p)
    o_n, o_d = mixer_prompt(rel_bias, cmp_pool[l], *parts_p)
    x1_p, h_p, lg_p = mix_back(x_prompt, mod_p, o_n, o_d, zg_p)

    parts_s, zg_s = mix_front(x_sample, mod_s)
    o_n, o_d, new_win = mixer_sample(cache_nsa_cmp[l], cache_nsa_sel[l], cache_nsa_win[l], cache_dsa_kv[l],
                                     cache_dsa_idx[l], page_table, rel_bias, cmp_pool[l], *parts_s)
    x1_s, h_s, lg_s = mix_back(x_sample, mod_s, o_n, o_d, zg_s)

    Tp, Ts = Bp * Sp, Bs * Ss
    h_all = jnp.concatenate([h_p.reshape(Tp, D), h_s.reshape(Ts, D)], 0)
    logits = jnp.concatenate([lg_p.reshape(Tp, N_EXPERTS), lg_s.reshape(Ts, N_EXPERTS)], 0)
    probs, row_tok, blk_e, n_used, dest_of = moe_route(logits)
    out_rows = moe_experts(h_all[row_tok], blk_e, n_used, w_gu[l], b_gu[l], w_down[l], b_down[l])
    picked = lambda lo, hi: [out_rows[dest_of[lo:hi, k]] for k in range(TOP_K)]
    y_p = moe_combine(picked(0, Tp), probs[:Tp], x1_p, g_post_ffn[l], mod_p[:, 5 * D:])
    y_s = moe_combine(picked(Tp, Tp + Ts), probs[Tp:], x1_s, g_post_ffn[l], mod_s[:, 5 * D:])

    st = lambda a: a[None]
    q_n, cmp_p, sel_p, win_p, g_n, q_d, dkv_p, q_i, ki_p, w_i = parts_p
    q_n, cmp_s, sel_s, win_s, g_n, q_d, dkv_s, q_i, ki_s, w_i = parts_s
    keep = min(WINDOW, Sp)
    return (y_p, y_s, st(cmp_p), st(sel_p), st(win_p[:, Sp - keep:]), st(dkv_p), st(ki_p),
            st(cmp_s), st(sel_s), st(new_win), st(dkv_s), st(ki_s))
```

```python
import functools
import math

import jax
import jax.numpy as jnp
import numpy as np
from jax import lax
from jax.experimental import pallas as pl
from jax.experimental.pallas import tpu as pltpu

F32 = jnp.float32
BF16 = jnp.bfloat16

D_MODEL = 2048
NSA_HEADS = 16
NSA_KV = 4
NSA_DH = 64
CMP_BLOCK = 64
N_SEL = 16
WINDOW = 512
DSA_HEADS = 16
DSA_KV = 4
DSA_DH = 64
IDX_HEADS = 16
IDX_DH = 64
DSA_TOPK_MAX = 256
NUM_BUCKETS = 32
REL_MAX_DIST = 128
N_EXPERTS = 32
TOP_K = 4
D_FF = 2048
SWIGLU_LIMIT = 7.0
SWIGLU_ALPHA = 1.702
PAGE_SIZE = 128
Q_BLOCK = 128
RMS_EPS = 1e-6
NEG = -1e30
NSA_WIDTH = NSA_HEADS * NSA_DH
DSA_WIDTH = DSA_HEADS * DSA_DH
KV_W = 2 * NSA_KV * NSA_DH

_SPLIT = (NSA_WIDTH, KV_W, KV_W, KV_W, 3 * NSA_HEADS, DSA_WIDTH, 2 * DSA_KV * DSA_DH,
          IDX_HEADS * IDX_DH, IDX_DH, IDX_HEADS, 2 * D_MODEL)
_OFF = tuple(int(v) for v in np.cumsum((0,) + _SPLIT))

MOE_TM = 512
VMEM_LIMIT = 48 * 1024 * 1024


def _cparams(sem):
    return pltpu.CompilerParams(dimension_semantics=sem, vmem_limit_bytes=VMEM_LIMIT)


def _norm_mod_kernel(x_ref, g_ref, sc_ref, sh_ref, o_ref):
    x = x_ref[0]
    y = x * lax.rsqrt(jnp.mean(x * x, axis=-1, keepdims=True) + RMS_EPS) * g_ref[...]
    o_ref[0] = (y * (1.0 + sc_ref[0]) + sh_ref[0]).astype(o_ref.dtype)


def norm_mod(x, g, scale, shift):
    B, S, D = x.shape
    ts = min(S, 512)
    row = pl.BlockSpec((1, ts, D), lambda b, s: (b, s, 0))
    per_b = pl.BlockSpec((1, 1, D), lambda b, s: (b, 0, 0))
    return pl.pallas_call(
        _norm_mod_kernel,
        out_shape=jax.ShapeDtypeStruct((B, S, D), BF16),
        grid=(B, S // ts),
        in_specs=[row, pl.BlockSpec((1, D), lambda b, s: (0, 0)), per_b, per_b],
        out_specs=row,
        compiler_params=_cparams(("parallel", "parallel")),
        name="norm_mod",
    )(x, g.reshape(1, D), scale.reshape(B, 1, D), shift.reshape(B, 1, D))


def _matmul_kernel(a_ref, w_ref, b_ref, o_ref):
    acc = jnp.dot(a_ref[...], w_ref[...].astype(BF16), preferred_element_type=F32)
    o_ref[...] = (acc + b_ref[...]).astype(o_ref.dtype)


def matmul(a, w, bias=None, out_dtype=F32, tm=512, tn=512):
    M, K = a.shape
    N = w.shape[1]
    tm, tn = min(tm, M), min(tn, N)
    assert M % tm == 0 and N % tn == 0
    if bias is None:
        bias = jnp.zeros((N,), F32)
    return pl.pallas_call(
        _matmul_kernel,
        out_shape=jax.ShapeDtypeStruct((M, N), out_dtype),
        grid=(M // tm, N // tn),
        in_specs=[pl.BlockSpec((tm, K), lambda i, j: (i, 0)),
                  pl.BlockSpec((K, tn), lambda i, j: (0, j)),
                  pl.BlockSpec((1, tn), lambda i, j: (0, j))],
        out_specs=pl.BlockSpec((tm, tn), lambda i, j: (i, j)),
        compiler_params=_cparams(("parallel", "parallel")),
        name="matmul",
    )(a, w, bias.reshape(1, N))


def _merge_kernel(on_ref, od_ref, ga_ref, gb_ref, wn_ref, wd_ref, o_ref):
    a = jnp.dot(on_ref[...], wn_ref[...], preferred_element_type=F32)
    b = jnp.dot(od_ref[...], wd_ref[...], preferred_element_type=F32)
    o_ref[...] = (jax.nn.sigmoid(ga_ref[...]) * a + jax.nn.sigmoid(gb_ref[...]) * b).astype(o_ref.dtype)


def merge_up(o_n, o_d, zg, w_up_nsa, w_up_dsa, tm=512, tn=512):
    M = o_n.shape[0]
    D = w_up_nsa.shape[1]
    tm = min(tm, M)
    nj = D // tn
    return pl.pallas_call(
        _merge_kernel,
        out_shape=jax.ShapeDtypeStruct((M, D), BF16),
        grid=(M // tm, nj),
        in_specs=[pl.BlockSpec((tm, o_n.shape[1]), lambda i, j: (i, 0)),
                  pl.BlockSpec((tm, o_d.shape[1]), lambda i, j: (i, 0)),
                  pl.BlockSpec((tm, tn), lambda i, j: (i, j)),
                  pl.BlockSpec((tm, tn), lambda i, j: (i, j + nj)),
                  pl.BlockSpec((o_n.shape[1], tn), lambda i, j: (0, j)),
                  pl.BlockSpec((o_d.shape[1], tn), lambda i, j: (0, j))],
        out_specs=pl.BlockSpec((tm, tn), lambda i, j: (i, j)),
        compiler_params=_cparams(("parallel", "parallel")),
        name="merge_up",
    )(o_n, o_d, zg, zg, w_up_nsa, w_up_dsa)


def _rms(x, g):
    return x * lax.rsqrt(jnp.mean(x * x, axis=-1, keepdims=True) + RMS_EPS) * g


def _out_proj_kernel(u_ref, x_ref, w_ref, gpost_ref, gpre_ref, ga_ref, sc_ref, sh_ref, wr_ref, br_ref,
                     x1_ref, h_ref, lg_ref):
    m = jnp.dot(u_ref[0], w_ref[...], preferred_element_type=F32)
    x1 = x_ref[0] + ga_ref[0] * _rms(m, gpost_ref[...])
    x1_ref[0] = x1
    h = _rms(x1, gpre_ref[...]) * (1.0 + sc_ref[0]) + sh_ref[0]
    h_ref[0] = h.astype(h_ref.dtype)
    lg_ref[0] = jnp.dot(h, wr_ref[...], preferred_element_type=F32,
                        precision=lax.Precision.HIGHEST) + br_ref[...]


def out_proj(u, x, w_out, g_post, g_pre, gate, scale, shift, w_router, b_router):
    B, S, D = x.shape
    ts = min(S, 256)
    E = w_router.shape[1]
    row = lambda d: pl.BlockSpec((1, ts, d), lambda b, s: (b, s, 0))
    per_b = pl.BlockSpec((1, 1, D), lambda b, s: (b, 0, 0))
    vec = lambda d: pl.BlockSpec((1, d), lambda b, s: (0, 0))
    return pl.pallas_call(
        _out_proj_kernel,
        out_shape=(jax.ShapeDtypeStruct((B, S, D), F32), jax.ShapeDtypeStruct((B, S, D), BF16),
                   jax.ShapeDtypeStruct((B, S, E), F32)),
        grid=(B, S // ts),
        in_specs=[row(D), row(D), pl.BlockSpec((D, D), lambda b, s: (0, 0)), vec(D), vec(D),
                  per_b, per_b, per_b, pl.BlockSpec((D, E), lambda b, s: (0, 0)), vec(E)],
        out_specs=(row(D), row(D), row(E)),
        compiler_params=_cparams(("parallel", "parallel")),
        name="out_proj",
    )(u.reshape(B, S, D), x, w_out, g_post.reshape(1, D), g_pre.reshape(1, D), gate.reshape(B, 1, D),
      scale.reshape(B, 1, D), shift.reshape(B, 1, D), w_router, b_router.reshape(1, E))


def _expert_changed(be_ref, b):
    return jnp.logical_or(b == 0, be_ref[b] != be_ref[jnp.maximum(b - 1, 0)])


def _moe_up_kernel(be_ref, nu_ref, x_ref, wg_ref, wl_ref, bg_ref, bl_ref, o_ref, wg_s, wl_s):
    b = pl.program_id(1)

    @pl.when(_expert_changed(be_ref, b))
    def _():
        wg_s[...] = wg_ref[0].astype(BF16)
        wl_s[...] = wl_ref[0].astype(BF16)

    @pl.when(b < nu_ref[0])
    def _():
        x = x_ref[...]
        g = jnp.dot(x, wg_s[...], preferred_element_type=F32) + bg_ref[0]
        lin = jnp.dot(x, wl_s[...], preferred_element_type=F32) + bl_ref[0]
        g = jnp.minimum(g, SWIGLU_LIMIT)
        lin = jnp.clip(lin, -SWIGLU_LIMIT, SWIGLU_LIMIT)
        o_ref[...] = (g * jax.nn.sigmoid(SWIGLU_ALPHA * g) * (lin + 1.0)).astype(o_ref.dtype)

    @pl.when(b >= nu_ref[0])
    def _():
        o_ref[...] = jnp.zeros_like(o_ref)


def _moe_down_kernel(be_ref, nu_ref, a_ref, w_ref, bias_ref, o_ref, w_s):
    b = pl.program_id(1)

    @pl.when(_expert_changed(be_ref, b))
    def _():
        w_s[...] = w_ref[0].astype(BF16)

    @pl.when(b < nu_ref[0])
    def _():
        o_ref[...] = jnp.dot(a_ref[...], w_s[...], preferred_element_type=F32) + bias_ref[0]

    @pl.when(b >= nu_ref[0])
    def _():
        o_ref[...] = jnp.zeros_like(o_ref)


def moe_experts(xs, blk_e, n_used, w_gu, b_gu, w_down, b_down, tf=512, tn=512):
    R, D = xs.shape
    nb = R // MOE_TM
    E = w_gu.shape[0]
    nf = D_FF // tf
    act = pl.pallas_call(
        _moe_up_kernel,
        out_shape=jax.ShapeDtypeStruct((R, D_FF), BF16),
        grid_spec=pltpu.PrefetchScalarGridSpec(
            num_scalar_prefetch=2,
            grid=(nf, nb),
            in_specs=[pl.BlockSpec((MOE_TM, D), lambda f, b, be, nu: (b, 0)),
                      pl.BlockSpec((1, D, tf), lambda f, b, be, nu: (be[b], 0, f)),
                      pl.BlockSpec((1, D, tf), lambda f, b, be, nu: (be[b], 0, f + nf)),
                      pl.BlockSpec((1, 1, tf), lambda f, b, be, nu: (be[b], 0, f)),
                      pl.BlockSpec((1, 1, tf), lambda f, b, be, nu: (be[b], 0, f + nf))],
            out_specs=pl.BlockSpec((MOE_TM, tf), lambda f, b, be, nu: (b, f)),
            scratch_shapes=[pltpu.VMEM((D, tf), BF16), pltpu.VMEM((D, tf), BF16)]),
        compiler_params=_cparams(("arbitrary", "arbitrary")),
        name="moe_up",
    )(blk_e, n_used, xs, w_gu, w_gu, b_gu.reshape(E, 1, 2 * D_FF), b_gu.reshape(E, 1, 2 * D_FF))
    nn = D // tn
    return pl.pallas_call(
        _moe_down_kernel,
        out_shape=jax.ShapeDtypeStruct((R, D), F32),
        grid_spec=pltpu.PrefetchScalarGridSpec(
            num_scalar_prefetch=2,
            grid=(nn, nb),
            in_specs=[pl.BlockSpec((MOE_TM, D_FF), lambda j, b, be, nu: (b, 0)),
                      pl.BlockSpec((1, D_FF, tn), lambda j, b, be, nu: (be[b], 0, j)),
                      pl.BlockSpec((1, 1, tn), lambda j, b, be, nu: (be[b], 0, j))],
            out_specs=pl.BlockSpec((MOE_TM, tn), lambda j, b, be, nu: (b, j)),
            scratch_shapes=[pltpu.VMEM((D_FF, tn), BF16)]),
        compiler_params=_cparams(("arbitrary", "arbitrary")),
        name="moe_down",
    )(blk_e, n_used, act, w_down, b_down.reshape(E, 1, D))


def _combine_kernel(p_ref, x_ref, g_ref, ga_ref, *refs):
    o_ref = refs[-1]
    p = p_ref[0]
    f = refs[0][0] * p[:, 0:1]
    for k in range(1, TOP_K):
        f = f + refs[k][0] * p[:, k:k + 1]
    o_ref[0] = x_ref[0] + ga_ref[0] * _rms(f, g_ref[...])


def moe_combine(rows, probs, x, g_post, gate):
    B, S, D = x.shape
    ts = min(S, 256)
    row = pl.BlockSpec((1, ts, D), lambda b, s: (b, s, 0))
    return pl.pallas_call(
        _combine_kernel,
        out_shape=jax.ShapeDtypeStruct((B, S, D), F32),
        grid=(B, S // ts),
        in_specs=[pl.BlockSpec((1, ts, TOP_K), lambda b, s: (b, s, 0)), row,
                  pl.BlockSpec((1, D), lambda b, s: (0, 0)),
                  pl.BlockSpec((1, 1, D), lambda b, s: (b, 0, 0))] + [row] * TOP_K,
        out_specs=row,
        compiler_params=_cparams(("parallel", "parallel")),
        name="moe_combine",
    )(probs.reshape(B, S, TOP_K), x, g_post.reshape(1, D), gate.reshape(B, 1, D),
      *[r.reshape(B, S, D) for r in rows])


def moe_route(logits):
    T = logits.shape[0]
    top_v, top_e = lax.top_k(logits, TOP_K)
    probs = jax.nn.softmax(top_v, axis=-1)
    flat_e = top_e.reshape(-1)
    n_assign = flat_e.shape[0]
    order = jnp.argsort(flat_e)
    e_sorted = flat_e[order]
    counts = jnp.bincount(flat_e, length=N_EXPERTS)
    padded = (counts + MOE_TM - 1) // MOE_TM * MOE_TM
    pad_end = jnp.cumsum(padded)
    start = jnp.cumsum(counts) - counts
    dest = (pad_end - padded)[e_sorted] + jnp.arange(n_assign) - start[e_sorted]
    n_blocks = -(-n_assign // MOE_TM) + N_EXPERTS
    row_tok = jnp.zeros((n_blocks * MOE_TM,), jnp.int32).at[dest].set((order // TOP_K).astype(jnp.int32))
    blk_e = jnp.minimum(jnp.searchsorted(pad_end, jnp.arange(n_blocks) * MOE_TM, side='right'),
                        N_EXPERTS - 1).astype(jnp.int32)
    dest_of = jnp.zeros((n_assign,), jnp.int32).at[order].set(dest.astype(jnp.int32))
    n_used = (pad_end[-1] // MOE_TM).astype(jnp.int32).reshape(1)
    return probs, row_tok, blk_e, n_used, dest_of.reshape(T, TOP_K)


def _rel_bucket(dist):
    n = jnp.maximum(dist, 0)
    exact = NUM_BUCKETS // 2
    log_b = exact + (jnp.log(jnp.maximum(n, 1).astype(F32) / exact)
                     / math.log(REL_MAX_DIST / exact) * (NUM_BUCKETS - exact)).astype(jnp.int32)
    return jnp.where(n < exact, n, jnp.minimum(log_b, NUM_BUCKETS - 1))


TQ = Q_BLOCK
HPG = NSA_HEADS // NSA_KV
INT_MIN = -2 ** 31


def _flash_step(q2, k, v, bias, mask, scale, carry):
    m, l, acc = carry
    nk = k.shape[0]
    s = lax.dot_general(q2, k, (((1,), (1,)), ((), ())), preferred_element_type=F32)
    s = s.reshape(HPG, TQ, nk) * scale + bias
    s = jnp.where(mask[None], s, NEG)
    m_new = jnp.maximum(m, jnp.max(s, axis=-1, keepdims=True))
    p = jnp.where(mask[None], jnp.exp(s - m_new), 0.0)
    alpha = jnp.exp(m - m_new)
    l = alpha * l + jnp.sum(p, axis=-1, keepdims=True)
    pv = jnp.dot(p.reshape(HPG * TQ, nk).astype(BF16), v, preferred_element_type=F32)
    acc = alpha * acc + pv.reshape(HPG, TQ, v.shape[1])
    return m_new, l, acc


def _flash_init(dh):
    return (jnp.full((HPG, TQ, 1), NEG, F32), jnp.zeros((HPG, TQ, 1), F32), jnp.zeros((HPG, TQ, dh), F32))


def _flash_out(carry):
    m, l, acc = carry
    return acc / jnp.maximum(l, 1e-30)


def _causal_blocks(i, q2, k_ref, v_ref, mask_ref, tiles_ref, far_ref, scale):
    dh = k_ref.shape[-1]
    a = lax.broadcasted_iota(jnp.int32, (TQ, TQ), 0)
    b = lax.broadcasted_iota(jnp.int32, (TQ, TQ), 1)

    def blk(ref, jb):
        return ref[pl.ds(pl.multiple_of(jb * TQ, TQ), TQ), :]

    carry = _flash_step(q2, blk(k_ref, i), blk(v_ref, i), tiles_ref[:, 0],
                        jnp.logical_and(mask_ref[i] > 0.5, a >= b), scale, _flash_init(dh))
    jp = jnp.maximum(i - 1, 0)
    carry = _flash_step(q2, blk(k_ref, jp), blk(v_ref, jp), tiles_ref[:, 1],
                        jnp.logical_and(mask_ref[jp] > 0.5, i >= 1), scale, carry)
    far = far_ref[...][:, :, :1]
    n_far = jnp.maximum(i - 1, 0)

    def pair(jp2, c):
        rows = pl.ds(pl.multiple_of(jp2 * (2 * TQ), 2 * TQ), 2 * TQ)
        mask = jnp.concatenate([mask_ref[2 * jp2], mask_ref[2 * jp2 + 1]], axis=1) > 0.5
        return _flash_step(q2, k_ref[rows, :], v_ref[rows, :], far, mask, scale, c)

    def single(_, c):
        jb = n_far - 1
        return _flash_step(q2, blk(k_ref, jb), blk(v_ref, jb), far, mask_ref[jb] > 0.5, scale, c)

    carry = lax.fori_loop(0, n_far // 2, pair, carry)
    return lax.fori_loop(0, n_far % 2, single, carry)


def _nsa_prompt_kernel(q_ref, kc_ref, vc_ref, ks_ref, vs_ref, kw_ref, vw_ref, pool_ref, bc_ref, tiles_ref,
                       far_ref, gate_ref, o_ref, kcs, vcs, exp_s, mask_s):
    i = pl.program_id(2)
    S = ks_ref.shape[3]
    nb = S // CMP_BLOCK
    scale = NSA_DH ** -0.5

    @pl.when(i == 0)
    def _():
        kr = kc_ref[0, 0, 0].reshape(nb, CMP_BLOCK, NSA_DH) * pool_ref[0][None]
        vr = vc_ref[0, 0, 0].reshape(nb, CMP_BLOCK, NSA_DH) * pool_ref[1][None]
        kcs[...] = jnp.sum(kr, axis=1).astype(BF16)
        vcs[...] = jnp.sum(vr, axis=1).astype(BF16)
        n_i = lax.broadcasted_iota(jnp.int32, (nb, S), 0)
        s_i = lax.broadcasted_iota(jnp.int32, (nb, S), 1)
        exp_s[...] = jnp.where(s_i // CMP_BLOCK == n_i, 1.0, 0.0).astype(BF16)

    q2 = q_ref[0].reshape(HPG * TQ, NSA_DH)
    t0 = i * TQ

    s_c = lax.dot_general(q2, kcs[...], (((1,), (1,)), ((), ())), preferred_element_type=F32)
    s_c = s_c.reshape(HPG, TQ, nb) * scale + bc_ref[...]
    t_i = t0 + lax.broadcasted_iota(jnp.int32, (TQ, nb), 0)
    n_i = lax.broadcasted_iota(jnp.int32, (TQ, nb), 1)
    vis = t_i >= n_i * CMP_BLOCK + (CMP_BLOCK - 1)
    s_c = jnp.where(vis[None], s_c, NEG)
    e_c = jnp.where(vis[None], jnp.exp(s_c - jnp.max(s_c, axis=-1, keepdims=True)), 0.0)
    p_c = e_c / jnp.maximum(jnp.sum(e_c, axis=-1, keepdims=True), 1e-30)
    o_c = jnp.dot(p_c.reshape(HPG * TQ, nb).astype(BF16), vcs[...], preferred_element_type=F32)

    nt = (((1,), (1,)), ((), ()))
    eye = lambda n: jnp.where(lax.broadcasted_iota(jnp.int32, (n, n), 0) == lax.broadcasted_iota(jnp.int32, (n, n), 1),
                              1.0, 0.0)
    imp_t = lax.dot_general(eye(nb), jnp.sum(p_c, axis=0), nt, preferred_element_type=F32,
                            precision=lax.Precision.HIGHEST)
    n_t = lax.broadcasted_iota(jnp.int32, (nb, TQ), 0)
    t_t = t0 + lax.broadcasted_iota(jnp.int32, (nb, TQ), 1)
    cur = n_t == t_t // CMP_BLOCK
    work = jnp.where(jnp.logical_and(t_t >= n_t * CMP_BLOCK + (CMP_BLOCK - 1), jnp.logical_not(cur)), imp_t, -1.0)
    sel_t = jnp.where(cur, 1.0, 0.0)
    n_f = n_t.astype(F32)
    for _ in range(min(N_SEL - 1, nb)):
        mx = jnp.max(work, axis=0, keepdims=True)
        first = jnp.min(jnp.where(work == mx, n_f, float(nb)), axis=0, keepdims=True)
        pick = n_f == first
        sel_t = jnp.where(jnp.logical_and(pick, mx >= 0.0), 1.0, sel_t)
        work = jnp.where(pick, -2.0, work)
    sel = lax.dot_general(eye(TQ).astype(BF16), sel_t.astype(BF16), nt, preferred_element_type=F32)
    selx = jnp.dot(sel.astype(BF16), exp_s[...], preferred_element_type=F32)
    for jb in range(S // TQ):
        mask_s[jb] = selx[:, jb * TQ:(jb + 1) * TQ]

    o_s = _flash_out(_causal_blocks(i, q2, ks_ref.at[0, 0, 0], vs_ref.at[0, 0, 0], mask_s, tiles_ref, far_ref,
                                    scale))

    nw = WINDOW + TQ
    rows = pl.ds(pl.multiple_of(t0, TQ), nw)
    s_w = lax.dot_general(q2, kw_ref[0, 0, 0, rows, :], (((1,), (1,)), ((), ())), preferred_element_type=F32)
    far_w = jnp.broadcast_to(far_ref[...][:, :, :1], (HPG, TQ, nw - 2 * TQ))
    s_w = s_w.reshape(HPG, TQ, nw) * scale + jnp.concatenate([far_w, tiles_ref[:, 1], tiles_ref[:, 0]], axis=-1)
    c_w = lax.broadcasted_iota(jnp.int32, (TQ, nw), 1)
    d_w = lax.broadcasted_iota(jnp.int32, (TQ, nw), 0) + WINDOW - c_w
    in_w = jnp.logical_and(jnp.logical_and(d_w >= 0, d_w < WINDOW), t0 - WINDOW + c_w >= 0)
    s_w = jnp.where(in_w[None], s_w, NEG)
    e_w = jnp.where(in_w[None], jnp.exp(s_w - jnp.max(s_w, axis=-1, keepdims=True)), 0.0)
    o_w = jnp.dot(e_w.reshape(HPG * TQ, nw).astype(BF16), vw_ref[0, 0, 0, rows, :], preferred_element_type=F32)
    o_w = o_w.reshape(HPG, TQ, NSA_DH) / jnp.maximum(jnp.sum(e_w, axis=-1, keepdims=True), 1e-30)

    g = jax.nn.sigmoid(gate_ref[0])
    o = g[..., 0:1] * o_c.reshape(HPG, TQ, NSA_DH) + g[..., 1:2] * o_s + g[..., 2:3] * o_w
    for j in range(HPG):
        o_ref[0, :, j * NSA_DH:(j + 1) * NSA_DH] = o[j].astype(o_ref.dtype)


def _bias_tables(rel_bias, S):
    assert 2 * TQ - (TQ - 1) >= REL_MAX_DIST
    nb = S // CMP_BLOCK
    t = jnp.arange(S)[:, None]
    bc = rel_bias[_rel_bucket(t - (jnp.arange(nb)[None, :] * CMP_BLOCK + CMP_BLOCK - 1))]
    d = jnp.arange(2)[:, None, None] * TQ + jnp.arange(TQ)[None, :, None] - jnp.arange(TQ)[None, None, :]
    tiles = rel_bias[_rel_bucket(d)]
    far = jnp.broadcast_to(rel_bias[NUM_BUCKETS - 1][:, None, None], (rel_bias.shape[1], 1, TQ))
    return bc.transpose(2, 0, 1), tiles.transpose(3, 0, 1, 2), far


def nsa_prompt(q_n, cmp_t, sel_t, win_t, pool, g_n, bias_c, tiles, far):
    B, H, S, dh = q_n.shape
    G = NSA_KV
    nb = S // CMP_BLOCK
    assert win_t.shape[3] == S + WINDOW and WINDOW >= 2 * TQ
    kv = lambda c, n=S: pl.BlockSpec((1, 1, 1, n, dh), lambda b, g, i: (b, c, g, 0, 0))
    return pl.pallas_call(
        _nsa_prompt_kernel,
        out_shape=jax.ShapeDtypeStruct((B, S, H * dh), BF16),
        grid=(B, G, S // TQ),
        in_specs=[pl.BlockSpec((1, HPG, TQ, dh), lambda b, g, i: (b, g, i, 0)),
                  kv(0), kv(1), kv(0), kv(1), kv(0, S + WINDOW), kv(1, S + WINDOW),
                  pl.BlockSpec((2, CMP_BLOCK, dh), lambda b, g, i: (0, 0, 0)),
                  pl.BlockSpec((HPG, TQ, nb), lambda b, g, i: (g, i, 0)),
                  pl.BlockSpec((HPG, 2, TQ, TQ), lambda b, g, i: (g, 0, 0, 0)),
                  pl.BlockSpec((HPG, 1, TQ), lambda b, g, i: (g, 0, 0)),
                  pl.BlockSpec((1, HPG, TQ, 3), lambda b, g, i: (b, g, i, 0))],
        out_specs=pl.BlockSpec((1, TQ, HPG * dh), lambda b, g, i: (b, i, g)),
        scratch_shapes=[pltpu.VMEM((nb, dh), BF16), pltpu.VMEM((nb, dh), BF16), pltpu.VMEM((nb, S), BF16),
                        pltpu.VMEM((S // TQ, TQ, TQ), F32)],
        compiler_params=_cparams(("parallel", "parallel", "arbitrary")),
        name="nsa_prompt",
    )(q_n, cmp_t, cmp_t, sel_t, sel_t, win_t, win_t,
      jnp.broadcast_to(pool[:, :, None], (2, CMP_BLOCK, dh)), bias_c, tiles, far, g_n)


def _dsa_prompt_kernel(qi_ref, ki_ref, wi_ref, qd_ref, kd_ref, vd_ref, tiles_ref, far_ref, o_ref,
                       key_s, mask_s, *, topk):
    i = pl.program_id(1)
    S = ki_ref.shape[1]
    nkb = S // TQ
    t0 = i * TQ
    a = lax.broadcasted_iota(jnp.int32, (TQ, TQ), 0)
    b = lax.broadcasted_iota(jnp.int32, (TQ, TQ), 1)

    w = wi_ref[0] * (IDX_DH ** -0.5 * IDX_HEADS ** -0.5)
    w_cols = [jnp.broadcast_to(w[:, h:h + 1], (TQ, TQ)) for h in range(IDX_HEADS)]
    for jb in range(nkb):
        @pl.when(jb <= i)
        def _():
            kk = ki_ref[0, jb * TQ:(jb + 1) * TQ, :]
            acc = jnp.zeros((TQ, TQ), F32)
            for h in range(IDX_HEADS):
                s = lax.dot_general(qi_ref[0, h], kk, (((1,), (1,)), ((), ())), preferred_element_type=F32)
                acc = acc + jnp.maximum(s, 0.0) * w_cols[h]
            score = jnp.where(jb * TQ + b <= t0 + a, acc, NEG) + 0.0
            bits = pltpu.bitcast(score, jnp.int32)
            key_s[jb] = jnp.where(bits < 0, bits ^ 0x7FFFFFFF, bits)

        @pl.when(jb > i)
        def _():
            bits = pltpu.bitcast(jnp.full((TQ, TQ), NEG, F32), jnp.int32)
            key_s[jb] = bits ^ 0x7FFFFFFF

    half = nkb // 2
    if half * TQ >= topk:
        thr, last = lax.cond(i < half, lambda: _kth_threshold(key_s[:half], topk),
                             lambda: _kth_threshold(key_s[...], topk))
    else:
        thr, last = _kth_threshold(key_s[...], topk)
    keys = key_s[...]
    pos = (lax.broadcasted_iota(jnp.int32, (nkb, TQ, TQ), 0) * TQ
           + lax.broadcasted_iota(jnp.int32, (nkb, TQ, TQ), 2))
    chosen = jnp.logical_or(keys > thr, jnp.logical_and(keys == thr, pos <= last))
    mask_s[...] = jnp.where(chosen, 1.0, 0.0)

    scale = DSA_DH ** -0.5
    for g in range(DSA_KV):
        q2 = qd_ref[0, g * HPG:(g + 1) * HPG].reshape(HPG * TQ, DSA_DH)
        o = _flash_out(_causal_blocks(i, q2, kd_ref.at[0, 0, g], vd_ref.at[0, 0, g], mask_s,
                                      tiles_ref.at[g * HPG:(g + 1) * HPG], far_ref.at[g * HPG:(g + 1) * HPG],
                                      scale))
        for j in range(HPG):
            h = g * HPG + j
            o_ref[0, :, h * DSA_DH:(h + 1) * DSA_DH] = o[j].astype(o_ref.dtype)


def dsa_prompt(q_i, k_i, w_i, q_d, dkv_t, tiles, far):
    B, H, S, dh = q_d.shape
    G = DSA_KV
    topk = min(DSA_TOPK_MAX, S // 4)
    heads = pl.BlockSpec((1, H, TQ, dh), lambda b, i: (b, 0, i, 0))
    kv = lambda c: pl.BlockSpec((1, 1, G, S, dh), lambda b, i: (b, c, 0, 0, 0))
    return pl.pallas_call(
        functools.partial(_dsa_prompt_kernel, topk=topk),
        out_shape=jax.ShapeDtypeStruct((B, S, H * dh), BF16),
        grid=(B, S // TQ),
        in_specs=[heads,
                  pl.BlockSpec((1, S, k_i.shape[2]), lambda b, i: (b, 0, 0)),
                  pl.BlockSpec((1, TQ, H), lambda b, i: (b, i, 0)),
                  heads, kv(0), kv(1),
                  pl.BlockSpec((H, 2, TQ, TQ), lambda b, i: (0, 0, 0, 0)),
                  pl.BlockSpec((H, 1, TQ), lambda b, i: (0, 0, 0))],
        out_specs=pl.BlockSpec((1, TQ, H * dh), lambda b, i: (b, i, 0)),
        scratch_shapes=[pltpu.VMEM((S // TQ, TQ, TQ), jnp.int32), pltpu.VMEM((S // TQ, TQ, TQ), F32)],
        compiler_params=_cparams(("parallel", "arbitrary")),
        name="dsa_prompt",
    )(q_i, k_i, w_i, q_d, dkv_t, dkv_t, tiles, far)


def mixer_prompt(rel_bias, pool, q_n, cmp, sel, win, g_n, q_d, dkv, q_i, k_i, w_i):
    B, S = q_n.shape[:2]
    heads_first = lambda a: a.transpose(0, 2, 1, 3)
    kv_t = lambda a, dt: a.transpose(0, 2, 3, 1, 4).astype(dt)
    bias_c, tiles, far = _bias_tables(rel_bias, S)
    win_t = jnp.pad(kv_t(win, BF16), ((0, 0), (0, 0), (0, 0), (WINDOW, 0), (0, 0)))
    o_n = nsa_prompt(heads_first(q_n).astype(BF16), kv_t(cmp, F32), kv_t(sel, BF16), win_t, pool,
                     heads_first(g_n), bias_c[:NSA_HEADS], tiles[:NSA_HEADS], far[:NSA_HEADS])
    o_d = dsa_prompt(heads_first(q_i).astype(BF16), k_i.astype(BF16), w_i, heads_first(q_d).astype(BF16),
                     kv_t(dkv, BF16), tiles[NSA_HEADS:], far[NSA_HEADS:])
    return o_n, o_d


KVH = NSA_KV * NSA_DH
PAGES_PER_STEP = 8


def _pad_groups(q, n_heads):
    R, dh = q.shape
    G = n_heads // HPG
    col_g = lax.broadcasted_iota(jnp.int32, (R, G * dh), 1) // dh
    row_g = (lax.broadcasted_iota(jnp.int32, (R, G * dh), 0) % n_heads) // HPG
    return jnp.where(col_g == row_g, jnp.concatenate([q] * G, axis=1), 0.0).astype(BF16)


def _own_group(o_pad, n_heads):
    R = o_pad.shape[0]
    G = n_heads // HPG
    dh = o_pad.shape[1] // G
    row_g = (lax.broadcasted_iota(jnp.int32, (R, dh), 0) % n_heads) // HPG
    out = o_pad[:, :dh]
    for g in range(1, G):
        out = jnp.where(row_g == g, o_pad[:, g * dh:(g + 1) * dh], out)
    return out


def _order_key(x):
    bits = pltpu.bitcast(x, jnp.int32)
    return jnp.where(bits < 0, bits ^ 0x7FFFFFFF, bits)


def _kth_threshold(keys, topk):
    n, R, L = keys.shape

    def count(pred):
        c = jnp.sum(jnp.where(pred, 1.0, 0.0), axis=0)
        return jnp.sum(c, axis=-1, keepdims=True)[None]

    def bit_body(i, pat):
        cand = pat | lax.shift_left(jnp.int32(1), 31 - i)
        return jnp.where(count(keys >= (cand ^ INT_MIN)) >= topk, cand, pat)

    thr = lax.fori_loop(0, 32, bit_body, jnp.zeros((1, R, 1), jnp.int32)) ^ INT_MIN
    need = topk - count(keys > thr)
    pos = lax.broadcasted_iota(jnp.int32, (n, R, L), 0) * L + lax.broadcasted_iota(jnp.int32, (n, R, L), 2)
    tie = keys == thr
    nbits = (n * L - 1).bit_length()

    def tie_body(i, last):
        cand = last | lax.shift_left(jnp.int32(1), nbits - 1 - i)
        return jnp.where(count(jnp.logical_and(tie, pos < cand)) < need, cand, last)

    crowded = jnp.max(jnp.where(count(tie) > need, 1.0, 0.0)) > 0.5
    last = lax.cond(crowded,
                    lambda: lax.fori_loop(0, nbits, tie_body, jnp.zeros((1, R, 1), jnp.int32)),
                    lambda: jnp.full((1, R, 1), n * L, jnp.int32))
    return thr, last


def _cmp_pages_kernel(pt_ref, pw_ref, *refs):
    o_ref = refs[-1]
    acc_k = acc_v = None
    for k, page in enumerate(refs[:-1]):
        x = page[0]
        yk = jnp.dot(x[:KVH], pw_ref[k, 0], preferred_element_type=F32, precision=lax.Precision.HIGHEST)
        yv = jnp.dot(x[KVH:], pw_ref[k, 1], preferred_element_type=F32, precision=lax.Precision.HIGHEST)
        acc_k = yk if acc_k is None else acc_k + yk
        acc_v = yv if acc_v is None else acc_v + yv
    o_ref[0, 0, :KVH, :] = acc_k
    o_ref[0, 0, KVH:, :] = acc_v


def cmp_pages(pt_flat, cache_t, pool, B):
    n_pages = pt_flat.shape[0] // B
    pgs = min(PAGES_PER_STEP, n_pages)
    bpp = PAGE_SIZE // CMP_BLOCK
    Wd = cache_t.shape[1]
    row = jnp.arange(PAGE_SIZE)
    col = jnp.arange(pgs * bpp)
    place = (col[None, None, :] == (jnp.arange(pgs)[:, None, None] * bpp + (row // CMP_BLOCK)[None, :, None]))
    pw = jnp.where(place[:, None], pool[:, row % CMP_BLOCK][None, :, :, None], 0.0)
    page = lambda k: pl.BlockSpec((1, Wd, PAGE_SIZE), lambda b, s, pt: (pt[b * n_pages + s * pgs + k], 0, 0))
    out = pl.pallas_call(
        _cmp_pages_kernel,
        out_shape=jax.ShapeDtypeStruct((B, n_pages // pgs, Wd, pgs * bpp), F32),
        grid_spec=pltpu.PrefetchScalarGridSpec(
            num_scalar_prefetch=1, grid=(B, n_pages // pgs),
            in_specs=[pl.BlockSpec((pgs, 2, PAGE_SIZE, pgs * bpp), lambda b, s, pt: (0, 0, 0, 0))]
                     + [page(k) for k in range(pgs)],
            out_specs=pl.BlockSpec((1, 1, Wd, pgs * bpp), lambda b, s, pt: (b, s, 0, 0))),
        compiler_params=_cparams(("parallel", "parallel")),
        name="cmp_pages",
    )(pt_flat, pw, *([cache_t] * pgs))
    return out.transpose(0, 2, 1, 3).reshape(B, Wd, n_pages * bpp)


def _nsa_sample_pre_kernel(q_ref, kvc_ref, wb_ref, new_ref, bc_ref, bw_ref, bn_ref, oc_ref, ow_ref, sel_ref,
                           *, past, n_new):
    H = NSA_HEADS
    R = q_ref.shape[1]
    nb = kvc_ref.shape[2]
    W = wb_ref.shape[2]
    scale = NSA_DH ** -0.5
    nt = (((1,), (1,)), ((), ()))
    qp = _pad_groups(q_ref[0], H)

    kvc = kvc_ref[0]
    s_c = jnp.dot(qp, kvc[:KVH].astype(BF16), preferred_element_type=F32) * scale + bc_ref[...]
    q_pos = past + lax.broadcasted_iota(jnp.int32, (R, nb), 0) // H
    n_i = lax.broadcasted_iota(jnp.int32, (R, nb), 1)
    vis = q_pos >= n_i * CMP_BLOCK + (CMP_BLOCK - 1)
    s_c = jnp.where(vis, s_c, NEG)
    e_c = jnp.where(vis, jnp.exp(s_c - jnp.max(s_c, axis=-1, keepdims=True)), 0.0)
    p_c = e_c / jnp.maximum(jnp.sum(e_c, axis=-1, keepdims=True), 1e-30)
    oc_ref[0] = _own_group(lax.dot_general(p_c.astype(BF16), kvc[KVH:].astype(BF16), nt,
                                           preferred_element_type=F32), H)

    same = (lax.broadcasted_iota(jnp.int32, (R, R), 0) // HPG == lax.broadcasted_iota(jnp.int32, (R, R), 1) // HPG)
    imp = jnp.dot(jnp.where(same, 1.0, 0.0), p_c, preferred_element_type=F32, precision=lax.Precision.HIGHEST)
    cur = n_i == q_pos // CMP_BLOCK
    work = jnp.where(jnp.logical_and(vis, jnp.logical_not(cur)), imp, -1.0)
    sel = jnp.where(cur, 1.0, 0.0)
    n_f = n_i.astype(F32)
    for _ in range(min(N_SEL - 1, nb)):
        mx = jnp.max(work, axis=-1, keepdims=True)
        first = jnp.min(jnp.where(work == mx, n_f, float(nb)), axis=-1, keepdims=True)
        pick = n_f == first
        sel = jnp.where(jnp.logical_and(pick, mx >= 0.0), 1.0, sel)
        work = jnp.where(pick, -2.0, work)
    sel_ref[0] = sel.astype(sel_ref.dtype)

    wb, new = wb_ref[0], new_ref[0]
    s1 = jnp.dot(qp, wb[:KVH].astype(BF16), preferred_element_type=F32) * scale + bw_ref[...]
    i1 = lax.broadcasted_iota(jnp.int32, (R, W), 1)
    d1 = W + lax.broadcasted_iota(jnp.int32, (R, W), 0) // H - i1
    m1 = jnp.logical_and(jnp.logical_and(d1 >= 0, d1 < WINDOW), past - W + i1 >= 0)
    L = new.shape[0]
    s2 = lax.dot_general(qp, new[:, :KVH].astype(BF16), nt, preferred_element_type=F32) * scale + bn_ref[...]
    j2 = lax.broadcasted_iota(jnp.int32, (R, L), 1)
    d2 = lax.broadcasted_iota(jnp.int32, (R, L), 0) // H - j2
    m2 = jnp.logical_and(jnp.logical_and(d2 >= 0, d2 < WINDOW), j2 < n_new)
    s1 = jnp.where(m1, s1, NEG)
    s2 = jnp.where(m2, s2, NEG)
    mx = jnp.maximum(jnp.max(s1, axis=-1, keepdims=True), jnp.max(s2, axis=-1, keepdims=True))
    e1 = jnp.where(m1, jnp.exp(s1 - mx), 0.0)
    e2 = jnp.where(m2, jnp.exp(s2 - mx), 0.0)
    den = jnp.sum(e1, axis=-1, keepdims=True) + jnp.sum(e2, axis=-1, keepdims=True)
    o_w = (lax.dot_general(e1.astype(BF16), wb[KVH:].astype(BF16), nt, preferred_element_type=F32)
           + jnp.dot(e2.astype(BF16), new[:, KVH:].astype(BF16), preferred_element_type=F32))
    ow_ref[0] = _own_group(o_w / jnp.maximum(den, 1e-30), H)


def nsa_sample_pre(q, kvc, wbuf, new_win, bias_c, bias_w, bias_new, past, n_new):
    B, R, dh = q.shape
    nb, W, L = kvc.shape[2], wbuf.shape[2], new_win.shape[1]
    per_b = lambda *s: pl.BlockSpec((1,) + s, lambda b: (b,) + (0,) * len(s))
    const = lambda *s: pl.BlockSpec(s, lambda b: (0,) * len(s))
    return pl.pallas_call(
        functools.partial(_nsa_sample_pre_kernel, past=past, n_new=n_new),
        out_shape=(jax.ShapeDtypeStruct((B, R, dh), F32), jax.ShapeDtypeStruct((B, R, dh), F32),
                   jax.ShapeDtypeStruct((B, R, nb), BF16)),
        grid=(B,),
        in_specs=[per_b(R, dh), per_b(2 * KVH, nb), per_b(2 * KVH, W), per_b(L, 2 * KVH),
                  const(R, nb), const(R, W), const(R, L)],
        out_specs=(per_b(R, dh), per_b(R, dh), per_b(R, nb)),
        compiler_params=_cparams(("parallel",)),
        name="nsa_sample_pre",
    )(q, kvc, wbuf, new_win, bias_c, bias_w, bias_new)


def _index_scores(qi, w, keys, n_heads, feature_major):
    contract = (((1,), (0,)), ((), ())) if feature_major else (((1,), (1,)), ((), ()))
    s = lax.dot_general(qi, keys.astype(BF16), contract, preferred_element_type=F32)
    v = jnp.maximum(s * IDX_DH ** -0.5, 0.0) * w
    T = qi.shape[0] // n_heads
    return jnp.sum(v.reshape(T, n_heads, s.shape[1]), axis=1) * n_heads ** -0.5 + 0.0


def _idx_scores_kernel(pt_ref, q_ref, w_ref, *refs):
    o_ref = refs[-1]
    qi = q_ref[0].astype(BF16)
    for k, page in enumerate(refs[:-1]):
        o_ref[0, k] = _index_scores(qi, w_ref[0], page[0], IDX_HEADS, True)


def idx_scores(pt_flat, q_i, w_i, cache_t):
    B, R, dh = q_i.shape
    T = R // IDX_HEADS
    n_pages = pt_flat.shape[0] // B
    pgs = min(2 * PAGES_PER_STEP, n_pages)
    page = lambda k: pl.BlockSpec((1, dh, PAGE_SIZE), lambda b, s, pt: (pt[b * n_pages + s * pgs + k], 0, 0))
    return pl.pallas_call(
        _idx_scores_kernel,
        out_shape=jax.ShapeDtypeStruct((B, n_pages, T, PAGE_SIZE), F32),
        grid_spec=pltpu.PrefetchScalarGridSpec(
            num_scalar_prefetch=1, grid=(B, n_pages // pgs),
            in_specs=[pl.BlockSpec((1, R, dh), lambda b, s, pt: (b, 0, 0)),
                      pl.BlockSpec((1, R, 1), lambda b, s, pt: (b, 0, 0))] + [page(k) for k in range(pgs)],
            out_specs=pl.BlockSpec((1, pgs, T, PAGE_SIZE), lambda b, s, pt: (b, s, 0, 0))),
        compiler_params=_cparams(("parallel", "parallel")),
        name="idx_scores",
    )(pt_flat, q_i, w_i, *([cache_t] * pgs))


def _dsa_sample_pre_kernel(sp_ref, q_ref, w_ref, kn_ref, sn_ref, thr_ref, last_ref, *, topk, n_new):
    T, L = sp_ref.shape[2], sp_ref.shape[3]
    sc = _index_scores(q_ref[0].astype(BF16), w_ref[0], kn_ref[0], IDX_HEADS, False)
    j = lax.broadcasted_iota(jnp.int32, (T, L), 1)
    t = lax.broadcasted_iota(jnp.int32, (T, L), 0)
    sc = jnp.where(jnp.logical_and(j <= t, j < n_new), sc, NEG)
    sn_ref[0, 0] = sc
    keys = jnp.concatenate([_order_key(sp_ref[0]), _order_key(sc)[None]], axis=0)
    thr, last = _kth_threshold(keys, topk)
    thr_ref[0] = jnp.broadcast_to(thr[0], (T, L))
    last_ref[0] = jnp.broadcast_to(last[0], (T, L))


def dsa_sample_pre(scores_past, q_i, w_i, new_ki, topk, n_new):
    B, n_pages, T, L = scores_past.shape
    R, dh = q_i.shape[1:]
    per_b = lambda *s: pl.BlockSpec((1,) + s, lambda b: (b,) + (0,) * len(s))
    return pl.pallas_call(
        functools.partial(_dsa_sample_pre_kernel, topk=topk, n_new=n_new),
        out_shape=(jax.ShapeDtypeStruct((B, 1, T, L), F32), jax.ShapeDtypeStruct((B, T, L), jnp.int32),
                   jax.ShapeDtypeStruct((B, T, L), jnp.int32)),
        grid=(B,),
        in_specs=[per_b(n_pages, T, L), per_b(R, dh), per_b(R, 1), per_b(L, dh)],
        out_specs=(per_b(1, T, L), per_b(T, L), per_b(T, L)),
        compiler_params=_cparams(("parallel",)),
        name="dsa_sample_pre",
    )(scores_past, q_i, w_i, new_ki)


def _paged_attn_kernel(pt_ref, *refs, mode, pgs, n_pages, n_heads, n_new, dh):
    if mode == "blocks":
        q_ref, new_ref, far_ref, tl_ref, tn_ref, sel_ref, oc_ref, ow_ref, gate_ref = refs[:9]
        rest = refs[9:]
    else:
        q_ref, new_ref, far_ref, tl_ref, tn_ref, sp_ref, sn_ref, thr_ref, last_ref = refs[:9]
        rest = refs[9:]
    pages, o_ref = rest[:pgs], rest[pgs]
    qp_s, m_s, l_s, acc_s = rest[pgs + 1:]
    step, n_steps = pl.program_id(1), pl.num_programs(1)
    R = q_ref.shape[1]
    T = R // n_heads
    L = PAGE_SIZE
    scale = dh ** -0.5

    @pl.when(step == 0)
    def _():
        qp_s[...] = _pad_groups(q_ref[0], n_heads)
        m_s[...] = jnp.full(m_s.shape, NEG, F32)
        l_s[...] = jnp.zeros(l_s.shape, F32)
        acc_s[...] = jnp.zeros(acc_s.shape, F32)

    def update(kv, feature_major, bias, mask):
        nt = (((1,), (1,)), ((), ()))
        if feature_major:
            s = jnp.dot(qp_s[...], kv[:KVH].astype(BF16), preferred_element_type=F32)
        else:
            s = lax.dot_general(qp_s[...], kv[:, :KVH].astype(BF16), nt, preferred_element_type=F32)
        s = jnp.where(mask, s * scale + bias, NEG)
        m_old = m_s[...]
        m_new = jnp.maximum(m_old, jnp.max(s, axis=-1, keepdims=True))
        p = jnp.where(mask, jnp.exp(s - m_new), 0.0)
        alpha = jnp.exp(m_old - m_new)
        l_s[...] = alpha * l_s[...] + jnp.sum(p, axis=-1, keepdims=True)
        if feature_major:
            pv = lax.dot_general(p.astype(BF16), kv[KVH:].astype(BF16), nt, preferred_element_type=F32)
        else:
            pv = jnp.dot(p.astype(BF16), kv[:, KVH:].astype(BF16), preferred_element_type=F32)
        acc_s[...] = alpha * acc_s[...] + pv
        m_s[...] = m_new

    def rows_of_tokens(x):
        return jnp.broadcast_to(x[:, None, :], (T, n_heads, L)).reshape(R, L)

    def chosen(scores, page):
        key = _order_key(scores)
        pos = page * L + lax.broadcasted_iota(jnp.int32, (T, L), 1)
        thr = thr_ref[0]
        ch = jnp.logical_or(key > thr, jnp.logical_and(key == thr, pos <= last_ref[0]))
        return rows_of_tokens(jnp.where(ch, 1.0, 0.0)) > 0.5

    for k in range(pgs):
        page = step * pgs + k
        bias = jnp.where(page == n_pages - 1, tl_ref[...], far_ref[...])
        if mode == "blocks":
            nb = sel_ref.shape[2]
            blk = page * (L // CMP_BLOCK) + lax.broadcasted_iota(jnp.int32, (nb, L), 1) // CMP_BLOCK
            expand = jnp.where(lax.broadcasted_iota(jnp.int32, (nb, L), 0) == blk, 1.0, 0.0).astype(BF16)
            mask = jnp.dot(sel_ref[0], expand, preferred_element_type=F32) > 0.5
        else:
            mask = chosen(sp_ref[0, k], page)
        update(pages[k][0], True, bias, mask)

    @pl.when(step == n_steps - 1)
    def _():
        j = lax.broadcasted_iota(jnp.int32, (R, L), 1)
        t = lax.broadcasted_iota(jnp.int32, (R, L), 0) // n_heads
        mask = jnp.logical_and(j <= t, j < n_new)
        if mode != "blocks":
            mask = jnp.logical_and(mask, chosen(sn_ref[0, 0], n_pages))
        update(new_ref[0], False, tn_ref[...], mask)
        o = _own_group(acc_s[...] / jnp.maximum(l_s[...], 1e-30), n_heads)
        if mode == "blocks":
            g = jax.nn.sigmoid(gate_ref[0])
            o = g[:, 0:1] * oc_ref[0] + g[:, 1:2] * o + g[:, 2:3] * ow_ref[0]
        o_ref[0] = o.astype(o_ref.dtype)


def paged_attn(mode, pt_flat, cache_t, q, new_rows, far, tile_last, tile_new, extra, n_heads, n_new):
    B, R, dh = q.shape
    n_pages = pt_flat.shape[0] // B
    pgs = min(PAGES_PER_STEP, n_pages)
    L = PAGE_SIZE
    Wd = cache_t.shape[1]
    per_b = lambda *s: pl.BlockSpec((1,) + s, lambda b, st, pt: (b,) + (0,) * len(s))
    const = lambda *s: pl.BlockSpec(s, lambda b, st, pt: (0,) * len(s))
    page = lambda k: pl.BlockSpec((1, Wd, L), lambda b, st, pt: (pt[b * n_pages + st * pgs + k], 0, 0))
    if mode == "blocks":
        sel = extra[0]
        extra_specs = [per_b(R, sel.shape[2]), per_b(R, dh), per_b(R, dh), per_b(R, 3)]
    else:
        T = R // n_heads
        extra_specs = [pl.BlockSpec((1, pgs, T, L), lambda b, st, pt: (b, st, 0, 0)), per_b(1, T, L),
                       per_b(T, L), per_b(T, L)]
    return pl.pallas_call(
        functools.partial(_paged_attn_kernel, mode=mode, pgs=pgs, n_pages=n_pages, n_heads=n_heads, n_new=n_new,
                          dh=dh),
        out_shape=jax.ShapeDtypeStruct((B, R, dh), BF16),
        grid_spec=pltpu.PrefetchScalarGridSpec(
            num_scalar_prefetch=1, grid=(B, n_pages // pgs),
            in_specs=[per_b(R, dh), per_b(L, Wd), const(R, 1), const(R, L), const(R, L)] + extra_specs
                     + [page(k) for k in range(pgs)],
            out_specs=per_b(R, dh),
            scratch_shapes=[pltpu.VMEM((R, KVH), BF16), pltpu.VMEM((R, 1), F32), pltpu.VMEM((R, 1), F32),
                            pltpu.VMEM((R, KVH), F32)]),
        compiler_params=_cparams(("parallel", "arbitrary")),
        name="paged_attn_" + mode,
    )(pt_flat, q, new_rows, far, tile_last, tile_new, *extra, *([cache_t] * pgs))


def _sample_bias(rel_bias, head0, n_heads, T, past, W, nb):
    assert PAGE_SIZE + 1 >= REL_MAX_DIST
    R = T * n_heads
    t = (jnp.arange(R) // n_heads)[:, None]
    h = head0 + (jnp.arange(R) % n_heads)[:, None]
    tab = lambda dist: rel_bias[_rel_bucket(dist), h]
    lane = jnp.arange(PAGE_SIZE)[None, :]
    return dict(far=rel_bias[NUM_BUCKETS - 1][h], last=tab(PAGE_SIZE + t - lane), new=tab(t - lane),
                cmp=tab(past + t - (jnp.arange(nb)[None, :] * CMP_BLOCK + CMP_BLOCK - 1)),
                win=tab(W + t - jnp.arange(W)[None, :]))


def mixer_sample(cache_cmp, cache_sel, cache_win, cache_dkv, cache_idx, page_table, rel_bias, pool,
                 q_n, cmp, sel, win, g_n, q_d, dkv, q_i, k_i, w_i):
    B, T = q_n.shape[:2]
    n_pages = page_table.shape[1]
    past = n_pages * PAGE_SIZE
    W = cache_win.shape[1]
    assert T < CMP_BLOCK and past % CMP_BLOCK == 0 and T <= PAGE_SIZE
    topk = min(DSA_TOPK_MAX, (past + T) // 4)
    n_phys = cache_sel.shape[0]
    nb = past // CMP_BLOCK
    pt_flat = page_table.reshape(-1)
    flat = lambda c: jnp.moveaxis(c, 1, -1).reshape(c.shape[0], -1, c.shape[1])
    rows = lambda a: a.reshape(B, T * a.shape[2], -1)
    pad_new = lambda a: jnp.pad(a.reshape(B, T, -1), ((0, 0), (0, PAGE_SIZE - T), (0, 0)))
    bn = _sample_bias(rel_bias, 0, NSA_HEADS, T, past, W, nb)
    bd = _sample_bias(rel_bias, NSA_HEADS, DSA_HEADS, T, past, W, nb)

    kvc = cmp_pages(pt_flat, flat(cache_cmp), pool, B)
    o_c, o_w, selm = nsa_sample_pre(rows(q_n), kvc, flat(cache_win), pad_new(win), bn["cmp"], bn["win"], bn["new"],
                                    past, T)
    o_n = paged_attn("blocks", pt_flat, flat(cache_sel), rows(q_n), pad_new(sel), bn["far"], bn["last"], bn["new"],
                     (selm, o_c, o_w, g_n.reshape(B, T * NSA_HEADS, 3)), NSA_HEADS, T)

    qi, wi = rows(q_i), w_i.reshape(B, T * IDX_HEADS, 1)
    sp = idx_scores(pt_flat, qi, wi, flat(cache_idx))
    sn, thr, last = dsa_sample_pre(sp, qi, wi, pad_new(k_i), topk, T)
    o_d = paged_attn("thr", pt_flat, flat(cache_dkv), rows(q_d), pad_new(dkv), bd["far"], bd["last"], bd["new"],
                     (sp, sn, thr, last), DSA_HEADS, T)
    new_win = jnp.concatenate([cache_win[:, T:], win], axis=1)
    return o_n.reshape(B, T, -1), o_d.reshape(B, T, -1), new_win


def _project(h, w_q, w_kv, w_s, w_g):
    B, S, D = h.shape
    h2 = h.reshape(B * S, D)
    zq = matmul(h2, w_q, out_dtype=F32)
    zkv = matmul(h2, w_kv, out_dtype=F32)
    zs = matmul(h2, w_s, out_dtype=F32)
    zg = matmul(h2, w_g, out_dtype=F32)
    kvn = (B, S, 2, NSA_KV, NSA_DH)
    q_n = zq[:, :NSA_WIDTH].reshape(B, S, NSA_HEADS, NSA_DH)
    q_d = zq[:, NSA_WIDTH:NSA_WIDTH + DSA_WIDTH].reshape(B, S, DSA_HEADS, DSA_DH)
    q_i = zq[:, NSA_WIDTH + DSA_WIDTH:].reshape(B, S, IDX_HEADS, IDX_DH)
    cmp = zkv[:, :KV_W].reshape(kvn)
    sel = zkv[:, KV_W:2 * KV_W].reshape(kvn)
    win = zkv[:, 2 * KV_W:3 * KV_W].reshape(kvn)
    dkv = zkv[:, 3 * KV_W:].reshape(B, S, 2, DSA_KV, DSA_DH)
    k_i = zs[:, :IDX_DH].reshape(B, S, IDX_DH)
    w_i = zs[:, IDX_DH:IDX_DH + IDX_HEADS].reshape(B, S, IDX_HEADS)
    g_n = zs[:, IDX_DH + IDX_HEADS:].reshape(B, S, NSA_HEADS, 3)
    return (q_n, cmp, sel, win, g_n, q_d, dkv, q_i, k_i, w_i), zg


def kernel(x_prompt, x_sample, c_prompt, c_sample, cache_nsa_cmp, cache_nsa_sel, cache_nsa_win, cache_dsa_kv,
           cache_dsa_idx, page_table, rel_bias, w_mod, b_mod, g_pre_mix, g_post_mix, g_pre_ffn, g_post_ffn,
           w_in, cmp_pool, w_up_nsa, w_up_dsa, w_out, w_router, b_router, w_gu, b_gu, w_down, b_down):
    l = 0
    D = D_MODEL
    Bp, Sp, _ = x_prompt.shape
    Bs, Ss, _ = x_sample.shape

    c = jnp.concatenate([c_prompt, c_sample], 0)
    n_c = c.shape[0]
    c_pad = jnp.pad(jax.nn.silu(c), ((0, -n_c % 16), (0, 0))).astype(BF16)
    mod = matmul(c_pad, w_mod[l], b_mod[l], tn=1536)[:n_c]
    mod_p, mod_s = mod[:Bp], mod[Bp:]

    wi = w_in[l]
    seg = lambda k: wi[:, _OFF[k]:_OFF[k + 1]]
    w_q = jnp.concatenate([seg(0), seg(5), seg(7)], 1).astype(BF16)
    w_kv = jnp.concatenate([seg(1), seg(2), seg(3), seg(6)], 1).astype(BF16)
    w_s = jnp.concatenate([seg(8), seg(9), seg(4)], 1).astype(BF16)
    w_g = seg(10).astype(BF16)
    wun, wud, wo = w_up_nsa[l].astype(BF16), w_up_dsa[l].astype(BF16), w_out[l].astype(BF16)

    def mix_front(x, mod_g):
        sh1, sc1 = mod_g[:, :D], mod_g[:, D:2 * D]
        h = norm_mod(x, g_pre_mix[l], sc1, sh1)
        return _project(h, w_q, w_kv, w_s, w_g)

    def mix_back(x, mod_g, o_n, o_d, zg):
        B, S, _ = x.shape
        ga1, sh2, sc2 = mod_g[:, 2 * D:3 * D], mod_g[:, 3 * D:4 * D], mod_g[:, 4 * D:5 * D]
        u = merge_up(o_n.reshape(B * S, -1).astype(BF16), o_d.reshape(B * S, -1).astype(BF16), zg, wun, wud)
        return out_proj(u, x, wo, g_post_mix[l], g_pre_ffn[l], ga1, sc2, sh2, w_router[l], b_router[l])

    parts_p, zg_p = mix_front(x_prompt, mod_p)
    o_n, o_d = mixer_prompt(rel_bias, cmp_pool[l], *parts_p)
    x1_p, h_p, lg_p = mix_back(x_prompt, mod_p, o_n, o_d, zg_p)

    parts_s, zg_s = mix_front(x_sample, mod_s)
    o_n, o_d, new_win = mixer_sample(cache_nsa_cmp[l], cache_nsa_sel[l], cache_nsa_win[l], cache_dsa_kv[l],
                                     cache_dsa_idx[l], page_table, rel_bias, cmp_pool[l], *parts_s)
    x1_s, h_s, lg_s = mix_back(x_sample, mod_s, o_n, o_d, zg_s)

    Tp, Ts = Bp * Sp, Bs * Ss
    h_all = jnp.concatenate([h_p.reshape(Tp, D), h_s.reshape(Ts, D)], 0)
    logits = jnp.concatenate([lg_p.reshape(Tp, N_EXPERTS), lg_s.reshape(Ts, N_EXPERTS)], 0)
    probs, row_tok, blk_e, n_used, dest_of = moe_route(logits)
    out_rows = moe_experts(h_all[row_tok], blk_e, n_used, w_gu[l], b_gu[l], w_down[l], b_down[l])
    picked = lambda lo, hi: [out_rows[dest_of[lo:hi, k]] for k in range(TOP_K)]
    y_p = moe_combine(picked(0, Tp), probs[:Tp], x1_p, g_post_ffn[l], mod_p[:, 5 * D:])
    y_s = moe_combine(picked(Tp, Tp + Ts), probs[Tp:], x1_s, g_post_ffn[l], mod_s[:, 5 * D:])

    st = lambda a: a[None]
    q_n, cmp_p, sel_p, win_p, g_n, q_d, dkv_p, q_i, ki_p, w_i = parts_p
    q_n, cmp_s, sel_s, win_s, g_n, q_d, dkv_s, q_i, ki_s, w_i = parts_s
    keep = min(WINDOW, Sp)
    return (y_p, y_s, st(cmp_p), st(sel_p), st(win_p[:, Sp - keep:]), st(dkv_p), st(ki_p),
            st(cmp_s), st(sel_s), st(new_win), st(dkv_s), st(ki_s))
```

```python
import functools
import math

import jax
import jax.numpy as jnp
import numpy as np
from jax import lax
from jax.experimental import pallas as pl
from jax.experimental.pallas import tpu as pltpu

F32 = jnp.float32
BF16 = jnp.bfloat16

D_MODEL = 2048
NSA_HEADS = 16
NSA_KV = 4
NSA_DH = 64
CMP_BLOCK = 64
N_SEL = 16
WINDOW = 512
DSA_HEADS = 16
DSA_KV = 4
DSA_DH = 64
IDX_HEADS = 16
IDX_DH = 64
DSA_TOPK_MAX = 256
NUM_BUCKETS = 32
REL_MAX_DIST = 128
N_EXPERTS = 32
TOP_K = 4
D_FF = 2048
SWIGLU_LIMIT = 7.0
SWIGLU_ALPHA = 1.702
PAGE_SIZE = 128
Q_BLOCK = 128
RMS_EPS = 1e-6
NEG = -1e30
NSA_WIDTH = NSA_HEADS * NSA_DH
DSA_WIDTH = DSA_HEADS * DSA_DH
KV_W = 2 * NSA_KV * NSA_DH

_SPLIT = (NSA_WIDTH, KV_W, KV_W, KV_W, 3 * NSA_HEADS, DSA_WIDTH, 2 * DSA_KV * DSA_DH,
          IDX_HEADS * IDX_DH, IDX_DH, IDX_HEADS, 2 * D_MODEL)
_OFF = tuple(int(v) for v in np.cumsum((0,) + _SPLIT))

MOE_TM = 512
VMEM_LIMIT = 48 * 1024 * 1024


def _cparams(sem):
    return pltpu.CompilerParams(dimension_semantics=sem, vmem_limit_bytes=VMEM_LIMIT)


def _norm_mod_kernel(x_ref, g_ref, sc_ref, sh_ref, o_ref):
    x = x_ref[0]
    y = x * lax.rsqrt(jnp.mean(x * x, axis=-1, keepdims=True) + RMS_EPS) * g_ref[...]
    o_ref[0] = (y * (1.0 + sc_ref[0]) + sh_ref[0]).astype(o_ref.dtype)


def norm_mod(x, g, scale, shift):
    B, S, D = x.shape
    ts = min(S, 512)
    row = pl.BlockSpec((1, ts, D), lambda b, s: (b, s, 0))
    per_b = pl.BlockSpec((1, 1, D), lambda b, s: (b, 0, 0))
    return pl.pallas_call(
        _norm_mod_kernel,
        out_shape=jax.ShapeDtypeStruct((B, S, D), BF16),
        grid=(B, S // ts),
        in_specs=[row, pl.BlockSpec((1, D), lambda b, s: (0, 0)), per_b, per_b],
        out_specs=row,
        compiler_params=_cparams(("parallel", "parallel")),
        name="norm_mod",
    )(x, g.reshape(1, D), scale.reshape(B, 1, D), shift.reshape(B, 1, D))


def _matmul_kernel(a_ref, w_ref, b_ref, o_ref):
    acc = jnp.dot(a_ref[...], w_ref[...].astype(BF16), preferred_element_type=F32)
    o_ref[...] = (acc + b_ref[...]).astype(o_ref.dtype)


def matmul(a, w, bias=None, out_dtype=F32, tm=512, tn=512):
    M, K = a.shape
    N = w.shape[1]
    tm, tn = min(tm, M), min(tn, N)
    assert M % tm == 0 and N % tn == 0
    if bias is None:
        bias = jnp.zeros((N,), F32)
    return pl.pallas_call(
        _matmul_kernel,
        out_shape=jax.ShapeDtypeStruct((M, N), out_dtype),
        grid=(M // tm, N // tn),
        in_specs=[pl.BlockSpec((tm, K), lambda i, j: (i, 0)),
                  pl.BlockSpec((K, tn), lambda i, j: (0, j)),
                  pl.BlockSpec((1, tn), lambda i, j: (0, j))],
        out_specs=pl.BlockSpec((tm, tn), lambda i, j: (i, j)),
        compiler_params=_cparams(("parallel", "parallel")),
        name="matmul",
    )(a, w, bias.reshape(1, N))


def _merge_kernel(on_ref, od_ref, ga_ref, gb_ref, wn_ref, wd_ref, o_ref):
    a = jnp.dot(on_ref[...], wn_ref[...], preferred_element_type=F32)
    b = jnp.dot(od_ref[...], wd_ref[...], preferred_element_type=F32)
    o_ref[...] = (jax.nn.sigmoid(ga_ref[...]) * a + jax.nn.sigmoid(gb_ref[...]) * b).astype(o_ref.dtype)


def merge_up(o_n, o_d, zg, w_up_nsa, w_up_dsa, tm=512, tn=512):
    M = o_n.shape[0]
    D = w_up_nsa.shape[1]
    tm = min(tm, M)
    nj = D // tn
    return pl.pallas_call(
        _merge_kernel,
        out_shape=jax.ShapeDtypeStruct((M, D), BF16),
        grid=(M // tm, nj),
        in_specs=[pl.BlockSpec((tm, o_n.shape[1]), lambda i, j: (i, 0)),
                  pl.BlockSpec((tm, o_d.shape[1]), lambda i, j: (i, 0)),
                  pl.BlockSpec((tm, tn), lambda i, j: (i, j)),
                  pl.BlockSpec((tm, tn), lambda i, j: (i, j + nj)),
                  pl.BlockSpec((o_n.shape[1], tn), lambda i, j: (0, j)),
                  pl.BlockSpec((o_d.shape[1], tn), lambda i, j: (0, j))],
        out_specs=pl.BlockSpec((tm, tn), lambda i, j: (i, j)),
        compiler_params=_cparams(("parallel", "parallel")),
        name="merge_up",
    )(o_n, o_d, zg, zg, w_up_nsa, w_up_dsa)


def _rms(x, g):
    return x * lax.rsqrt(jnp.mean(x * x, axis=-1, keepdims=True) + RMS_EPS) * g


def _out_proj_kernel(u_ref, x_ref, w_ref, gpost_ref, gpre_ref, ga_ref, sc_ref, sh_ref, wr_ref, br_ref,
                     x1_ref, h_ref, lg_ref):
    m = jnp.dot(u_ref[0], w_ref[...], preferred_element_type=F32)
    x1 = x_ref[0] + ga_ref[0] * _rms(m, gpost_ref[...])
    x1_ref[0] = x1
    h = _rms(x1, gpre_ref[...]) * (1.0 + sc_ref[0]) + sh_ref[0]
    h_ref[0] = h.astype(h_ref.dtype)
    lg_ref[0] = jnp.dot(h, wr_ref[...], preferred_element_type=F32,
                        precision=lax.Precision.HIGHEST) + br_ref[...]


def out_proj(u, x, w_out, g_post, g_pre, gate, scale, shift, w_router, b_router):
    B, S, D = x.shape
    ts = min(S, 256)
    E = w_router.shape[1]
    row = lambda d: pl.BlockSpec((1, ts, d), lambda b, s: (b, s, 0))
    per_b = pl.BlockSpec((1, 1, D), lambda b, s: (b, 0, 0))
    vec = lambda d: pl.BlockSpec((1, d), lambda b, s: (0, 0))
    return pl.pallas_call(
        _out_proj_kernel,
        out_shape=(jax.ShapeDtypeStruct((B, S, D), F32), jax.ShapeDtypeStruct((B, S, D), BF16),
                   jax.ShapeDtypeStruct((B, S, E), F32)),
        grid=(B, S // ts),
        in_specs=[row(D), row(D), pl.BlockSpec((D, D), lambda b, s: (0, 0)), vec(D), vec(D),
                  per_b, per_b, per_b, pl.BlockSpec((D, E), lambda b, s: (0, 0)), vec(E)],
        out_specs=(row(D), row(D), row(E)),
        compiler_params=_cparams(("parallel", "parallel")),
        name="out_proj",
    )(u.reshape(B, S, D), x, w_out, g_post.reshape(1, D), g_pre.reshape(1, D), gate.reshape(B, 1, D),
      scale.reshape(B, 1, D), shift.reshape(B, 1, D), w_router, b_router.reshape(1, E))


def _expert_changed(be_ref, b):
    return jnp.logical_or(b == 0, be_ref[b] != be_ref[jnp.maximum(b - 1, 0)])


def _moe_up_kernel(be_ref, nu_ref, x_ref, wg_ref, wl_ref, bg_ref, bl_ref, o_ref, wg_s, wl_s):
    b = pl.program_id(1)

    @pl.when(_expert_changed(be_ref, b))
    def _():
        wg_s[...] = wg_ref[0].astype(BF16)
        wl_s[...] = wl_ref[0].astype(BF16)

    @pl.when(b < nu_ref[0])
    def _():
        x = x_ref[...]
        g = jnp.dot(x, wg_s[...], preferred_element_type=F32) + bg_ref[0]
        lin = jnp.dot(x, wl_s[...], preferred_element_type=F32) + bl_ref[0]
        g = jnp.minimum(g, SWIGLU_LIMIT)
        lin = jnp.clip(lin, -SWIGLU_LIMIT, SWIGLU_LIMIT)
        o_ref[...] = (g * jax.nn.sigmoid(SWIGLU_ALPHA * g) * (lin + 1.0)).astype(o_ref.dtype)

    @pl.when(b >= nu_ref[0])
    def _():
        o_ref[...] = jnp.zeros_like(o_ref)


def _moe_down_kernel(be_ref, nu_ref, a_ref, w_ref, bias_ref, o_ref, w_s):
    b = pl.program_id(1)

    @pl.when(_expert_changed(be_ref, b))
    def _():
        w_s[...] = w_ref[0].astype(BF16)

    @pl.when(b < nu_ref[0])
    def _():
        o_ref[...] = jnp.dot(a_ref[...], w_s[...], preferred_element_type=F32) + bias_ref[0]

    @pl.when(b >= nu_ref[0])
    def _():
        o_ref[...] = jnp.zeros_like(o_ref)


def moe_experts(xs, blk_e, n_used, w_gu, b_gu, w_down, b_down, tf=512, tn=512):
    R, D = xs.shape
    nb = R // MOE_TM
    E = w_gu.shape[0]
    nf = D_FF // tf
    act = pl.pallas_call(
        _moe_up_kernel,
        out_shape=jax.ShapeDtypeStruct((R, D_FF), BF16),
        grid_spec=pltpu.PrefetchScalarGridSpec(
            num_scalar_prefetch=2,
            grid=(nf, nb),
            in_specs=[pl.BlockSpec((MOE_TM, D), lambda f, b, be, nu: (b, 0)),
                      pl.BlockSpec((1, D, tf), lambda f, b, be, nu: (be[b], 0, f)),
                      pl.BlockSpec((1, D, tf), lambda f, b, be, nu: (be[b], 0, f + nf)),
                      pl.BlockSpec((1, 1, tf), lambda f, b, be, nu: (be[b], 0, f)),
                      pl.BlockSpec((1, 1, tf), lambda f, b, be, nu: (be[b], 0, f + nf))],
            out_specs=pl.BlockSpec((MOE_TM, tf), lambda f, b, be, nu: (b, f)),
            scratch_shapes=[pltpu.VMEM((D, tf), BF16), pltpu.VMEM((D, tf), BF16)]),
        compiler_params=_cparams(("arbitrary", "arbitrary")),
        name="moe_up",
    )(blk_e, n_used, xs, w_gu, w_gu, b_gu.reshape(E, 1, 2 * D_FF), b_gu.reshape(E, 1, 2 * D_FF))
    nn = D // tn
    return pl.pallas_call(
        _moe_down_kernel,
        out_shape=jax.ShapeDtypeStruct((R, D), F32),
        grid_spec=pltpu.PrefetchScalarGridSpec(
            num_scalar_prefetch=2,
            grid=(nn, nb),
            in_specs=[pl.BlockSpec((MOE_TM, D_FF), lambda j, b, be, nu: (b, 0)),
                      pl.BlockSpec((1, D_FF, tn), lambda j, b, be, nu: (be[b], 0, j)),
                      pl.BlockSpec((1, 1, tn), lambda j, b, be, nu: (be[b], 0, j))],
            out_specs=pl.BlockSpec((MOE_TM, tn), lambda j, b, be, nu: (b, j)),
            scratch_shapes=[pltpu.VMEM((D_FF, tn), BF16)]),
        compiler_params=_cparams(("arbitrary", "arbitrary")),
        name="moe_down",
    )(blk_e, n_used, act, w_down, b_down.reshape(E, 1, D))


def _combine_kernel(p_ref, x_ref, g_ref, ga_ref, *refs):
    o_ref = refs[-1]
    p = p_ref[0]
    f = refs[0][0] * p[:, 0:1]
    for k in range(1, TOP_K):
        f = f + refs[k][0] * p[:, k:k + 1]
    o_ref[0] = x_ref[0] + ga_ref[0] * _rms(f, g_ref[...])


def moe_combine(rows, probs, x, g_post, gate):
    B, S, D = x.shape
    ts = min(S, 256)
    row = pl.BlockSpec((1, ts, D), lambda b, s: (b, s, 0))
    return pl.pallas_call(
        _combine_kernel,
        out_shape=jax.ShapeDtypeStruct((B, S, D), F32),
        grid=(B, S // ts),
        in_specs=[pl.BlockSpec((1, ts, TOP_K), lambda b, s: (b, s, 0)), row,
                  pl.BlockSpec((1, D), lambda b, s: (0, 0)),
                  pl.BlockSpec((1, 1, D), lambda b, s: (b, 0, 0))] + [row] * TOP_K,
        out_specs=row,
        compiler_params=_cparams(("parallel", "parallel")),
        name="moe_combine",
    )(probs.reshape(B, S, TOP_K), x, g_post.reshape(1, D), gate.reshape(B, 1, D),
      *[r.reshape(B, S, D) for r in rows])


def moe_route(logits):
    T = logits.shape[0]
    top_v, top_e = lax.top_k(logits, TOP_K)
    probs = jax.nn.softmax(top_v, axis=-1)
    flat_e = top_e.reshape(-1)
    n_assign = flat_e.shape[0]
    order = jnp.argsort(flat_e)
    e_sorted = flat_e[order]
    counts = jnp.bincount(flat_e, length=N_EXPERTS)
    padded = (counts + MOE_TM - 1) // MOE_TM * MOE_TM
    pad_end = jnp.cumsum(padded)
    start = jnp.cumsum(counts) - counts
    dest = (pad_end - padded)[e_sorted] + jnp.arange(n_assign) - start[e_sorted]
    n_blocks = -(-n_assign // MOE_TM) + N_EXPERTS
    row_tok = jnp.zeros((n_blocks * MOE_TM,), jnp.int32).at[dest].set((order // TOP_K).astype(jnp.int32))
    blk_e = jnp.minimum(jnp.searchsorted(pad_end, jnp.arange(n_blocks) * MOE_TM, side='right'),
                        N_EXPERTS - 1).astype(jnp.int32)
    dest_of = jnp.zeros((n_assign,), jnp.int32).at[order].set(dest.astype(jnp.int32))
    n_used = (pad_end[-1] // MOE_TM).astype(jnp.int32).reshape(1)
    return probs, row_tok, blk_e, n_used, dest_of.reshape(T, TOP_K)


def _rel_bucket(dist):
    n = jnp.maximum(dist, 0)
    exact = NUM_BUCKETS // 2
    log_b = exact + (jnp.log(jnp.maximum(n, 1).astype(F32) / exact)
                     / math.log(REL_MAX_DIST / exact) * (NUM_BUCKETS - exact)).astype(jnp.int32)
    return jnp.where(n < exact, n, jnp.minimum(log_b, NUM_BUCKETS - 1))


TQ = Q_BLOCK
HPG = NSA_HEADS // NSA_KV
INT_MIN = -2 ** 31


def _flash_step(q2, k, v, bias, mask, scale, carry):
    m, l, acc = carry
    nk = k.shape[0]
    s = lax.dot_general(q2, k, (((1,), (1,)), ((), ())), preferred_element_type=F32)
    s = s.reshape(HPG, TQ, nk) * scale + bias
    s = jnp.where(mask[None], s, NEG)
    m_new = jnp.maximum(m, jnp.max(s, axis=-1, keepdims=True))
    p = jnp.where(mask[None], jnp.exp(s - m_new), 0.0)
    alpha = jnp.exp(m - m_new)
    l = alpha * l + jnp.sum(p, axis=-1, keepdims=True)
    pv = jnp.dot(p.reshape(HPG * TQ, nk).astype(BF16), v, preferred_element_type=F32)
    acc = alpha * acc + pv.reshape(HPG, TQ, v.shape[1])
    return m_new, l, acc


def _flash_init(dh):
    return (jnp.full((HPG, TQ, 1), NEG, F32), jnp.zeros((HPG, TQ, 1), F32), jnp.zeros((HPG, TQ, dh), F32))


def _flash_out(carry):
    m, l, acc = carry
    return acc / jnp.maximum(l, 1e-30)


def _causal_blocks(i, q2, k_ref, v_ref, mask_ref, tiles_ref, far_ref, scale):
    dh = k_ref.shape[-1]
    a = lax.broadcasted_iota(jnp.int32, (TQ, TQ), 0)
    b = lax.broadcasted_iota(jnp.int32, (TQ, TQ), 1)

    def blk(ref, jb):
        return ref[pl.ds(pl.multiple_of(jb * TQ, TQ), TQ), :]

    carry = _flash_step(q2, blk(k_ref, i), blk(v_ref, i), tiles_ref[:, 0],
                        jnp.logical_and(mask_ref[i] > 0.5, a >= b), scale, _flash_init(dh))
    jp = jnp.maximum(i - 1, 0)
    carry = _flash_step(q2, blk(k_ref, jp), blk(v_ref, jp), tiles_ref[:, 1],
                        jnp.logical_and(mask_ref[jp] > 0.5, i >= 1), scale, carry)
    far = far_ref[...][:, :, :1]
    n_far = jnp.maximum(i - 1, 0)

    def pair(jp2, c):
        rows = pl.ds(pl.multiple_of(jp2 * (2 * TQ), 2 * TQ), 2 * TQ)
        mask = jnp.concatenate([mask_ref[2 * jp2], mask_ref[2 * jp2 + 1]], axis=1) > 0.5
        return _flash_step(q2, k_ref[rows, :], v_ref[rows, :], far, mask, scale, c)

    def single(_, c):
        jb = n_far - 1
        return _flash_step(q2, blk(k_ref, jb), blk(v_ref, jb), far, mask_ref[jb] > 0.5, scale, c)

    carry = lax.fori_loop(0, n_far // 2, pair, carry)
    return lax.fori_loop(0, n_far % 2, single, carry)


def _nsa_prompt_kernel(q_ref, kc_ref, vc_ref, ks_ref, vs_ref, kw_ref, vw_ref, pool_ref, bc_ref, tiles_ref,
                       far_ref, gate_ref, o_ref, kcs, vcs, exp_s, mask_s):
    i = pl.program_id(2)
    S = ks_ref.shape[3]
    nb = S // CMP_BLOCK
    scale = NSA_DH ** -0.5

    @pl.when(i == 0)
    def _():
        kr = kc_ref[0, 0, 0].reshape(nb, CMP_BLOCK, NSA_DH) * pool_ref[0][None]
        vr = vc_ref[0, 0, 0].reshape(nb, CMP_BLOCK, NSA_DH) * pool_ref[1][None]
        kcs[...] = jnp.sum(kr, axis=1).astype(BF16)
        vcs[...] = jnp.sum(vr, axis=1).astype(BF16)
        n_i = lax.broadcasted_iota(jnp.int32, (nb, S), 0)
        s_i = lax.broadcasted_iota(jnp.int32, (nb, S), 1)
        exp_s[...] = jnp.where(s_i // CMP_BLOCK == n_i, 1.0, 0.0).astype(BF16)

    q2 = q_ref[0].reshape(HPG * TQ, NSA_DH)
    t0 = i * TQ

    s_c = lax.dot_general(q2, kcs[...], (((1,), (1,)), ((), ())), preferred_element_type=F32)
    s_c = s_c.reshape(HPG, TQ, nb) * scale + bc_ref[...]
    t_i = t0 + lax.broadcasted_iota(jnp.int32, (TQ, nb), 0)
    n_i = lax.broadcasted_iota(jnp.int32, (TQ, nb), 1)
    vis = t_i >= n_i * CMP_BLOCK + (CMP_BLOCK - 1)
    s_c = jnp.where(vis[None], s_c, NEG)
    e_c = jnp.where(vis[None], jnp.exp(s_c - jnp.max(s_c, axis=-1, keepdims=True)), 0.0)
    p_c = e_c / jnp.maximum(jnp.sum(e_c, axis=-1, keepdims=True), 1e-30)
    o_c = jnp.dot(p_c.reshape(HPG * TQ, nb).astype(BF16), vcs[...], preferred_element_type=F32)

    nt = (((1,), (1,)), ((), ()))
    eye = lambda n: jnp.where(lax.broadcasted_iota(jnp.int32, (n, n), 0) == lax.broadcasted_iota(jnp.int32, (n, n), 1),
                              1.0, 0.0)
    imp_t = lax.dot_general(eye(nb), jnp.sum(p_c, axis=0), nt, preferred_element_type=F32,
                            precision=lax.Precision.HIGHEST)
    n_t = lax.broadcasted_iota(jnp.int32, (nb, TQ), 0)
    t_t = t0 + lax.broadcasted_iota(jnp.int32, (nb, TQ), 1)
    cur = n_t == t_t // CMP_BLOCK
    work = jnp.where(jnp.logical_and(t_t >= n_t * CMP_BLOCK + (CMP_BLOCK - 1), jnp.logical_not(cur)), imp_t, -1.0)
    sel_t = jnp.where(cur, 1.0, 0.0)
    n_f = n_t.astype(F32)
    for _ in range(min(N_SEL - 1, nb)):
        mx = jnp.max(work, axis=0, keepdims=True)
        first = jnp.min(jnp.where(work == mx, n_f, float(nb)), axis=0, keepdims=True)
        pick = n_f == first
        sel_t = jnp.where(jnp.logical_and(pick, mx >= 0.0), 1.0, sel_t)
        work = jnp.where(pick, -2.0, work)
    sel = lax.dot_general(eye(TQ).astype(BF16), sel_t.astype(BF16), nt, preferred_element_type=F32)
    selx = jnp.dot(sel.astype(BF16), exp_s[...], preferred_element_type=F32)
    for jb in range(S // TQ):
        mask_s[jb] = selx[:, jb * TQ:(jb + 1) * TQ]

    o_s = _flash_out(_causal_blocks(i, q2, ks_ref.at[0, 0, 0], vs_ref.at[0, 0, 0], mask_s, tiles_ref, far_ref,
                                    scale))

    nw = WINDOW + TQ
    rows = pl.ds(pl.multiple_of(t0, TQ), nw)
    s_w = lax.dot_general(q2, kw_ref[0, 0, 0, rows, :], (((1,), (1,)), ((), ())), preferred_element_type=F32)
    far_w = jnp.broadcast_to(far_ref[...][:, :, :1], (HPG, TQ, nw - 2 * TQ))
    s_w = s_w.reshape(HPG, TQ, nw) * scale + jnp.concatenate([far_w, tiles_ref[:, 1], tiles_ref[:, 0]], axis=-1)
    c_w = lax.broadcasted_iota(jnp.int32, (TQ, nw), 1)
    d_w = lax.broadcasted_iota(jnp.int32, (TQ, nw), 0) + WINDOW - c_w
    in_w = jnp.logical_and(jnp.logical_and(d_w >= 0, d_w < WINDOW), t0 - WINDOW + c_w >= 0)
    s_w = jnp.where(in_w[None], s_w, NEG)
    e_w = jnp.where(in_w[None], jnp.exp(s_w - jnp.max(s_w, axis=-1, keepdims=True)), 0.0)
    o_w = jnp.dot(e_w.reshape(HPG * TQ, nw).astype(BF16), vw_ref[0, 0, 0, rows, :], preferred_element_type=F32)
    o_w = o_w.reshape(HPG, TQ, NSA_DH) / jnp.maximum(jnp.sum(e_w, axis=-1, keepdims=True), 1e-30)

    g = jax.nn.sigmoid(gate_ref[0])
    o = g[..., 0:1] * o_c.reshape(HPG, TQ, NSA_DH) + g[..., 1:2] * o_s + g[..., 2:3] * o_w
    for j in range(HPG):
        o_ref[0, :, j * NSA_DH:(j + 1) * NSA_DH] = o[j].astype(o_ref.dtype)


def _bias_tables(rel_bias, S):
    assert 2 * TQ - (TQ - 1) >= REL_MAX_DIST
    nb = S // CMP_BLOCK
    t = jnp.arange(S)[:, None]
    bc = rel_bias[_rel_bucket(t - (jnp.arange(nb)[None, :] * CMP_BLOCK + CMP_BLOCK - 1))]
    d = jnp.arange(2)[:, None, None] * TQ + jnp.arange(TQ)[None, :, None] - jnp.arange(TQ)[None, None, :]
    tiles = rel_bias[_rel_bucket(d)]
    far = jnp.broadcast_to(rel_bias[NUM_BUCKETS - 1][:, None, None], (rel_bias.shape[1], 1, TQ))
    return bc.transpose(2, 0, 1), tiles.transpose(3, 0, 1, 2), far


def nsa_prompt(q_n, cmp_t, sel_t, win_t, pool, g_n, bias_c, tiles, far):
    B, H, S, dh = q_n.shape
    G = NSA_KV
    nb = S // CMP_BLOCK
    assert win_t.shape[3] == S + WINDOW and WINDOW >= 2 * TQ
    kv = lambda c, n=S: pl.BlockSpec((1, 1, 1, n, dh), lambda b, g, i: (b, c, g, 0, 0))
    return pl.pallas_call(
        _nsa_prompt_kernel,
        out_shape=jax.ShapeDtypeStruct((B, S, H * dh), BF16),
        grid=(B, G, S // TQ),
        in_specs=[pl.BlockSpec((1, HPG, TQ, dh), lambda b, g, i: (b, g, i, 0)),
                  kv(0), kv(1), kv(0), kv(1), kv(0, S + WINDOW), kv(1, S + WINDOW),
                  pl.BlockSpec((2, CMP_BLOCK, dh), lambda b, g, i: (0, 0, 0)),
                  pl.BlockSpec((HPG, TQ, nb), lambda b, g, i: (g, i, 0)),
                  pl.BlockSpec((HPG, 2, TQ, TQ), lambda b, g, i: (g, 0, 0, 0)),
                  pl.BlockSpec((HPG, 1, TQ), lambda b, g, i: (g, 0, 0)),
                  pl.BlockSpec((1, HPG, TQ, 3), lambda b, g, i: (b, g, i, 0))],
        out_specs=pl.BlockSpec((1, TQ, HPG * dh), lambda b, g, i: (b, i, g)),
        scratch_shapes=[pltpu.VMEM((nb, dh), BF16), pltpu.VMEM((nb, dh), BF16), pltpu.VMEM((nb, S), BF16),
                        pltpu.VMEM((S // TQ, TQ, TQ), F32)],
        compiler_params=_cparams(("parallel", "parallel", "arbitrary")),
        name="nsa_prompt",
    )(q_n, cmp_t, cmp_t, sel_t, sel_t, win_t, win_t,
      jnp.broadcast_to(pool[:, :, None], (2, CMP_BLOCK, dh)), bias_c, tiles, far, g_n)


def _dsa_prompt_kernel(qi_ref, ki_ref, wi_ref, qd_ref, kd_ref, vd_ref, tiles_ref, far_ref, o_ref,
                       key_s, mask_s, *, topk):
    i = pl.program_id(1)
    S = ki_ref.shape[1]
    nkb = S // TQ
    t0 = i * TQ
    a = lax.broadcasted_iota(jnp.int32, (TQ, TQ), 0)
    b = lax.broadcasted_iota(jnp.int32, (TQ, TQ), 1)

    w = wi_ref[0] * (IDX_DH ** -0.5 * IDX_HEADS ** -0.5)
    w_cols = [jnp.broadcast_to(w[:, h:h + 1], (TQ, TQ)) for h in range(IDX_HEADS)]
    for jb in range(nkb):
        @pl.when(jb <= i)
        def _():
            kk = ki_ref[0, jb * TQ:(jb + 1) * TQ, :]
            acc = jnp.zeros((TQ, TQ), F32)
            for h in range(IDX_HEADS):
                s = lax.dot_general(qi_ref[0, h], kk, (((1,), (1,)), ((), ())), preferred_element_type=F32)
                acc = acc + jnp.maximum(s, 0.0) * w_cols[h]
            score = jnp.where(jb * TQ + b <= t0 + a, acc, NEG) + 0.0
            bits = pltpu.bitcast(score, jnp.int32)
            key_s[jb] = jnp.where(bits < 0, bits ^ 0x7FFFFFFF, bits)

        @pl.when(jb > i)
        def _():
            bits = pltpu.bitcast(jnp.full((TQ, TQ), NEG, F32), jnp.int32)
            key_s[jb] = bits ^ 0x7FFFFFFF

    half = nkb // 2
    if half * TQ >= topk:
        thr, last = lax.cond(i < half, lambda: _kth_threshold(key_s[:half], topk),
                             lambda: _kth_threshold(key_s[...], topk))
    else:
        thr, last = _kth_threshold(key_s[...], topk)
    keys = key_s[...]
    pos = (lax.broadcasted_iota(jnp.int32, (nkb, TQ, TQ), 0) * TQ
           + lax.broadcasted_iota(jnp.int32, (nkb, TQ, TQ), 2))
    chosen = jnp.logical_or(keys > thr, jnp.logical_and(keys == thr, pos <= last))
    mask_s[...] = jnp.where(chosen, 1.0, 0.0)

    scale = DSA_DH ** -0.5
    for g in range(DSA_KV):
        q2 = qd_ref[0, g * HPG:(g + 1) * HPG].reshape(HPG * TQ, DSA_DH)
        o = _flash_out(_causal_blocks(i, q2, kd_ref.at[0, 0, g], vd_ref.at[0, 0, g], mask_s,
                                      tiles_ref.at[g * HPG:(g + 1) * HPG], far_ref.at[g * HPG:(g + 1) * HPG],
                                      scale))
        for j in range(HPG):
            h = g * HPG + j
            o_ref[0, :, h * DSA_DH:(h + 1) * DSA_DH] = o[j].astype(o_ref.dtype)


def dsa_prompt(q_i, k_i, w_i, q_d, dkv_t, tiles, far):
    B, H, S, dh = q_d.shape
    G = DSA_KV
    topk = min(DSA_TOPK_MAX, S // 4)
    heads = pl.BlockSpec((1, H, TQ, dh), lambda b, i: (b, 0, i, 0))
    kv = lambda c: pl.BlockSpec((1, 1, G, S, dh), lambda b, i: (b, c, 0, 0, 0))
    return pl.pallas_call(
        functools.partial(_dsa_prompt_kernel, topk=topk),
        out_shape=jax.ShapeDtypeStruct((B, S, H * dh), BF16),
        grid=(B, S // TQ),
        in_specs=[heads,
                  pl.BlockSpec((1, S, k_i.shape[2]), lambda b, i: (b, 0, 0)),
                  pl.BlockSpec((1, TQ, H), lambda b, i: (b, i, 0)),
                  heads, kv(0), kv(1),
                  pl.BlockSpec((H, 2, TQ, TQ), lambda b, i: (0, 0, 0, 0)),
                  pl.BlockSpec((H, 1, TQ), lambda b, i: (0, 0, 0))],
        out_specs=pl.BlockSpec((1, TQ, H * dh), lambda b, i: (b, i, 0)),
        scratch_shapes=[pltpu.VMEM((S // TQ, TQ, TQ), jnp.int32), pltpu.VMEM((S // TQ, TQ, TQ), F32)],
        compiler_params=_cparams(("parallel", "arbitrary")),
        name="dsa_prompt",
    )(q_i, k_i, w_i, q_d, dkv_t, dkv_t, tiles, far)


def mixer_prompt(rel_bias, pool, q_n, cmp, sel, win, g_n, q_d, dkv, q_i, k_i, w_i):
    B, S = q_n.shape[:2]
    heads_first = lambda a: a.transpose(0, 2, 1, 3)
    kv_t = lambda a, dt: a.transpose(0, 2, 3, 1, 4).astype(dt)
    bias_c, tiles, far = _bias_tables(rel_bias, S)
    win_t = jnp.pad(kv_t(win, BF16), ((0, 0), (0, 0), (0, 0), (WINDOW, 0), (0, 0)))
    o_n = nsa_prompt(heads_first(q_n).astype(BF16), kv_t(cmp, F32), kv_t(sel, BF16), win_t, pool,
                     heads_first(g_n), bias_c[:NSA_HEADS], tiles[:NSA_HEADS], far[:NSA_HEADS])
    o_d = dsa_prompt(heads_first(q_i).astype(BF16), k_i.astype(BF16), w_i, heads_first(q_d).astype(BF16),
                     kv_t(dkv, BF16), tiles[NSA_HEADS:], far[NSA_HEADS:])
    return o_n, o_d


KVH = NSA_KV * NSA_DH
PAGES_PER_STEP = 8


def _pad_groups(q, n_heads):
    R, dh = q.shape
    G = n_heads // HPG
    col_g = lax.broadcasted_iota(jnp.int32, (R, G * dh), 1) // dh
    row_g = (lax.broadcasted_iota(jnp.int32, (R, G * dh), 0) % n_heads) // HPG
    return jnp.where(col_g == row_g, jnp.concatenate([q] * G, axis=1), 0.0).astype(BF16)


def _own_group(o_pad, n_heads):
    R = o_pad.shape[0]
    G = n_heads // HPG
    dh = o_pad.shape[1] // G
    row_g = (lax.broadcasted_iota(jnp.int32, (R, dh), 0) % n_heads) // HPG
    out = o_pad[:, :dh]
    for g in range(1, G):
        out = jnp.where(row_g == g, o_pad[:, g * dh:(g + 1) * dh], out)
    return out


def _order_key(x):
    bits = pltpu.bitcast(x, jnp.int32)
    return jnp.where(bits < 0, bits ^ 0x7FFFFFFF, bits)


def _kth_threshold(keys, topk):
    n, R, L = keys.shape

    def count(pred):
        c = jnp.sum(jnp.where(pred, 1.0, 0.0), axis=0)
        return jnp.sum(c, axis=-1, keepdims=True)[None]

    def bit_body(i, pat):
        cand = pat | lax.shift_left(jnp.int32(1), 31 - i)
        return jnp.where(count(keys >= (cand ^ INT_MIN)) >= topk, cand, pat)

    thr = lax.fori_loop(0, 32, bit_body, jnp.zeros((1, R, 1), jnp.int32)) ^ INT_MIN
    need = topk - count(keys > thr)
    pos = lax.broadcasted_iota(jnp.int32, (n, R, L), 0) * L + lax.broadcasted_iota(jnp.int32, (n, R, L), 2)
    tie = keys == thr
    nbits = (n * L - 1).bit_length()

    def tie_body(i, last):
        cand = last | lax.shift_left(jnp.int32(1), nbits - 1 - i)
        return jnp.where(count(jnp.logical_and(tie, pos < cand)) < need, cand, last)

    crowded = jnp.max(jnp.where(count(tie) > need, 1.0, 0.0)) > 0.5
    last = lax.cond(crowded,
                    lambda: lax.fori_loop(0, nbits, tie_body, jnp.zeros((1, R, 1), jnp.int32)),
                    lambda: jnp.full((1, R, 1), n * L, jnp.int32))
    return thr, last


def _cmp_pages_kernel(pt_ref, pw_ref, *refs):
    o_ref = refs[-1]
    Wd, ncol = o_ref.shape[2], o_ref.shape[3]
    bpp = PAGE_SIZE // CMP_BLOCK
    col_id = lax.broadcasted_iota(jnp.int32, (Wd, ncol), 1)
    blk_of_row = lax.broadcasted_iota(jnp.int32, (Wd, PAGE_SIZE), 1) // CMP_BLOCK
    out = jnp.zeros((Wd, ncol), F32)
    for k, page in enumerate(refs[:-1]):
        y = page[0] * pw_ref[...]
        for j in range(bpp):
            summary = jnp.sum(jnp.where(blk_of_row == j, y, 0.0), axis=-1, keepdims=True)
            out = jnp.where(col_id == k * bpp + j, summary, out)
    o_ref[0, 0] = out


def cmp_pages(pt_flat, cache_t, pool, B):
    n_pages = pt_flat.shape[0] // B
    pgs = min(PAGES_PER_STEP, n_pages)
    bpp = PAGE_SIZE // CMP_BLOCK
    Wd = cache_t.shape[1]
    pw = jnp.repeat(pool[:, jnp.arange(PAGE_SIZE) % CMP_BLOCK], Wd // 2, axis=0)
    page = lambda k: pl.BlockSpec((1, Wd, PAGE_SIZE), lambda b, s, pt: (pt[b * n_pages + s * pgs + k], 0, 0))
    out = pl.pallas_call(
        _cmp_pages_kernel,
        out_shape=jax.ShapeDtypeStruct((B, n_pages // pgs, Wd, pgs * bpp), F32),
        grid_spec=pltpu.PrefetchScalarGridSpec(
            num_scalar_prefetch=1, grid=(B, n_pages // pgs),
            in_specs=[pl.BlockSpec((Wd, PAGE_SIZE), lambda b, s, pt: (0, 0))]
                     + [page(k) for k in range(pgs)],
            out_specs=pl.BlockSpec((1, 1, Wd, pgs * bpp), lambda b, s, pt: (b, s, 0, 0))),
        compiler_params=_cparams(("parallel", "parallel")),
        name="cmp_pages",
    )(pt_flat, pw, *([cache_t] * pgs))
    return out.transpose(0, 2, 1, 3).reshape(B, Wd, n_pages * bpp)


def _nsa_sample_pre_kernel(q_ref, kvc_ref, wb_ref, new_ref, bc_ref, bw_ref, bn_ref, oc_ref, ow_ref, sel_ref,
                           *, past, n_new):
    H = NSA_HEADS
    R = q_ref.shape[1]
    nb = kvc_ref.shape[2]
    W = wb_ref.shape[2]
    scale = NSA_DH ** -0.5
    nt = (((1,), (1,)), ((), ()))
    qp = _pad_groups(q_ref[0], H)

    kvc = kvc_ref[0]
    s_c = jnp.dot(qp, kvc[:KVH].astype(BF16), preferred_element_type=F32) * scale + bc_ref[...]
    q_pos = past + lax.broadcasted_iota(jnp.int32, (R, nb), 0) // H
    n_i = lax.broadcasted_iota(jnp.int32, (R, nb), 1)
    vis = q_pos >= n_i * CMP_BLOCK + (CMP_BLOCK - 1)
    s_c = jnp.where(vis, s_c, NEG)
    e_c = jnp.where(vis, jnp.exp(s_c - jnp.max(s_c, axis=-1, keepdims=True)), 0.0)
    p_c = e_c / jnp.maximum(jnp.sum(e_c, axis=-1, keepdims=True), 1e-30)
    oc_ref[0] = _own_group(lax.dot_general(p_c.astype(BF16), kvc[KVH:].astype(BF16), nt,
                                           preferred_element_type=F32), H)

    same = (lax.broadcasted_iota(jnp.int32, (R, R), 0) // HPG == lax.broadcasted_iota(jnp.int32, (R, R), 1) // HPG)
    imp = jnp.dot(jnp.where(same, 1.0, 0.0), p_c, preferred_element_type=F32, precision=lax.Precision.HIGHEST)
    cur = n_i == q_pos // CMP_BLOCK
    work = jnp.where(jnp.logical_and(vis, jnp.logical_not(cur)), imp, -1.0)
    sel = jnp.where(cur, 1.0, 0.0)
    n_f = n_i.astype(F32)
    for _ in range(min(N_SEL - 1, nb)):
        mx = jnp.max(work, axis=-1, keepdims=True)
        first = jnp.min(jnp.where(work == mx, n_f, float(nb)), axis=-1, keepdims=True)
        pick = n_f == first
        sel = jnp.where(jnp.logical_and(pick, mx >= 0.0), 1.0, sel)
        work = jnp.where(pick, -2.0, work)
    sel_ref[0] = sel.astype(sel_ref.dtype)

    wb, new = wb_ref[0], new_ref[0]
    s1 = jnp.dot(qp, wb[:KVH].astype(BF16), preferred_element_type=F32) * scale + bw_ref[...]
    i1 = lax.broadcasted_iota(jnp.int32, (R, W), 1)
    d1 = W + lax.broadcasted_iota(jnp.int32, (R, W), 0) // H - i1
    m1 = jnp.logical_and(jnp.logical_and(d1 >= 0, d1 < WINDOW), past - W + i1 >= 0)
    L = new.shape[0]
    s2 = lax.dot_general(qp, new[:, :KVH].astype(BF16), nt, preferred_element_type=F32) * scale + bn_ref[...]
    j2 = lax.broadcasted_iota(jnp.int32, (R, L), 1)
    d2 = lax.broadcasted_iota(jnp.int32, (R, L), 0) // H - j2
    m2 = jnp.logical_and(jnp.logical_and(d2 >= 0, d2 < WINDOW), j2 < n_new)
    s1 = jnp.where(m1, s1, NEG)
    s2 = jnp.where(m2, s2, NEG)
    mx = jnp.maximum(jnp.max(s1, axis=-1, keepdims=True), jnp.max(s2, axis=-1, keepdims=True))
    e1 = jnp.where(m1, jnp.exp(s1 - mx), 0.0)
    e2 = jnp.where(m2, jnp.exp(s2 - mx), 0.0)
    den = jnp.sum(e1, axis=-1, keepdims=True) + jnp.sum(e2, axis=-1, keepdims=True)
    o_w = (lax.dot_general(e1.astype(BF16), wb[KVH:].astype(BF16), nt, preferred_element_type=F32)
           + jnp.dot(e2.astype(BF16), new[:, KVH:].astype(BF16), preferred_element_type=F32))
    ow_ref[0] = _own_group(o_w / jnp.maximum(den, 1e-30), H)


def nsa_sample_pre(q, kvc, wbuf, new_win, bias_c, bias_w, bias_new, past, n_new):
    B, R, dh = q.shape
    nb, W, L = kvc.shape[2], wbuf.shape[2], new_win.shape[1]
    per_b = lambda *s: pl.BlockSpec((1,) + s, lambda b: (b,) + (0,) * len(s))
    const = lambda *s: pl.BlockSpec(s, lambda b: (0,) * len(s))
    return pl.pallas_call(
        functools.partial(_nsa_sample_pre_kernel, past=past, n_new=n_new),
        out_shape=(jax.ShapeDtypeStruct((B, R, dh), F32), jax.ShapeDtypeStruct((B, R, dh), F32),
                   jax.ShapeDtypeStruct((B, R, nb), BF16)),
        grid=(B,),
        in_specs=[per_b(R, dh), per_b(2 * KVH, nb), per_b(2 * KVH, W), per_b(L, 2 * KVH),
                  const(R, nb), const(R, W), const(R, L)],
        out_specs=(per_b(R, dh), per_b(R, dh), per_b(R, nb)),
        compiler_params=_cparams(("parallel",)),
        name="nsa_sample_pre",
    )(q, kvc, wbuf, new_win, bias_c, bias_w, bias_new)


def _index_scores(qi, w, keys, n_heads, feature_major):
    contract = (((1,), (0,)), ((), ())) if feature_major else (((1,), (1,)), ((), ()))
    s = lax.dot_general(qi, keys.astype(BF16), contract, preferred_element_type=F32)
    v = jnp.maximum(s * IDX_DH ** -0.5, 0.0) * w
    T = qi.shape[0] // n_heads
    return jnp.sum(v.reshape(T, n_heads, s.shape[1]), axis=1) * n_heads ** -0.5 + 0.0


def _idx_scores_kernel(pt_ref, q_ref, w_ref, *refs):
    o_ref = refs[-1]
    qi = q_ref[0].astype(BF16)
    for k, page in enumerate(refs[:-1]):
        o_ref[0, k] = _index_scores(qi, w_ref[0], page[0], IDX_HEADS, True)


def idx_scores(pt_flat, q_i, w_i, cache_t):
    B, R, dh = q_i.shape
    T = R // IDX_HEADS
    n_pages = pt_flat.shape[0] // B
    pgs = min(2 * PAGES_PER_STEP, n_pages)
    page = lambda k: pl.BlockSpec((1, dh, PAGE_SIZE), lambda b, s, pt: (pt[b * n_pages + s * pgs + k], 0, 0))
    return pl.pallas_call(
        _idx_scores_kernel,
        out_shape=jax.ShapeDtypeStruct((B, n_pages, T, PAGE_SIZE), F32),
        grid_spec=pltpu.PrefetchScalarGridSpec(
            num_scalar_prefetch=1, grid=(B, n_pages // pgs),
            in_specs=[pl.BlockSpec((1, R, dh), lambda b, s, pt: (b, 0, 0)),
                      pl.BlockSpec((1, R, 1), lambda b, s, pt: (b, 0, 0))] + [page(k) for k in range(pgs)],
            out_specs=pl.BlockSpec((1, pgs, T, PAGE_SIZE), lambda b, s, pt: (b, s, 0, 0))),
        compiler_params=_cparams(("parallel", "parallel")),
        name="idx_scores",
    )(pt_flat, q_i, w_i, *([cache_t] * pgs))


def _dsa_sample_pre_kernel(sp_ref, q_ref, w_ref, kn_ref, sn_ref, thr_ref, last_ref, *, topk, n_new):
    T, L = sp_ref.shape[2], sp_ref.shape[3]
    sc = _index_scores(q_ref[0].astype(BF16), w_ref[0], kn_ref[0], IDX_HEADS, False)
    j = lax.broadcasted_iota(jnp.int32, (T, L), 1)
    t = lax.broadcasted_iota(jnp.int32, (T, L), 0)
    sc = jnp.where(jnp.logical_and(j <= t, j < n_new), sc, NEG)
    sn_ref[0, 0] = sc
    keys = jnp.concatenate([_order_key(sp_ref[0]), _order_key(sc)[None]], axis=0)
    thr, last = _kth_threshold(keys, topk)
    thr_ref[0] = jnp.broadcast_to(thr[0], (T, L))
    last_ref[0] = jnp.broadcast_to(last[0], (T, L))


def dsa_sample_pre(scores_past, q_i, w_i, new_ki, topk, n_new):
    B, n_pages, T, L = scores_past.shape
    R, dh = q_i.shape[1:]
    per_b = lambda *s: pl.BlockSpec((1,) + s, lambda b: (b,) + (0,) * len(s))
    return pl.pallas_call(
        functools.partial(_dsa_sample_pre_kernel, topk=topk, n_new=n_new),
        out_shape=(jax.ShapeDtypeStruct((B, 1, T, L), F32), jax.ShapeDtypeStruct((B, T, L), jnp.int32),
                   jax.ShapeDtypeStruct((B, T, L), jnp.int32)),
        grid=(B,),
        in_specs=[per_b(n_pages, T, L), per_b(R, dh), per_b(R, 1), per_b(L, dh)],
        out_specs=(per_b(1, T, L), per_b(T, L), per_b(T, L)),
        compiler_params=_cparams(("parallel",)),
        name="dsa_sample_pre",
    )(scores_past, q_i, w_i, new_ki)


def _paged_attn_kernel(pt_ref, *refs, mode, pgs, n_pages, n_heads, n_new, dh):
    if mode == "blocks":
        q_ref, new_ref, far_ref, tl_ref, tn_ref, sel_ref, oc_ref, ow_ref, gate_ref = refs[:9]
        rest = refs[9:]
    else:
        q_ref, new_ref, far_ref, tl_ref, tn_ref, sp_ref, sn_ref, thr_ref, last_ref = refs[:9]
        rest = refs[9:]
    pages, o_ref = rest[:pgs], rest[pgs]
    qp_s, m_s, l_s, acc_s = rest[pgs + 1:]
    step, n_steps = pl.program_id(1), pl.num_programs(1)
    R = q_ref.shape[1]
    T = R // n_heads
    L = PAGE_SIZE
    scale = dh ** -0.5

    @pl.when(step == 0)
    def _():
        qp_s[...] = _pad_groups(q_ref[0], n_heads)
        m_s[...] = jnp.full(m_s.shape, NEG, F32)
        l_s[...] = jnp.zeros(l_s.shape, F32)
        acc_s[...] = jnp.zeros(acc_s.shape, F32)

    def update(kv, feature_major, bias, mask):
        nt = (((1,), (1,)), ((), ()))
        if feature_major:
            s = jnp.dot(qp_s[...], kv[:KVH].astype(BF16), preferred_element_type=F32)
        else:
            s = lax.dot_general(qp_s[...], kv[:, :KVH].astype(BF16), nt, preferred_element_type=F32)
        s = jnp.where(mask, s * scale + bias, NEG)
        m_old = m_s[...]
        m_new = jnp.maximum(m_old, jnp.max(s, axis=-1, keepdims=True))
        p = jnp.where(mask, jnp.exp(s - m_new), 0.0)
        alpha = jnp.exp(m_old - m_new)
        l_s[...] = alpha * l_s[...] + jnp.sum(p, axis=-1, keepdims=True)
        if feature_major:
            pv = lax.dot_general(p.astype(BF16), kv[KVH:].astype(BF16), nt, preferred_element_type=F32)
        else:
            pv = jnp.dot(p.astype(BF16), kv[:, KVH:].astype(BF16), preferred_element_type=F32)
        acc_s[...] = alpha * acc_s[...] + pv
        m_s[...] = m_new

    def rows_of_tokens(x):
        return jnp.broadcast_to(x[:, None, :], (T, n_heads, L)).reshape(R, L)

    def chosen(scores, page):
        key = _order_key(scores)
        pos = page * L + lax.broadcasted_iota(jnp.int32, (T, L), 1)
        thr = thr_ref[0]
        ch = jnp.logical_or(key > thr, jnp.logical_and(key == thr, pos <= last_ref[0]))
        return rows_of_tokens(jnp.where(ch, 1.0, 0.0)) > 0.5

    qp = qp_s[...]
    scores, masks = [], []
    for k in range(pgs):
        page = step * pgs + k
        bias = jnp.where(page == n_pages - 1, tl_ref[...], far_ref[...])
        if mode == "blocks":
            nb = sel_ref.shape[2]
            blk = page * (L // CMP_BLOCK) + lax.broadcasted_iota(jnp.int32, (nb, L), 1) // CMP_BLOCK
            expand = jnp.where(lax.broadcasted_iota(jnp.int32, (nb, L), 0) == blk, 1.0, 0.0).astype(BF16)
            mask = jnp.dot(sel_ref[0], expand, preferred_element_type=F32) > 0.5
        else:
            mask = chosen(sp_ref[0, k], page)
        s = jnp.dot(qp, pages[k][0, :KVH].astype(BF16), preferred_element_type=F32)
        scores.append(jnp.where(mask, s * scale + bias, NEG))
        masks.append(mask)
    m_old = m_s[...]
    m_new = m_old
    for s in scores:
        m_new = jnp.maximum(m_new, jnp.max(s, axis=-1, keepdims=True))
    alpha = jnp.exp(m_old - m_new)
    l_new = alpha * l_s[...]
    acc = alpha * acc_s[...]
    for k in range(pgs):
        p = jnp.where(masks[k], jnp.exp(scores[k] - m_new), 0.0)
        l_new = l_new + jnp.sum(p, axis=-1, keepdims=True)
        acc = acc + lax.dot_general(p.astype(BF16), pages[k][0, KVH:].astype(BF16), (((1,), (1,)), ((), ())),
                                    preferred_element_type=F32)
    m_s[...] = m_new
    l_s[...] = l_new
    acc_s[...] = acc

    @pl.when(step == n_steps - 1)
    def _():
        j = lax.broadcasted_iota(jnp.int32, (R, L), 1)
        t = lax.broadcasted_iota(jnp.int32, (R, L), 0) // n_heads
        mask = jnp.logical_and(j <= t, j < n_new)
        if mode != "blocks":
            mask = jnp.logical_and(mask, chosen(sn_ref[0, 0], n_pages))
        update(new_ref[0], False, tn_ref[...], mask)
        o = _own_group(acc_s[...] / jnp.maximum(l_s[...], 1e-30), n_heads)
        if mode == "blocks":
            g = jax.nn.sigmoid(gate_ref[0])
            o = g[:, 0:1] * oc_ref[0] + g[:, 1:2] * o + g[:, 2:3] * ow_ref[0]
        o_ref[0] = o.astype(o_ref.dtype)


def paged_attn(mode, pt_flat, cache_t, q, new_rows, far, tile_last, tile_new, extra, n_heads, n_new):
    B, R, dh = q.shape
    n_pages = pt_flat.shape[0] // B
    pgs = min(PAGES_PER_STEP, n_pages)
    L = PAGE_SIZE
    Wd = cache_t.shape[1]
    per_b = lambda *s: pl.BlockSpec((1,) + s, lambda b, st, pt: (b,) + (0,) * len(s))
    const = lambda *s: pl.BlockSpec(s, lambda b, st, pt: (0,) * len(s))
    page = lambda k: pl.BlockSpec((1, Wd, L), lambda b, st, pt: (pt[b * n_pages + st * pgs + k], 0, 0))
    if mode == "blocks":
        sel = extra[0]
        extra_specs = [per_b(R, sel.shape[2]), per_b(R, dh), per_b(R, dh), per_b(R, 3)]
    else:
        T = R // n_heads
        extra_specs = [pl.BlockSpec((1, pgs, T, L), lambda b, st, pt: (b, st, 0, 0)), per_b(1, T, L),
                       per_b(T, L), per_b(T, L)]
    return pl.pallas_call(
        functools.partial(_paged_attn_kernel, mode=mode, pgs=pgs, n_pages=n_pages, n_heads=n_heads, n_new=n_new,
                          dh=dh),
        out_shape=jax.ShapeDtypeStruct((B, R, dh), BF16),
        grid_spec=pltpu.PrefetchScalarGridSpec(
            num_scalar_prefetch=1, grid=(B, n_pages // pgs),
            in_specs=[per_b(R, dh), per_b(L, Wd), const(R, 1), const(R, L), const(R, L)] + extra_specs
                     + [page(k) for k in range(pgs)],
            out_specs=per_b(R, dh),
            scratch_shapes=[pltpu.VMEM((R, KVH), BF16), pltpu.VMEM((R, 1), F32), pltpu.VMEM((R, 1), F32),
                            pltpu.VMEM((R, KVH), F32)]),
        compiler_params=_cparams(("parallel", "arbitrary")),
        name="paged_attn_" + mode,
    )(pt_flat, q, new_rows, far, tile_last, tile_new, *extra, *([cache_t] * pgs))


def _sample_bias(rel_bias, head0, n_heads, T, past, W, nb):
    assert PAGE_SIZE + 1 >= REL_MAX_DIST
    R = T * n_heads
    t = (jnp.arange(R) // n_heads)[:, None]
    h = head0 + (jnp.arange(R) % n_heads)[:, None]
    tab = lambda dist: rel_bias[_rel_bucket(dist), h]
    lane = jnp.arange(PAGE_SIZE)[None, :]
    return dict(far=rel_bias[NUM_BUCKETS - 1][h], last=tab(PAGE_SIZE + t - lane), new=tab(t - lane),
                cmp=tab(past + t - (jnp.arange(nb)[None, :] * CMP_BLOCK + CMP_BLOCK - 1)),
                win=tab(W + t - jnp.arange(W)[None, :]))


def mixer_sample(cache_cmp, cache_sel, cache_win, cache_dkv, cache_idx, page_table, rel_bias, pool,
                 q_n, cmp, sel, win, g_n, q_d, dkv, q_i, k_i, w_i):
    B, T = q_n.shape[:2]
    n_pages = page_table.shape[1]
    past = n_pages * PAGE_SIZE
    W = cache_win.shape[1]
    assert T < CMP_BLOCK and past % CMP_BLOCK == 0 and T <= PAGE_SIZE
    topk = min(DSA_TOPK_MAX, (past + T) // 4)
    n_phys = cache_sel.shape[0]
    nb = past // CMP_BLOCK
    pt_flat = page_table.reshape(-1)
    flat = lambda c: jnp.moveaxis(c, 1, -1).reshape(c.shape[0], -1, c.shape[1])
    rows = lambda a: a.reshape(B, T * a.shape[2], -1)
    pad_new = lambda a: jnp.pad(a.reshape(B, T, -1), ((0, 0), (0, PAGE_SIZE - T), (0, 0)))
    bn = _sample_bias(rel_bias, 0, NSA_HEADS, T, past, W, nb)
    bd = _sample_bias(rel_bias, NSA_HEADS, DSA_HEADS, T, past, W, nb)

    kvc = cmp_pages(pt_flat, flat(cache_cmp), pool, B)
    o_c, o_w, selm = nsa_sample_pre(rows(q_n), kvc, flat(cache_win), pad_new(win), bn["cmp"], bn["win"], bn["new"],
                                    past, T)
    o_n = paged_attn("blocks", pt_flat, flat(cache_sel), rows(q_n), pad_new(sel), bn["far"], bn["last"], bn["new"],
                     (selm, o_c, o_w, g_n.reshape(B, T * NSA_HEADS, 3)), NSA_HEADS, T)

    qi, wi = rows(q_i), w_i.reshape(B, T * IDX_HEADS, 1)
    sp = idx_scores(pt_flat, qi, wi, flat(cache_idx))
    sn, thr, last = dsa_sample_pre(sp, qi, wi, pad_new(k_i), topk, T)
    o_d = paged_attn("thr", pt_flat, flat(cache_dkv), rows(q_d), pad_new(dkv), bd["far"], bd["last"], bd["new"],
                     (sp, sn, thr, last), DSA_HEADS, T)
    new_win = jnp.concatenate([cache_win[:, T:], win], axis=1)
    return o_n.reshape(B, T, -1), o_d.reshape(B, T, -1), new_win


def _project(h, w_q, w_kv, w_s, w_g):
    B, S, D = h.shape
    h2 = h.reshape(B * S, D)
    zq = matmul(h2, w_q, out_dtype=F32)
    zkv = matmul(h2, w_kv, out_dtype=F32)
    zs = matmul(h2, w_s, out_dtype=F32)
    zg = matmul(h2, w_g, out_dtype=F32)
    kvn = (B, S, 2, NSA_KV, NSA_DH)
    q_n = zq[:, :NSA_WIDTH].reshape(B, S, NSA_HEADS, NSA_DH)
    q_d = zq[:, NSA_WIDTH:NSA_WIDTH + DSA_WIDTH].reshape(B, S, DSA_HEADS, DSA_DH)
    q_i = zq[:, NSA_WIDTH + DSA_WIDTH:].reshape(B, S, IDX_HEADS, IDX_DH)
    cmp = zkv[:, :KV_W].reshape(kvn)
    sel = zkv[:, KV_W:2 * KV_W].reshape(kvn)
    win = zkv[:, 2 * KV_W:3 * KV_W].reshape(kvn)
    dkv = zkv[:, 3 * KV_W:].reshape(B, S, 2, DSA_KV, DSA_DH)
    k_i = zs[:, :IDX_DH].reshape(B, S, IDX_DH)
    w_i = zs[:, IDX_DH:IDX_DH + IDX_HEADS].reshape(B, S, IDX_HEADS)
    g_n = zs[:, IDX_DH + IDX_HEADS:].reshape(B, S, NSA_HEADS, 3)
    return (q_n, cmp, sel, win, g_n, q_d, dkv, q_i, k_i, w_i), zg


def kernel(x_prompt, x_sample, c_prompt, c_sample, cache_nsa_cmp, cache_nsa_sel, cache_nsa_win, cache_dsa_kv,
           cache_dsa_idx, page_table, rel_bias, w_mod, b_mod, g_pre_mix, g_post_mix, g_pre_ffn, g_post_ffn,
           w_in, cmp_pool, w_up_nsa, w_up_dsa, w_out, w_router, b_router, w_gu, b_gu, w_down, b_down):
    l = 0
    D = D_MODEL
    Bp, Sp, _ = x_prompt.shape
    Bs, Ss, _ = x_sample.shape

    c = jnp.concatenate([c_prompt, c_sample], 0)
    n_c = c.shape[0]
    c_pad = jnp.pad(jax.nn.silu(c), ((0, -n_c % 16), (0, 0))).astype(BF16)
    mod = matmul(c_pad, w_mod[l], b_mod[l], tn=1536)[:n_c]
    mod_p, mod_s = mod[:Bp], mod[Bp:]

    wi = w_in[l]
    seg = lambda k: wi[:, _OFF[k]:_OFF[k + 1]]
    w_q = jnp.concatenate([seg(0), seg(5), seg(7)], 1).astype(BF16)
    w_kv = jnp.concatenate([seg(1), seg(2), seg(3), seg(6)], 1).astype(BF16)
    w_s = jnp.concatenate([seg(8), seg(9), seg(4)], 1).astype(BF16)
    w_g = seg(10).astype(BF16)
    wun, wud, wo = w_up_nsa[l].astype(BF16), w_up_dsa[l].astype(BF16), w_out[l].astype(BF16)

    def mix_front(x, mod_g):
        sh1, sc1 = mod_g[:, :D], mod_g[:, D:2 * D]
        h = norm_mod(x, g_pre_mix[l], sc1, sh1)
        return _project(h, w_q, w_kv, w_s, w_g)

    def mix_back(x, mod_g, o_n, o_d, zg):
        B, S, _ = x.shape
        ga1, sh2, sc2 = mod_g[:, 2 * D:3 * D], mod_g[:, 3 * D:4 * D], mod_g[:, 4 * D:5 * D]
        u = merge_up(o_n.reshape(B * S, -1).astype(BF16), o_d.reshape(B * S, -1).astype(BF16), zg, wun, wud)
        return out_proj(u, x, wo, g_post_mix[l], g_pre_ffn[l], ga1, sc2, sh2, w_router[l], b_router[l])

    parts_p, zg_p = mix_front(x_prompt, mod_p)
    o_n, o_d = mixer_prompt(rel_bias, cmp_pool[l], *parts_p)
    x1_p, h_p, lg_p = mix_back(x_prompt, mod_p, o_n, o_d, zg_p)

    parts_s, zg_s = mix_front(x_sample, mod_s)
    o_n, o_d, new_win = mixer_sample(cache_nsa_cmp[l], cache_nsa_sel[l], cache_nsa_win[l], cache_dsa_kv[l],
                                     cache_dsa_idx[l], page_table, rel_bias, cmp_pool[l], *parts_s)
    x1_s, h_s, lg_s = mix_back(x_sample, mod_s, o_n, o_d, zg_s)

    Tp, Ts = Bp * Sp, Bs * Ss
    h_all = jnp.concatenate([h_p.reshape(Tp, D), h_s.reshape(Ts, D)], 0)
    logits = jnp.concatenate([lg_p.reshape(Tp, N_EXPERTS), lg_s.reshape(Ts, N_EXPERTS)], 0)
    probs, row_tok, blk_e, n_used, dest_of = moe_route(logits)
    out_rows = moe_experts(h_all[row_tok], blk_e, n_used, w_gu[l], b_gu[l], w_down[l], b_down[l])
    picked = lambda lo, hi: [out_rows[dest_of[lo:hi, k]] for k in range(TOP_K)]
    y_p = moe_combine(picked(0, Tp), probs[:Tp], x1_p, g_post_ffn[l], mod_p[:, 5 * D:])
    y_s = moe_combine(picked(Tp, Tp + Ts), probs[Tp:], x1_s, g_post_ffn[l], mod_s[:, 5 * D:])

    st = lambda a: a[None]
    q_n, cmp_p, sel_p, win_p, g_n, q_d, dkv_p, q_i, ki_p, w_i = parts_p
    q_n, cmp_s, sel_s, win_s, g_n, q_d, dkv_s, q_i, ki_s, w_i = parts_s
    keep = min(WINDOW, Sp)
    return (y_p, y_s, st(cmp_p), st(sel_p), st(win_p[:, Sp - keep:]), st(dkv_p), st(ki_p),
            st(cmp_s), st(sel_s), st(new_win), st(dkv_s), st(ki_s))
```

```python
import functools
import math

import jax
import jax.numpy as jnp
import numpy as np
from jax import lax
from jax.experimental import pallas as pl
from jax.experimental.pallas import tpu as pltpu

F32 = jnp.float32
BF16 = jnp.bfloat16

D_MODEL = 2048
NSA_HEADS = 16
NSA_KV = 4
NSA_DH = 64
CMP_BLOCK = 64
N_SEL = 16
WINDOW = 512
DSA_HEADS = 16
DSA_KV = 4
DSA_DH = 64
IDX_HEADS = 16
IDX_DH = 64
DSA_TOPK_MAX = 256
NUM_BUCKETS = 32
REL_MAX_DIST = 128
N_EXPERTS = 32
TOP_K = 4
D_FF = 2048
SWIGLU_LIMIT = 7.0
SWIGLU_ALPHA = 1.702
PAGE_SIZE = 128
Q_BLOCK = 128
RMS_EPS = 1e-6
NEG = -1e30
NSA_WIDTH = NSA_HEADS * NSA_DH
DSA_WIDTH = DSA_HEADS * DSA_DH
KV_W = 2 * NSA_KV * NSA_DH

_SPLIT = (NSA_WIDTH, KV_W, KV_W, KV_W, 3 * NSA_HEADS, DSA_WIDTH, 2 * DSA_KV * DSA_DH,
          IDX_HEADS * IDX_DH, IDX_DH, IDX_HEADS, 2 * D_MODEL)
_OFF = tuple(int(v) for v in np.cumsum((0,) + _SPLIT))

MOE_TM = 512
VMEM_LIMIT = 48 * 1024 * 1024


def _cparams(sem):
    return pltpu.CompilerParams(dimension_semantics=sem, vmem_limit_bytes=VMEM_LIMIT)


def _norm_mod_kernel(x_ref, g_ref, sc_ref, sh_ref, o_ref):
    x = x_ref[0]
    y = x * lax.rsqrt(jnp.mean(x * x, axis=-1, keepdims=True) + RMS_EPS) * g_ref[...]
    o_ref[0] = (y * (1.0 + sc_ref[0]) + sh_ref[0]).astype(o_ref.dtype)


def norm_mod(x, g, scale, shift):
    B, S, D = x.shape
    ts = min(S, 512)
    row = pl.BlockSpec((1, ts, D), lambda b, s: (b, s, 0))
    per_b = pl.BlockSpec((1, 1, D), lambda b, s: (b, 0, 0))
    return pl.pallas_call(
        _norm_mod_kernel,
        out_shape=jax.ShapeDtypeStruct((B, S, D), BF16),
        grid=(B, S // ts),
        in_specs=[row, pl.BlockSpec((1, D), lambda b, s: (0, 0)), per_b, per_b],
        out_specs=row,
        compiler_params=_cparams(("parallel", "parallel")),
        name="norm_mod",
    )(x, g.reshape(1, D), scale.reshape(B, 1, D), shift.reshape(B, 1, D))


def _matmul_kernel(a_ref, w_ref, b_ref, o_ref):
    acc = jnp.dot(a_ref[...], w_ref[...].astype(BF16), preferred_element_type=F32)
    o_ref[...] = (acc + b_ref[...]).astype(o_ref.dtype)


def matmul(a, w, bias=None, out_dtype=F32, tm=512, tn=512):
    M, K = a.shape
    N = w.shape[1]
    tm, tn = min(tm, M), min(tn, N)
    assert M % tm == 0 and N % tn == 0
    if bias is None:
        bias = jnp.zeros((N,), F32)
    return pl.pallas_call(
        _matmul_kernel,
        out_shape=jax.ShapeDtypeStruct((M, N), out_dtype),
        grid=(M // tm, N // tn),
        in_specs=[pl.BlockSpec((tm, K), lambda i, j: (i, 0)),
                  pl.BlockSpec((K, tn), lambda i, j: (0, j)),
                  pl.BlockSpec((1, tn), lambda i, j: (0, j))],
        out_specs=pl.BlockSpec((tm, tn), lambda i, j: (i, j)),
        compiler_params=_cparams(("parallel", "parallel")),
        name="matmul",
    )(a, w, bias.reshape(1, N))


def _merge_kernel(on_ref, od_ref, ga_ref, gb_ref, wn_ref, wd_ref, o_ref):
    a = jnp.dot(on_ref[...], wn_ref[...], preferred_element_type=F32)
    b = jnp.dot(od_ref[...], wd_ref[...], preferred_element_type=F32)
    o_ref[...] = (jax.nn.sigmoid(ga_ref[...]) * a + jax.nn.sigmoid(gb_ref[...]) * b).astype(o_ref.dtype)


def merge_up(o_n, o_d, zg, w_up_nsa, w_up_dsa, tm=512, tn=512):
    M = o_n.shape[0]
    D = w_up_nsa.shape[1]
    tm = min(tm, M)
    nj = D // tn
    return pl.pallas_call(
        _merge_kernel,
        out_shape=jax.ShapeDtypeStruct((M, D), BF16),
        grid=(M // tm, nj),
        in_specs=[pl.BlockSpec((tm, o_n.shape[1]), lambda i, j: (i, 0)),
                  pl.BlockSpec((tm, o_d.shape[1]), lambda i, j: (i, 0)),
                  pl.BlockSpec((tm, tn), lambda i, j: (i, j)),
                  pl.BlockSpec((tm, tn), lambda i, j: (i, j + nj)),
                  pl.BlockSpec((o_n.shape[1], tn), lambda i, j: (0, j)),
                  pl.BlockSpec((o_d.shape[1], tn), lambda i, j: (0, j))],
        out_specs=pl.BlockSpec((tm, tn), lambda i, j: (i, j)),
        compiler_params=_cparams(("parallel", "parallel")),
        name="merge_up",
    )(o_n, o_d, zg, zg, w_up_nsa, w_up_dsa)


def _rms(x, g):
    return x * lax.rsqrt(jnp.mean(x * x, axis=-1, keepdims=True) + RMS_EPS) * g


def _out_proj_kernel(u_ref, x_ref, w_ref, gpost_ref, gpre_ref, ga_ref, sc_ref, sh_ref, wr_ref, br_ref,
                     x1_ref, h_ref, lg_ref):
    m = jnp.dot(u_ref[0], w_ref[...], preferred_element_type=F32)
    x1 = x_ref[0] + ga_ref[0] * _rms(m, gpost_ref[...])
    x1_ref[0] = x1
    h = _rms(x1, gpre_ref[...]) * (1.0 + sc_ref[0]) + sh_ref[0]
    h_ref[0] = h.astype(h_ref.dtype)
    lg_ref[0] = jnp.dot(h, wr_ref[...], preferred_element_type=F32,
                        precision=lax.Precision.HIGHEST) + br_ref[...]


def out_proj(u, x, w_out, g_post, g_pre, gate, scale, shift, w_router, b_router):
    B, S, D = x.shape
    ts = min(S, 256)
    E = w_router.shape[1]
    row = lambda d: pl.BlockSpec((1, ts, d), lambda b, s: (b, s, 0))
    per_b = pl.BlockSpec((1, 1, D), lambda b, s: (b, 0, 0))
    vec = lambda d: pl.BlockSpec((1, d), lambda b, s: (0, 0))
    return pl.pallas_call(
        _out_proj_kernel,
        out_shape=(jax.ShapeDtypeStruct((B, S, D), F32), jax.ShapeDtypeStruct((B, S, D), BF16),
                   jax.ShapeDtypeStruct((B, S, E), F32)),
        grid=(B, S // ts),
        in_specs=[row(D), row(D), pl.BlockSpec((D, D), lambda b, s: (0, 0)), vec(D), vec(D),
                  per_b, per_b, per_b, pl.BlockSpec((D, E), lambda b, s: (0, 0)), vec(E)],
        out_specs=(row(D), row(D), row(E)),
        compiler_params=_cparams(("parallel", "parallel")),
        name="out_proj",
    )(u.reshape(B, S, D), x, w_out, g_post.reshape(1, D), g_pre.reshape(1, D), gate.reshape(B, 1, D),
      scale.reshape(B, 1, D), shift.reshape(B, 1, D), w_router, b_router.reshape(1, E))


def _expert_changed(be_ref, b):
    return jnp.logical_or(b == 0, be_ref[b] != be_ref[jnp.maximum(b - 1, 0)])


def _moe_up_kernel(be_ref, nu_ref, x_ref, wg_ref, wl_ref, bg_ref, bl_ref, o_ref, wg_s, wl_s):
    b = pl.program_id(1)

    @pl.when(_expert_changed(be_ref, b))
    def _():
        wg_s[...] = wg_ref[0].astype(BF16)
        wl_s[...] = wl_ref[0].astype(BF16)

    @pl.when(b < nu_ref[0])
    def _():
        x = x_ref[...]
        g = jnp.dot(x, wg_s[...], preferred_element_type=F32) + bg_ref[0]
        lin = jnp.dot(x, wl_s[...], preferred_element_type=F32) + bl_ref[0]
        g = jnp.minimum(g, SWIGLU_LIMIT)
        lin = jnp.clip(lin, -SWIGLU_LIMIT, SWIGLU_LIMIT)
        o_ref[...] = (g * jax.nn.sigmoid(SWIGLU_ALPHA * g) * (lin + 1.0)).astype(o_ref.dtype)

    @pl.when(b >= nu_ref[0])
    def _():
        o_ref[...] = jnp.zeros_like(o_ref)


def _moe_down_kernel(be_ref, nu_ref, a_ref, w_ref, bias_ref, o_ref, w_s):
    b = pl.program_id(1)

    @pl.when(_expert_changed(be_ref, b))
    def _():
        w_s[...] = w_ref[0].astype(BF16)

    @pl.when(b < nu_ref[0])
    def _():
        o_ref[...] = jnp.dot(a_ref[...], w_s[...], preferred_element_type=F32) + bias_ref[0]

    @pl.when(b >= nu_ref[0])
    def _():
        o_ref[...] = jnp.zeros_like(o_ref)


def moe_experts(xs, blk_e, n_used, w_gu, b_gu, w_down, b_down, tf=512, tn=512):
    R, D = xs.shape
    nb = R // MOE_TM
    E = w_gu.shape[0]
    nf = D_FF // tf
    act = pl.pallas_call(
        _moe_up_kernel,
        out_shape=jax.ShapeDtypeStruct((R, D_FF), BF16),
        grid_spec=pltpu.PrefetchScalarGridSpec(
            num_scalar_prefetch=2,
            grid=(nf, nb),
            in_specs=[pl.BlockSpec((MOE_TM, D), lambda f, b, be, nu: (b, 0)),
                      pl.BlockSpec((1, D, tf), lambda f, b, be, nu: (be[b], 0, f)),
                      pl.BlockSpec((1, D, tf), lambda f, b, be, nu: (be[b], 0, f + nf)),
                      pl.BlockSpec((1, 1, tf), lambda f, b, be, nu: (be[b], 0, f)),
                      pl.BlockSpec((1, 1, tf), lambda f, b, be, nu: (be[b], 0, f + nf))],
            out_specs=pl.BlockSpec((MOE_TM, tf), lambda f, b, be, nu: (b, f)),
            scratch_shapes=[pltpu.VMEM((D, tf), BF16), pltpu.VMEM((D, tf), BF16)]),
        compiler_params=_cparams(("arbitrary", "arbitrary")),
        name="moe_up",
    )(blk_e, n_used, xs, w_gu, w_gu, b_gu.reshape(E, 1, 2 * D_FF), b_gu.reshape(E, 1, 2 * D_FF))
    nn = D // tn
    return pl.pallas_call(
        _moe_down_kernel,
        out_shape=jax.ShapeDtypeStruct((R, D), F32),
        grid_spec=pltpu.PrefetchScalarGridSpec(
            num_scalar_prefetch=2,
            grid=(nn, nb),
            in_specs=[pl.BlockSpec((MOE_TM, D_FF), lambda j, b, be, nu: (b, 0)),
                      pl.BlockSpec((1, D_FF, tn), lambda j, b, be, nu: (be[b], 0, j)),
                      pl.BlockSpec((1, 1, tn), lambda j, b, be, nu: (be[b], 0, j))],
            out_specs=pl.BlockSpec((MOE_TM, tn), lambda j, b, be, nu: (b, j)),
            scratch_shapes=[pltpu.VMEM((D_FF, tn), BF16)]),
        compiler_params=_cparams(("arbitrary", "arbitrary")),
        name="moe_down",
    )(blk_e, n_used, act, w_down, b_down.reshape(E, 1, D))


def _combine_kernel(p_ref, x_ref, g_ref, ga_ref, *refs):
    o_ref = refs[-1]
    p = p_ref[0]
    f = refs[0][0] * p[:, 0:1]
    for k in range(1, TOP_K):
        f = f + refs[k][0] * p[:, k:k + 1]
    o_ref[0] = x_ref[0] + ga_ref[0] * _rms(f, g_ref[...])


def moe_combine(rows, probs, x, g_post, gate):
    B, S, D = x.shape
    ts = min(S, 256)
    row = pl.BlockSpec((1, ts, D), lambda b, s: (b, s, 0))
    return pl.pallas_call(
        _combine_kernel,
        out_shape=jax.ShapeDtypeStruct((B, S, D), F32),
        grid=(B, S // ts),
        in_specs=[pl.BlockSpec((1, ts, TOP_K), lambda b, s: (b, s, 0)), row,
                  pl.BlockSpec((1, D), lambda b, s: (0, 0)),
                  pl.BlockSpec((1, 1, D), lambda b, s: (b, 0, 0))] + [row] * TOP_K,
        out_specs=row,
        compiler_params=_cparams(("parallel", "parallel")),
        name="moe_combine",
    )(probs.reshape(B, S, TOP_K), x, g_post.reshape(1, D), gate.reshape(B, 1, D),
      *[r.reshape(B, S, D) for r in rows])


def moe_route(logits):
    T = logits.shape[0]
    top_v, top_e = lax.top_k(logits, TOP_K)
    probs = jax.nn.softmax(top_v, axis=-1)
    flat_e = top_e.reshape(-1)
    n_assign = flat_e.shape[0]
    order = jnp.argsort(flat_e)
    e_sorted = flat_e[order]
    counts = jnp.bincount(flat_e, length=N_EXPERTS)
    padded = (counts + MOE_TM - 1) // MOE_TM * MOE_TM
    pad_end = jnp.cumsum(padded)
    start = jnp.cumsum(counts) - counts
    dest = (pad_end - padded)[e_sorted] + jnp.arange(n_assign) - start[e_sorted]
    n_blocks = -(-n_assign // MOE_TM) + N_EXPERTS
    row_tok = jnp.zeros((n_blocks * MOE_TM,), jnp.int32).at[dest].set((order // TOP_K).astype(jnp.int32))
    blk_e = jnp.minimum(jnp.searchsorted(pad_end, jnp.arange(n_blocks) * MOE_TM, side='right'),
                        N_EXPERTS - 1).astype(jnp.int32)
    dest_of = jnp.zeros((n_assign,), jnp.int32).at[order].set(dest.astype(jnp.int32))
    n_used = (pad_end[-1] // MOE_TM).astype(jnp.int32).reshape(1)
    return probs, row_tok, blk_e, n_used, dest_of.reshape(T, TOP_K)


def _rel_bucket(dist):
    n = np.maximum(np.asarray(dist), 0)
    exact = NUM_BUCKETS // 2
    log_b = exact + (np.log(np.maximum(n, 1).astype(np.float32) / np.float32(exact))
                     / np.float32(math.log(REL_MAX_DIST / exact)) * (NUM_BUCKETS - exact)).astype(np.int32)
    return np.where(n < exact, n, np.minimum(log_b, NUM_BUCKETS - 1)).astype(np.int32)


def _bias_rows(bkt, table):
    onehot = jax.nn.one_hot(jnp.asarray(bkt), NUM_BUCKETS, dtype=F32)
    return jnp.tensordot(onehot, table, axes=1, precision=lax.Precision.HIGHEST)


TQ = Q_BLOCK
HPG = NSA_HEADS // NSA_KV
INT_MIN = -2 ** 31


def _flash_step(q2, k, v, bias, mask, scale, carry):
    m, l, acc = carry
    nk = k.shape[0]
    s = lax.dot_general(q2, k, (((1,), (1,)), ((), ())), preferred_element_type=F32)
    s = s.reshape(HPG, TQ, nk) * scale + bias
    s = jnp.where(mask[None], s, NEG)
    m_new = jnp.maximum(m, jnp.max(s, axis=-1, keepdims=True))
    p = jnp.where(mask[None], jnp.exp(s - m_new), 0.0)
    alpha = jnp.exp(m - m_new)
    l = alpha * l + jnp.sum(p, axis=-1, keepdims=True)
    pv = jnp.dot(p.reshape(HPG * TQ, nk).astype(BF16), v, preferred_element_type=F32)
    acc = alpha * acc + pv.reshape(HPG, TQ, v.shape[1])
    return m_new, l, acc


def _flash_init(dh):
    return (jnp.full((HPG, TQ, 1), NEG, F32), jnp.zeros((HPG, TQ, 1), F32), jnp.zeros((HPG, TQ, dh), F32))


def _flash_out(carry):
    m, l, acc = carry
    return acc / jnp.maximum(l, 1e-30)


def _causal_blocks(i, q2, k_ref, v_ref, mask_ref, tiles_ref, far_ref, scale):
    dh = k_ref.shape[-1]
    a = lax.broadcasted_iota(jnp.int32, (TQ, TQ), 0)
    b = lax.broadcasted_iota(jnp.int32, (TQ, TQ), 1)

    def blk(ref, jb):
        return ref[pl.ds(pl.multiple_of(jb * TQ, TQ), TQ), :]

    first = i == 0
    jp = jnp.maximum(i - 1, 0)
    causal = jnp.where(a >= b, 1.0, 0.0)
    mask_a = jnp.where(first, mask_ref[jp] * causal, mask_ref[jp])
    mask_b = jnp.where(first, 0.0, mask_ref[jp + 1] * causal)
    bias = jnp.concatenate([jnp.where(first, tiles_ref[:, 0], tiles_ref[:, 1]), tiles_ref[:, 0]], axis=-1)
    near = pl.ds(pl.multiple_of(jp * TQ, TQ), 2 * TQ)
    carry = _flash_step(q2, k_ref[near, :], v_ref[near, :], bias,
                        jnp.concatenate([mask_a, mask_b], axis=1) > 0.5, scale, _flash_init(dh))
    far = far_ref[...][:, :, :1]
    n_far = jnp.maximum(i - 1, 0)

    def pair(jp2, c):
        rows = pl.ds(pl.multiple_of(jp2 * (2 * TQ), 2 * TQ), 2 * TQ)
        mask = jnp.concatenate([mask_ref[2 * jp2], mask_ref[2 * jp2 + 1]], axis=1) > 0.5
        return _flash_step(q2, k_ref[rows, :], v_ref[rows, :], far, mask, scale, c)

    def single(_, c):
        jb = n_far - 1
        return _flash_step(q2, blk(k_ref, jb), blk(v_ref, jb), far, mask_ref[jb] > 0.5, scale, c)

    carry = lax.fori_loop(0, n_far // 2, pair, carry)
    return lax.fori_loop(0, n_far % 2, single, carry)


def _nsa_prompt_kernel(q_ref, kc_ref, vc_ref, ks_ref, vs_ref, kw_ref, vw_ref, pool_ref, bc_ref, tiles_ref,
                       far_ref, gate_ref, o_ref, kcs, vcs, exp_s, mask_s):
    i = pl.program_id(2)
    S = ks_ref.shape[3]
    nb = S // CMP_BLOCK
    scale = NSA_DH ** -0.5

    @pl.when(i == 0)
    def _():
        kr = kc_ref[0, 0, 0].reshape(nb, CMP_BLOCK, NSA_DH) * pool_ref[0][None]
        vr = vc_ref[0, 0, 0].reshape(nb, CMP_BLOCK, NSA_DH) * pool_ref[1][None]
        kcs[...] = jnp.sum(kr, axis=1).astype(BF16)
        vcs[...] = jnp.sum(vr, axis=1).astype(BF16)
        n_i = lax.broadcasted_iota(jnp.int32, (nb, S), 0)
        s_i = lax.broadcasted_iota(jnp.int32, (nb, S), 1)
        exp_s[...] = jnp.where(s_i // CMP_BLOCK == n_i, 1.0, 0.0).astype(BF16)

    q2 = q_ref[0].reshape(HPG * TQ, NSA_DH)
    t0 = i * TQ

    s_c = lax.dot_general(q2, kcs[...], (((1,), (1,)), ((), ())), preferred_element_type=F32)
    s_c = s_c.reshape(HPG, TQ, nb) * scale + bc_ref[...]
    t_i = t0 + lax.broadcasted_iota(jnp.int32, (TQ, nb), 0)
    n_i = lax.broadcasted_iota(jnp.int32, (TQ, nb), 1)
    vis = t_i >= n_i * CMP_BLOCK + (CMP_BLOCK - 1)
    s_c = jnp.where(vis[None], s_c, NEG)
    e_c = jnp.where(vis[None], jnp.exp(s_c - jnp.max(s_c, axis=-1, keepdims=True)), 0.0)
    p_c = e_c / jnp.maximum(jnp.sum(e_c, axis=-1, keepdims=True), 1e-30)
    o_c = jnp.dot(p_c.reshape(HPG * TQ, nb).astype(BF16), vcs[...], preferred_element_type=F32)

    nt = (((1,), (1,)), ((), ()))
    eye = lambda n: jnp.where(lax.broadcasted_iota(jnp.int32, (n, n), 0) == lax.broadcasted_iota(jnp.int32, (n, n), 1),
                              1.0, 0.0)
    imp_t = lax.dot_general(eye(nb), jnp.sum(p_c, axis=0), nt, preferred_element_type=F32,
                            precision=lax.Precision.HIGHEST)
    n_t = lax.broadcasted_iota(jnp.int32, (nb, TQ), 0)
    t_t = t0 + lax.broadcasted_iota(jnp.int32, (nb, TQ), 1)
    cur = n_t == t_t // CMP_BLOCK
    work = jnp.where(jnp.logical_and(t_t >= n_t * CMP_BLOCK + (CMP_BLOCK - 1), jnp.logical_not(cur)), imp_t, -1.0)
    sel_t = jnp.where(cur, 1.0, 0.0)
    n_f = n_t.astype(F32)
    for _ in range(min(N_SEL - 1, nb)):
        mx = jnp.max(work, axis=0, keepdims=True)
        first = jnp.min(jnp.where(work == mx, n_f, float(nb)), axis=0, keepdims=True)
        pick = n_f == first
        sel_t = jnp.where(jnp.logical_and(pick, mx >= 0.0), 1.0, sel_t)
        work = jnp.where(pick, -2.0, work)
    sel = lax.dot_general(eye(TQ).astype(BF16), sel_t.astype(BF16), nt, preferred_element_type=F32)
    selx = jnp.dot(sel.astype(BF16), exp_s[...], preferred_element_type=F32)
    for jb in range(S // TQ):
        mask_s[jb] = selx[:, jb * TQ:(jb + 1) * TQ]

    o_s = _flash_out(_causal_blocks(i, q2, ks_ref.at[0, 0, 0], vs_ref.at[0, 0, 0], mask_s, tiles_ref, far_ref,
                                    scale))

    nw = WINDOW + TQ
    rows = pl.ds(pl.multiple_of(t0, TQ), nw)
    s_w = lax.dot_general(q2, kw_ref[0, 0, 0, rows, :], (((1,), (1,)), ((), ())), preferred_element_type=F32)
    far_w = jnp.broadcast_to(far_ref[...][:, :, :1], (HPG, TQ, nw - 2 * TQ))
    s_w = s_w.reshape(HPG, TQ, nw) * scale + jnp.concatenate([far_w, tiles_ref[:, 1], tiles_ref[:, 0]], axis=-1)
    c_w = lax.broadcasted_iota(jnp.int32, (TQ, nw), 1)
    d_w = lax.broadcasted_iota(jnp.int32, (TQ, nw), 0) + WINDOW - c_w
    in_w = jnp.logical_and(jnp.logical_and(d_w >= 0, d_w < WINDOW), t0 - WINDOW + c_w >= 0)
    s_w = jnp.where(in_w[None], s_w, NEG)
    e_w = jnp.where(in_w[None], jnp.exp(s_w - jnp.max(s_w, axis=-1, keepdims=True)), 0.0)
    o_w = jnp.dot(e_w.reshape(HPG * TQ, nw).astype(BF16), vw_ref[0, 0, 0, rows, :], preferred_element_type=F32)
    o_w = o_w.reshape(HPG, TQ, NSA_DH) / jnp.maximum(jnp.sum(e_w, axis=-1, keepdims=True), 1e-30)

    g = jax.nn.sigmoid(gate_ref[0])
    o = g[..., 0:1] * o_c.reshape(HPG, TQ, NSA_DH) + g[..., 1:2] * o_s + g[..., 2:3] * o_w
    for j in range(HPG):
        o_ref[0, :, j * NSA_DH:(j + 1) * NSA_DH] = o[j].astype(o_ref.dtype)


def _bias_tables(rel_bias, S):
    assert 2 * TQ - (TQ - 1) >= REL_MAX_DIST
    nb = S // CMP_BLOCK
    t = np.arange(S)[:, None]
    bc = _bias_rows(_rel_bucket(t - (np.arange(nb)[None, :] * CMP_BLOCK + CMP_BLOCK - 1)), rel_bias)
    d = np.arange(2)[:, None, None] * TQ + np.arange(TQ)[None, :, None] - np.arange(TQ)[None, None, :]
    tiles = _bias_rows(_rel_bucket(d), rel_bias)
    far = jnp.broadcast_to(rel_bias[NUM_BUCKETS - 1][:, None, None], (rel_bias.shape[1], 1, TQ))
    return bc.transpose(2, 0, 1), tiles.transpose(3, 0, 1, 2), far


def nsa_prompt(q_n, cmp_t, sel_t, win_t, pool, g_n, bias_c, tiles, far):
    B, H, S, dh = q_n.shape
    assert S >= 2 * TQ
    G = NSA_KV
    nb = S // CMP_BLOCK
    assert win_t.shape[3] == S + WINDOW and WINDOW >= 2 * TQ
    kv = lambda c, n=S: pl.BlockSpec((1, 1, 1, n, dh), lambda b, g, i: (b, c, g, 0, 0))
    return pl.pallas_call(
        _nsa_prompt_kernel,
        out_shape=jax.ShapeDtypeStruct((B, S, H * dh), BF16),
        grid=(B, G, S // TQ),
        in_specs=[pl.BlockSpec((1, HPG, TQ, dh), lambda b, g, i: (b, g, i, 0)),
                  kv(0), kv(1), kv(0), kv(1), kv(0, S + WINDOW), kv(1, S + WINDOW),
                  pl.BlockSpec((2, CMP_BLOCK, dh), lambda b, g, i: (0, 0, 0)),
                  pl.BlockSpec((HPG, TQ, nb), lambda b, g, i: (g, i, 0)),
                  pl.BlockSpec((HPG, 2, TQ, TQ), lambda b, g, i: (g, 0, 0, 0)),
                  pl.BlockSpec((HPG, 1, TQ), lambda b, g, i: (g, 0, 0)),
                  pl.BlockSpec((1, HPG, TQ, 3), lambda b, g, i: (b, g, i, 0))],
        out_specs=pl.BlockSpec((1, TQ, HPG * dh), lambda b, g, i: (b, i, g)),
        scratch_shapes=[pltpu.VMEM((nb, dh), BF16), pltpu.VMEM((nb, dh), BF16), pltpu.VMEM((nb, S), BF16),
                        pltpu.VMEM((S // TQ, TQ, TQ), F32)],
        compiler_params=_cparams(("parallel", "parallel", "arbitrary")),
        name="nsa_prompt",
    )(q_n, cmp_t, cmp_t, sel_t, sel_t, win_t, win_t,
      jnp.broadcast_to(pool[:, :, None], (2, CMP_BLOCK, dh)), bias_c, tiles, far, g_n)


def _dsa_prompt_kernel(qi_ref, ki_ref, wi_ref, qd_ref, kd_ref, vd_ref, tiles_ref, far_ref, o_ref,
                       key_s, mask_s, *, topk):
    i = pl.program_id(1)
    S = ki_ref.shape[1]
    nkb = S // TQ
    t0 = i * TQ
    a = lax.broadcasted_iota(jnp.int32, (TQ, TQ), 0)
    b = lax.broadcasted_iota(jnp.int32, (TQ, TQ), 1)

    w = wi_ref[0] * (IDX_DH ** -0.5 * IDX_HEADS ** -0.5)
    w_cols = [jnp.broadcast_to(w[:, h:h + 1], (TQ, TQ)) for h in range(IDX_HEADS)]
    for jb in range(nkb):
        @pl.when(jb <= i)
        def _():
            kk = ki_ref[0, jb * TQ:(jb + 1) * TQ, :]
            acc = jnp.zeros((TQ, TQ), F32)
            for h in range(IDX_HEADS):
                s = lax.dot_general(qi_ref[0, h], kk, (((1,), (1,)), ((), ())), preferred_element_type=F32)
                acc = acc + jnp.maximum(s, 0.0) * w_cols[h]
            score = jnp.where(jb * TQ + b <= t0 + a, acc, NEG) + 0.0
            bits = pltpu.bitcast(score, jnp.int32)
            key_s[jb] = jnp.where(bits < 0, bits ^ 0x7FFFFFFF, bits)

        @pl.when(jb > i)
        def _():
            bits = pltpu.bitcast(jnp.full((TQ, TQ), NEG, F32), jnp.int32)
            key_s[jb] = bits ^ 0x7FFFFFFF

    half = nkb // 2
    if half * TQ >= topk:
        thr, last = lax.cond(i < half, lambda: _kth_threshold(key_s[:half], topk),
                             lambda: _kth_threshold(key_s[...], topk))
    else:
        thr, last = _kth_threshold(key_s[...], topk)
    keys = key_s[...]
    pos = (lax.broadcasted_iota(jnp.int32, (nkb, TQ, TQ), 0) * TQ
           + lax.broadcasted_iota(jnp.int32, (nkb, TQ, TQ), 2))
    chosen = jnp.logical_or(keys > thr, jnp.logical_and(keys == thr, pos <= last))
    mask_s[...] = jnp.where(chosen, 1.0, 0.0)

    scale = DSA_DH ** -0.5
    for g in range(DSA_KV):
        q2 = qd_ref[0, g * HPG:(g + 1) * HPG].reshape(HPG * TQ, DSA_DH)
        o = _flash_out(_causal_blocks(i, q2, kd_ref.at[0, 0, g], vd_ref.at[0, 0, g], mask_s,
                                      tiles_ref.at[g * HPG:(g + 1) * HPG], far_ref.at[g * HPG:(g + 1) * HPG],
                                      scale))
        for j in range(HPG):
            h = g * HPG + j
            o_ref[0, :, h * DSA_DH:(h + 1) * DSA_DH] = o[j].astype(o_ref.dtype)


def dsa_prompt(q_i, k_i, w_i, q_d, dkv_t, tiles, far):
    B, H, S, dh = q_d.shape
    assert S >= 2 * TQ
    G = DSA_KV
    topk = min(DSA_TOPK_MAX, S // 4)
    heads = pl.BlockSpec((1, H, TQ, dh), lambda b, i: (b, 0, i, 0))
    kv = lambda c: pl.BlockSpec((1, 1, G, S, dh), lambda b, i: (b, c, 0, 0, 0))
    return pl.pallas_call(
        functools.partial(_dsa_prompt_kernel, topk=topk),
        out_shape=jax.ShapeDtypeStruct((B, S, H * dh), BF16),
        grid=(B, S // TQ),
        in_specs=[heads,
                  pl.BlockSpec((1, S, k_i.shape[2]), lambda b, i: (b, 0, 0)),
                  pl.BlockSpec((1, TQ, H), lambda b, i: (b, i, 0)),
                  heads, kv(0), kv(1),
                  pl.BlockSpec((H, 2, TQ, TQ), lambda b, i: (0, 0, 0, 0)),
                  pl.BlockSpec((H, 1, TQ), lambda b, i: (0, 0, 0))],
        out_specs=pl.BlockSpec((1, TQ, H * dh), lambda b, i: (b, i, 0)),
        scratch_shapes=[pltpu.VMEM((S // TQ, TQ, TQ), jnp.int32), pltpu.VMEM((S // TQ, TQ, TQ), F32)],
        compiler_params=_cparams(("parallel", "arbitrary")),
        name="dsa_prompt",
    )(q_i, k_i, w_i, q_d, dkv_t, dkv_t, tiles, far)


def mixer_prompt(rel_bias, pool, q_n, cmp, sel, win, g_n, q_d, dkv, q_i, k_i, w_i):
    B, S = q_n.shape[:2]
    heads_first = lambda a: a.transpose(0, 2, 1, 3)
    kv_t = lambda a, dt: a.transpose(0, 2, 3, 1, 4).astype(dt)
    bias_c, tiles, far = _bias_tables(rel_bias, S)
    win_t = jnp.pad(kv_t(win, BF16), ((0, 0), (0, 0), (0, 0), (WINDOW, 0), (0, 0)))
    o_n = nsa_prompt(heads_first(q_n).astype(BF16), kv_t(cmp, F32), kv_t(sel, BF16), win_t, pool,
                     heads_first(g_n), bias_c[:NSA_HEADS], tiles[:NSA_HEADS], far[:NSA_HEADS])
    o_d = dsa_prompt(heads_first(q_i).astype(BF16), k_i.astype(BF16), w_i, heads_first(q_d).astype(BF16),
                     kv_t(dkv, BF16), tiles[NSA_HEADS:], far[NSA_HEADS:])
    return o_n, o_d


KVH = NSA_KV * NSA_DH
PAGES_PER_STEP = 8


def _pad_groups(q, n_heads):
    R, dh = q.shape
    G = n_heads // HPG
    col_g = lax.broadcasted_iota(jnp.int32, (R, G * dh), 1) // dh
    row_g = (lax.broadcasted_iota(jnp.int32, (R, G * dh), 0) % n_heads) // HPG
    return jnp.where(col_g == row_g, jnp.concatenate([q] * G, axis=1), 0.0).astype(BF16)


def _own_group(o_pad, n_heads):
    R = o_pad.shape[0]
    G = n_heads // HPG
    dh = o_pad.shape[1] // G
    row_g = (lax.broadcasted_iota(jnp.int32, (R, dh), 0) % n_heads) // HPG
    out = o_pad[:, :dh]
    for g in range(1, G):
        out = jnp.where(row_g == g, o_pad[:, g * dh:(g + 1) * dh], out)
    return out


def _order_key(x):
    bits = pltpu.bitcast(x, jnp.int32)
    return jnp.where(bits < 0, bits ^ 0x7FFFFFFF, bits)


def _kth_threshold(keys, topk):
    n, R, L = keys.shape

    def count(pred):
        c = jnp.sum(jnp.where(pred, 1.0, 0.0), axis=0)
        return jnp.sum(c, axis=-1, keepdims=True)[None]

    def bit_body(i, pat):
        cand = pat | lax.shift_left(jnp.int32(1), 31 - i)
        return jnp.where(count(keys >= (cand ^ INT_MIN)) >= topk, cand, pat)

    thr = lax.fori_loop(0, 32, bit_body, jnp.zeros((1, R, 1), jnp.int32)) ^ INT_MIN
    need = topk - count(keys > thr)
    pos = lax.broadcasted_iota(jnp.int32, (n, R, L), 0) * L + lax.broadcasted_iota(jnp.int32, (n, R, L), 2)
    tie = keys == thr
    nbits = (n * L - 1).bit_length()

    def tie_body(i, last):
        cand = last | lax.shift_left(jnp.int32(1), nbits - 1 - i)
        return jnp.where(count(jnp.logical_and(tie, pos < cand)) < need, cand, last)

    crowded = jnp.max(jnp.where(count(tie) > need, 1.0, 0.0)) > 0.5
    last = lax.cond(crowded,
                    lambda: lax.fori_loop(0, nbits, tie_body, jnp.zeros((1, R, 1), jnp.int32)),
                    lambda: jnp.full((1, R, 1), n * L, jnp.int32))
    return thr, last


def _cmp_pages_kernel(pt_ref, pw_ref, *refs):
    o_ref = refs[-1]
    Wd, ncol = o_ref.shape[2], o_ref.shape[3]
    bpp = PAGE_SIZE // CMP_BLOCK
    col_id = lax.broadcasted_iota(jnp.int32, (Wd, ncol), 1)
    blk_of_row = lax.broadcasted_iota(jnp.int32, (Wd, PAGE_SIZE), 1) // CMP_BLOCK
    out = jnp.zeros((Wd, ncol), F32)
    for k, page in enumerate(refs[:-1]):
        y = page[0] * pw_ref[...]
        for j in range(bpp):
            summary = jnp.sum(jnp.where(blk_of_row == j, y, 0.0), axis=-1, keepdims=True)
            out = jnp.where(col_id == k * bpp + j, summary, out)
    o_ref[0, 0] = out


def cmp_pages(pt_flat, cache_t, pool, B):
    n_pages = pt_flat.shape[0] // B
    pgs = min(PAGES_PER_STEP, n_pages)
    bpp = PAGE_SIZE // CMP_BLOCK
    Wd = cache_t.shape[1]
    pw = jnp.repeat(pool[:, jnp.arange(PAGE_SIZE) % CMP_BLOCK], Wd // 2, axis=0)
    page = lambda k: pl.BlockSpec((1, Wd, PAGE_SIZE), lambda b, s, pt: (pt[b * n_pages + s * pgs + k], 0, 0))
    out = pl.pallas_call(
        _cmp_pages_kernel,
        out_shape=jax.ShapeDtypeStruct((B, n_pages // pgs, Wd, pgs * bpp), F32),
        grid_spec=pltpu.PrefetchScalarGridSpec(
            num_scalar_prefetch=1, grid=(B, n_pages // pgs),
            in_specs=[pl.BlockSpec((Wd, PAGE_SIZE), lambda b, s, pt: (0, 0))]
                     + [page(k) for k in range(pgs)],
            out_specs=pl.BlockSpec((1, 1, Wd, pgs * bpp), lambda b, s, pt: (b, s, 0, 0))),
        compiler_params=_cparams(("parallel", "parallel")),
        name="cmp_pages",
    )(pt_flat, pw, *([cache_t] * pgs))
    return out.transpose(0, 2, 1, 3).reshape(B, Wd, n_pages * bpp)


def _nsa_sample_pre_kernel(q_ref, kvc_ref, wb_ref, new_ref, bc_ref, bw_ref, bn_ref, oc_ref, ow_ref, sel_ref,
                           *, past, n_new):
    H = NSA_HEADS
    R = q_ref.shape[1]
    nb = kvc_ref.shape[2]
    W = wb_ref.shape[2]
    scale = NSA_DH ** -0.5
    nt = (((1,), (1,)), ((), ()))
    qp = _pad_groups(q_ref[0], H)

    kvc = kvc_ref[0]
    s_c = jnp.dot(qp, kvc[:KVH].astype(BF16), preferred_element_type=F32) * scale + bc_ref[...]
    q_pos = past + lax.broadcasted_iota(jnp.int32, (R, nb), 0) // H
    n_i = lax.broadcasted_iota(jnp.int32, (R, nb), 1)
    vis = q_pos >= n_i * CMP_BLOCK + (CMP_BLOCK - 1)
    s_c = jnp.where(vis, s_c, NEG)
    e_c = jnp.where(vis, jnp.exp(s_c - jnp.max(s_c, axis=-1, keepdims=True)), 0.0)
    p_c = e_c / jnp.maximum(jnp.sum(e_c, axis=-1, keepdims=True), 1e-30)
    oc_ref[0] = _own_group(lax.dot_general(p_c.astype(BF16), kvc[KVH:].astype(BF16), nt,
                                           preferred_element_type=F32), H)

    same = (lax.broadcasted_iota(jnp.int32, (R, R), 0) // HPG == lax.broadcasted_iota(jnp.int32, (R, R), 1) // HPG)
    imp = jnp.dot(jnp.where(same, 1.0, 0.0), p_c, preferred_element_type=F32, precision=lax.Precision.HIGHEST)
    cur = n_i == q_pos // CMP_BLOCK
    work = jnp.where(jnp.logical_and(vis, jnp.logical_not(cur)), imp, -1.0)
    sel = jnp.where(cur, 1.0, 0.0)
    n_f = n_i.astype(F32)
    for _ in range(min(N_SEL - 1, nb)):
        mx = jnp.max(work, axis=-1, keepdims=True)
        first = jnp.min(jnp.where(work == mx, n_f, float(nb)), axis=-1, keepdims=True)
        pick = n_f == first
        sel = jnp.where(jnp.logical_and(pick, mx >= 0.0), 1.0, sel)
        work = jnp.where(pick, -2.0, work)
    sel_ref[0] = sel.astype(sel_ref.dtype)

    wb, new = wb_ref[0], new_ref[0]
    s1 = jnp.dot(qp, wb[:KVH].astype(BF16), preferred_element_type=F32) * scale + bw_ref[...]
    i1 = lax.broadcasted_iota(jnp.int32, (R, W), 1)
    d1 = W + lax.broadcasted_iota(jnp.int32, (R, W), 0) // H - i1
    m1 = jnp.logical_and(jnp.logical_and(d1 >= 0, d1 < WINDOW), past - W + i1 >= 0)
    L = new.shape[0]
    s2 = lax.dot_general(qp, new[:, :KVH].astype(BF16), nt, preferred_element_type=F32) * scale + bn_ref[...]
    j2 = lax.broadcasted_iota(jnp.int32, (R, L), 1)
    d2 = lax.broadcasted_iota(jnp.int32, (R, L), 0) // H - j2
    m2 = jnp.logical_and(jnp.logical_and(d2 >= 0, d2 < WINDOW), j2 < n_new)
    s1 = jnp.where(m1, s1, NEG)
    s2 = jnp.where(m2, s2, NEG)
    mx = jnp.maximum(jnp.max(s1, axis=-1, keepdims=True), jnp.max(s2, axis=-1, keepdims=True))
    e1 = jnp.where(m1, jnp.exp(s1 - mx), 0.0)
    e2 = jnp.where(m2, jnp.exp(s2 - mx), 0.0)
    den = jnp.sum(e1, axis=-1, keepdims=True) + jnp.sum(e2, axis=-1, keepdims=True)
    o_w = (lax.dot_general(e1.astype(BF16), wb[KVH:].astype(BF16), nt, preferred_element_type=F32)
           + jnp.dot(e2.astype(BF16), new[:, KVH:].astype(BF16), preferred_element_type=F32))
    ow_ref[0] = _own_group(o_w / jnp.maximum(den, 1e-30), H)


def nsa_sample_pre(q, kvc, wbuf, new_win, bias_c, bias_w, bias_new, past, n_new):
    B, R, dh = q.shape
    nb, W, L = kvc.shape[2], wbuf.shape[2], new_win.shape[1]
    per_b = lambda *s: pl.BlockSpec((1,) + s, lambda b: (b,) + (0,) * len(s))
    const = lambda *s: pl.BlockSpec(s, lambda b: (0,) * len(s))
    return pl.pallas_call(
        functools.partial(_nsa_sample_pre_kernel, past=past, n_new=n_new),
        out_shape=(jax.ShapeDtypeStruct((B, R, dh), F32), jax.ShapeDtypeStruct((B, R, dh), F32),
                   jax.ShapeDtypeStruct((B, R, nb), BF16)),
        grid=(B,),
        in_specs=[per_b(R, dh), per_b(2 * KVH, nb), per_b(2 * KVH, W), per_b(L, 2 * KVH),
                  const(R, nb), const(R, W), const(R, L)],
        out_specs=(per_b(R, dh), per_b(R, dh), per_b(R, nb)),
        compiler_params=_cparams(("parallel",)),
        name="nsa_sample_pre",
    )(q, kvc, wbuf, new_win, bias_c, bias_w, bias_new)


def _index_scores(qi, w, keys, n_heads, feature_major):
    contract = (((1,), (0,)), ((), ())) if feature_major else (((1,), (1,)), ((), ()))
    s = lax.dot_general(qi, keys.astype(BF16), contract, preferred_element_type=F32)
    v = jnp.maximum(s * IDX_DH ** -0.5, 0.0) * w
    T = qi.shape[0] // n_heads
    return jnp.sum(v.reshape(T, n_heads, s.shape[1]), axis=1) * n_heads ** -0.5 + 0.0


def _idx_scores_kernel(pt_ref, q_ref, w_ref, *refs):
    o_ref = refs[-1]
    qi = q_ref[0].astype(BF16)
    for k, page in enumerate(refs[:-1]):
        o_ref[0, k] = _index_scores(qi, w_ref[0], page[0], IDX_HEADS, True)


def idx_scores(pt_flat, q_i, w_i, cache_t):
    B, R, dh = q_i.shape
    T = R // IDX_HEADS
    n_pages = pt_flat.shape[0] // B
    pgs = min(2 * PAGES_PER_STEP, n_pages)
    page = lambda k: pl.BlockSpec((1, dh, PAGE_SIZE), lambda b, s, pt: (pt[b * n_pages + s * pgs + k], 0, 0))
    return pl.pallas_call(
        _idx_scores_kernel,
        out_shape=jax.ShapeDtypeStruct((B, n_pages, T, PAGE_SIZE), F32),
        grid_spec=pltpu.PrefetchScalarGridSpec(
            num_scalar_prefetch=1, grid=(B, n_pages // pgs),
            in_specs=[pl.BlockSpec((1, R, dh), lambda b, s, pt: (b, 0, 0)),
                      pl.BlockSpec((1, R, 1), lambda b, s, pt: (b, 0, 0))] + [page(k) for k in range(pgs)],
            out_specs=pl.BlockSpec((1, pgs, T, PAGE_SIZE), lambda b, s, pt: (b, s, 0, 0))),
        compiler_params=_cparams(("parallel", "parallel")),
        name="idx_scores",
    )(pt_flat, q_i, w_i, *([cache_t] * pgs))


def _dsa_sample_pre_kernel(sp_ref, q_ref, w_ref, kn_ref, sn_ref, thr_ref, last_ref, *, topk, n_new):
    T, L = sp_ref.shape[2], sp_ref.shape[3]
    sc = _index_scores(q_ref[0].astype(BF16), w_ref[0], kn_ref[0], IDX_HEADS, False)
    j = lax.broadcasted_iota(jnp.int32, (T, L), 1)
    t = lax.broadcasted_iota(jnp.int32, (T, L), 0)
    sc = jnp.where(jnp.logical_and(j <= t, j < n_new), sc, NEG)
    sn_ref[0, 0] = sc
    keys = jnp.concatenate([_order_key(sp_ref[0]), _order_key(sc)[None]], axis=0)
    thr, last = _kth_threshold(keys, topk)
    thr_ref[0] = jnp.broadcast_to(thr[0], (T, L))
    last_ref[0] = jnp.broadcast_to(last[0], (T, L))


def dsa_sample_pre(scores_past, q_i, w_i, new_ki, topk, n_new):
    B, n_pages, T, L = scores_past.shape
    R, dh = q_i.shape[1:]
    per_b = lambda *s: pl.BlockSpec((1,) + s, lambda b: (b,) + (0,) * len(s))
    return pl.pallas_call(
        functools.partial(_dsa_sample_pre_kernel, topk=topk, n_new=n_new),
        out_shape=(jax.ShapeDtypeStruct((B, 1, T, L), F32), jax.ShapeDtypeStruct((B, T, L), jnp.int32),
                   jax.ShapeDtypeStruct((B, T, L), jnp.int32)),
        grid=(B,),
        in_specs=[per_b(n_pages, T, L), per_b(R, dh), per_b(R, 1), per_b(L, dh)],
        out_specs=(per_b(1, T, L), per_b(T, L), per_b(T, L)),
        compiler_params=_cparams(("parallel",)),
        name="dsa_sample_pre",
    )(scores_past, q_i, w_i, new_ki)


def _paged_attn_kernel(pt_ref, *refs, mode, pgs, n_pages, n_heads, n_new, dh):
    if mode == "blocks":
        q_ref, new_ref, far_ref, tl_ref, tn_ref, sel_ref, oc_ref, ow_ref, gate_ref = refs[:9]
        rest = refs[9:]
    else:
        q_ref, new_ref, far_ref, tl_ref, tn_ref, sp_ref, sn_ref, thr_ref, last_ref = refs[:9]
        rest = refs[9:]
    pages, o_ref = rest[:pgs], rest[pgs]
    qp_s, m_s, l_s, acc_s = rest[pgs + 1:]
    step, n_steps = pl.program_id(1), pl.num_programs(1)
    R = q_ref.shape[1]
    T = R // n_heads
    L = PAGE_SIZE
    scale = dh ** -0.5

    @pl.when(step == 0)
    def _():
        qp_s[...] = _pad_groups(q_ref[0], n_heads)
        m_s[...] = jnp.full(m_s.shape, NEG, F32)
        l_s[...] = jnp.zeros(l_s.shape, F32)
        acc_s[...] = jnp.zeros(acc_s.shape, F32)

    def update_new(kv, bias, mask):
        s = lax.dot_general(qp_s[...], kv[:, :KVH].astype(BF16), (((1,), (1,)), ((), ())),
                            preferred_element_type=F32)
        s = jnp.where(mask, s * scale + bias, NEG)
        m_old = m_s[...]
        m_new = jnp.maximum(m_old, jnp.max(s, axis=-1, keepdims=True))
        p = jnp.where(mask, jnp.exp(s - m_new), 0.0)
        alpha = jnp.exp(m_old - m_new)
        l_s[...] = alpha * l_s[...] + jnp.sum(p, axis=-1, keepdims=True)
        acc_s[...] = alpha * acc_s[...] + jnp.dot(p.astype(BF16), kv[:, KVH:].astype(BF16),
                                                  preferred_element_type=F32)
        m_s[...] = m_new

    def rows_of_tokens(x):
        return jnp.broadcast_to(x[:, None, :], (T, n_heads, L)).reshape(R, L)

    def chosen(scores, page):
        key = _order_key(scores)
        pos = page * L + lax.broadcasted_iota(jnp.int32, (T, L), 1)
        thr = thr_ref[0]
        ch = jnp.logical_or(key > thr, jnp.logical_and(key == thr, pos <= last_ref[0]))
        return rows_of_tokens(jnp.where(ch, 1.0, 0.0)) > 0.5

    qp = qp_s[...]
    scores, masks = [], []
    for k in range(pgs):
        page = step * pgs + k
        bias = jnp.where(page == n_pages - 1, tl_ref[...], far_ref[...])
        if mode == "blocks":
            nb = sel_ref.shape[2]
            blk = page * (L // CMP_BLOCK) + lax.broadcasted_iota(jnp.int32, (nb, L), 1) // CMP_BLOCK
            expand = jnp.where(lax.broadcasted_iota(jnp.int32, (nb, L), 0) == blk, 1.0, 0.0).astype(BF16)
            mask = jnp.dot(sel_ref[0], expand, preferred_element_type=F32) > 0.5
        else:
            mask = chosen(sp_ref[0, k], page)
        s = jnp.dot(qp, pages[k][0, :KVH].astype(BF16), preferred_element_type=F32)
        scores.append(jnp.where(mask, s * scale + bias, NEG))
        masks.append(mask)
    m_old = m_s[...]
    m_new = m_old
    for s in scores:
        m_new = jnp.maximum(m_new, jnp.max(s, axis=-1, keepdims=True))
    alpha = jnp.exp(m_old - m_new)
    l_new = alpha * l_s[...]
    acc = alpha * acc_s[...]
    for k in range(pgs):
        p = jnp.where(masks[k], jnp.exp(scores[k] - m_new), 0.0)
        l_new = l_new + jnp.sum(p, axis=-1, keepdims=True)
        acc = acc + lax.dot_general(p.astype(BF16), pages[k][0, KVH:].astype(BF16), (((1,), (1,)), ((), ())),
                                    preferred_element_type=F32)
    m_s[...] = m_new
    l_s[...] = l_new
    acc_s[...] = acc

    @pl.when(step == n_steps - 1)
    def _():
        j = lax.broadcasted_iota(jnp.int32, (R, L), 1)
        t = lax.broadcasted_iota(jnp.int32, (R, L), 0) // n_heads
        mask = jnp.logical_and(j <= t, j < n_new)
        if mode != "blocks":
            mask = jnp.logical_and(mask, chosen(sn_ref[0, 0], n_pages))
        update_new(new_ref[0], tn_ref[...], mask)
        o = _own_group(acc_s[...] / jnp.maximum(l_s[...], 1e-30), n_heads)
        if mode == "blocks":
            g = jax.nn.sigmoid(gate_ref[0])
            o = g[:, 0:1] * oc_ref[0] + g[:, 1:2] * o + g[:, 2:3] * ow_ref[0]
        o_ref[0] = o.astype(o_ref.dtype)


def paged_attn(mode, pt_flat, cache_t, q, new_rows, far, tile_last, tile_new, extra, n_heads, n_new):
    B, R, dh = q.shape
    n_pages = pt_flat.shape[0] // B
    pgs = min(PAGES_PER_STEP, n_pages)
    L = PAGE_SIZE
    Wd = cache_t.shape[1]
    per_b = lambda *s: pl.BlockSpec((1,) + s, lambda b, st, pt: (b,) + (0,) * len(s))
    const = lambda *s: pl.BlockSpec(s, lambda b, st, pt: (0,) * len(s))
    page = lambda k: pl.BlockSpec((1, Wd, L), lambda b, st, pt: (pt[b * n_pages + st * pgs + k], 0, 0))
    if mode == "blocks":
        sel = extra[0]
        extra_specs = [per_b(R, sel.shape[2]), per_b(R, dh), per_b(R, dh), per_b(R, 3)]
    else:
        T = R // n_heads
        extra_specs = [pl.BlockSpec((1, pgs, T, L), lambda b, st, pt: (b, st, 0, 0)), per_b(1, T, L),
                       per_b(T, L), per_b(T, L)]
    return pl.pallas_call(
        functools.partial(_paged_attn_kernel, mode=mode, pgs=pgs, n_pages=n_pages, n_heads=n_heads, n_new=n_new,
                          dh=dh),
        out_shape=jax.ShapeDtypeStruct((B, R, dh), BF16),
        grid_spec=pltpu.PrefetchScalarGridSpec(
            num_scalar_prefetch=1, grid=(B, n_pages // pgs),
            in_specs=[per_b(R, dh), per_b(L, Wd), const(R, 1), const(R, L), const(R, L)] + extra_specs
                     + [page(k) for k in range(pgs)],
            out_specs=per_b(R, dh),
            scratch_shapes=[pltpu.VMEM((R, KVH), BF16), pltpu.VMEM((R, 1), F32), pltpu.VMEM((R, 1), F32),
                            pltpu.VMEM((R, KVH), F32)]),
        compiler_params=_cparams(("parallel", "arbitrary")),
        name="paged_attn_" + mode,
    )(pt_flat, q, new_rows, far, tile_last, tile_new, *extra, *([cache_t] * pgs))


def _sample_bias(rel_bias, head0, n_heads, T, past, W, nb):
    assert PAGE_SIZE + 1 >= REL_MAX_DIST
    R = T * n_heads
    t = (np.arange(R) // n_heads)[:, None]
    per_row = jnp.tile(rel_bias[:, head0:head0 + n_heads], (1, T))
    onehot = lambda dist: jax.nn.one_hot(jnp.asarray(_rel_bucket(dist)), NUM_BUCKETS, dtype=F32)
    tab = lambda dist: jnp.einsum('rxk,kr->rx', onehot(dist), per_row, precision=lax.Precision.HIGHEST)
    lane = np.arange(PAGE_SIZE)[None, :]
    return dict(far=per_row[NUM_BUCKETS - 1][:, None], last=tab(PAGE_SIZE + t - lane), new=tab(t - lane),
                cmp=tab(past + t - (np.arange(nb)[None, :] * CMP_BLOCK + CMP_BLOCK - 1)),
                win=tab(W + t - np.arange(W)[None, :]))


def mixer_sample(cache_cmp, cache_sel, cache_win, cache_dkv, cache_idx, page_table, rel_bias, pool,
                 q_n, cmp, sel, win, g_n, q_d, dkv, q_i, k_i, w_i):
    B, T = q_n.shape[:2]
    n_pages = page_table.shape[1]
    past = n_pages * PAGE_SIZE
    W = cache_win.shape[1]
    assert T < CMP_BLOCK and past % CMP_BLOCK == 0 and T <= PAGE_SIZE
    topk = min(DSA_TOPK_MAX, (past + T) // 4)
    n_phys = cache_sel.shape[0]
    nb = past // CMP_BLOCK
    pt_flat = page_table.reshape(-1)
    flat = lambda c: jnp.moveaxis(c, 1, -1).reshape(c.shape[0], -1, c.shape[1])
    rows = lambda a: a.reshape(B, T * a.shape[2], -1)
    pad_new = lambda a: jnp.pad(a.reshape(B, T, -1), ((0, 0), (0, PAGE_SIZE - T), (0, 0)))
    bn = _sample_bias(rel_bias, 0, NSA_HEADS, T, past, W, nb)
    bd = _sample_bias(rel_bias, NSA_HEADS, DSA_HEADS, T, past, W, nb)

    kvc = cmp_pages(pt_flat, flat(cache_cmp), pool, B)
    o_c, o_w, selm = nsa_sample_pre(rows(q_n), kvc, flat(cache_win), pad_new(win), bn["cmp"], bn["win"], bn["new"],
                                    past, T)
    o_n = paged_attn("blocks", pt_flat, flat(cache_sel), rows(q_n), pad_new(sel), bn["far"], bn["last"], bn["new"],
                     (selm, o_c, o_w, g_n.reshape(B, T * NSA_HEADS, 3)), NSA_HEADS, T)

    qi, wi = rows(q_i), w_i.reshape(B, T * IDX_HEADS, 1)
    sp = idx_scores(pt_flat, qi, wi, flat(cache_idx))
    sn, thr, last = dsa_sample_pre(sp, qi, wi, pad_new(k_i), topk, T)
    o_d = paged_attn("thr", pt_flat, flat(cache_dkv), rows(q_d), pad_new(dkv), bd["far"], bd["last"], bd["new"],
                     (sp, sn, thr, last), DSA_HEADS, T)
    new_win = jnp.concatenate([cache_win[:, T:], win], axis=1)
    return o_n.reshape(B, T, -1), o_d.reshape(B, T, -1), new_win


def _project(h, w_q, w_kv, w_s, w_g):
    B, S, D = h.shape
    h2 = h.reshape(B * S, D)
    zq = matmul(h2, w_q, out_dtype=F32)
    zkv = matmul(h2, w_kv, out_dtype=F32)
    zs = matmul(h2, w_s, out_dtype=F32)
    zg = matmul(h2, w_g, out_dtype=F32)
    kvn = (B, S, 2, NSA_KV, NSA_DH)
    q_n = zq[:, :NSA_WIDTH].reshape(B, S, NSA_HEADS, NSA_DH)
    q_d = zq[:, NSA_WIDTH:NSA_WIDTH + DSA_WIDTH].reshape(B, S, DSA_HEADS, DSA_DH)
    q_i = zq[:, NSA_WIDTH + DSA_WIDTH:].reshape(B, S, IDX_HEADS, IDX_DH)
    cmp = zkv[:, :KV_W].reshape(kvn)
    sel = zkv[:, KV_W:2 * KV_W].reshape(kvn)
    win = zkv[:, 2 * KV_W:3 * KV_W].reshape(kvn)
    dkv = zkv[:, 3 * KV_W:].reshape(B, S, 2, DSA_KV, DSA_DH)
    k_i = zs[:, :IDX_DH].reshape(B, S, IDX_DH)
    w_i = zs[:, IDX_DH:IDX_DH + IDX_HEADS].reshape(B, S, IDX_HEADS)
    g_n = zs[:, IDX_DH + IDX_HEADS:].reshape(B, S, NSA_HEADS, 3)
    return (q_n, cmp, sel, win, g_n, q_d, dkv, q_i, k_i, w_i), zg


def kernel(x_prompt, x_sample, c_prompt, c_sample, cache_nsa_cmp, cache_nsa_sel, cache_nsa_win, cache_dsa_kv,
           cache_dsa_idx, page_table, rel_bias, w_mod, b_mod, g_pre_mix, g_post_mix, g_pre_ffn, g_post_ffn,
           w_in, cmp_pool, w_up_nsa, w_up_dsa, w_out, w_router, b_router, w_gu, b_gu, w_down, b_down):
    l = 0
    D = D_MODEL
    Bp, Sp, _ = x_prompt.shape
    Bs, Ss, _ = x_sample.shape

    c = jnp.concatenate([c_prompt, c_sample], 0)
    n_c = c.shape[0]
    c_pad = jnp.pad(jax.nn.silu(c), ((0, -n_c % 16), (0, 0))).astype(BF16)
    mod = matmul(c_pad, w_mod[l], b_mod[l], tn=1536)[:n_c]
    mod_p, mod_s = mod[:Bp], mod[Bp:]

    wi = w_in[l]
    seg = lambda k: wi[:, _OFF[k]:_OFF[k + 1]]
    w_q = jnp.concatenate([seg(0), seg(5), seg(7)], 1).astype(BF16)
    w_kv = jnp.concatenate([seg(1), seg(2), seg(3), seg(6)], 1).astype(BF16)
    w_s = jnp.concatenate([seg(8), seg(9), seg(4)], 1).astype(BF16)
    w_g = seg(10).astype(BF16)
    wun, wud, wo = w_up_nsa[l].astype(BF16), w_up_dsa[l].astype(BF16), w_out[l].astype(BF16)

    def mix_front(x, mod_g):
        sh1, sc1 = mod_g[:, :D], mod_g[:, D:2 * D]
        h = norm_mod(x, g_pre_mix[l], sc1, sh1)
        return _project(h, w_q, w_kv, w_s, w_g)

    def mix_back(x, mod_g, o_n, o_d, zg):
        B, S, _ = x.shape
        ga1, sh2, sc2 = mod_g[:, 2 * D:3 * D], mod_g[:, 3 * D:4 * D], mod_g[:, 4 * D:5 * D]
        u = merge_up(o_n.reshape(B * S, -1).astype(BF16), o_d.reshape(B * S, -1).astype(BF16), zg, wun, wud)
        return out_proj(u, x, wo, g_post_mix[l], g_pre_ffn[l], ga1, sc2, sh2, w_router[l], b_router[l])

    parts_p, zg_p = mix_front(x_prompt, mod_p)
    o_n, o_d = mixer_prompt(rel_bias, cmp_pool[l], *parts_p)
    x1_p, h_p, lg_p = mix_back(x_prompt, mod_p, o_n, o_d, zg_p)

    parts_s, zg_s = mix_front(x_sample, mod_s)
    o_n, o_d, new_win = mixer_sample(cache_nsa_cmp[l], cache_nsa_sel[l], cache_nsa_win[l], cache_dsa_kv[l],
                                     cache_dsa_idx[l], page_table, rel_bias, cmp_pool[l], *parts_s)
    x1_s, h_s, lg_s = mix_back(x_sample, mod_s, o_n, o_d, zg_s)

    Tp, Ts = Bp * Sp, Bs * Ss
    h_all = jnp.concatenate([h_p.reshape(Tp, D), h_s.reshape(Ts, D)], 0)
    logits = jnp.concatenate([lg_p.reshape(Tp, N_EXPERTS), lg_s.reshape(Ts, N_EXPERTS)], 0)
    probs, row_tok, blk_e, n_used, dest_of = moe_route(logits)
    out_rows = moe_experts(h_all[row_tok], blk_e, n_used, w_gu[l], b_gu[l], w_down[l], b_down[l])
    picked = lambda lo, hi: [out_rows[dest_of[lo:hi, k]] for k in range(TOP_K)]
    y_p = moe_combine(picked(0, Tp), probs[:Tp], x1_p, g_post_ffn[l], mod_p[:, 5 * D:])
    y_s = moe_combine(picked(Tp, Tp + Ts), probs[Tp:], x1_s, g_post_ffn[l], mod_s[:, 5 * D:])

    st = lambda a: a[None]
    q_n, cmp_p, sel_p, win_p, g_n, q_d, dkv_p, q_i, ki_p, w_i = parts_p
    q_n, cmp_s, sel_s, win_s, g_n, q_d, dkv_s, q_i, ki_s, w_i = parts_s
    keep = min(WINDOW, Sp)
    return (y_p, y_s, st(cmp_p), st(sel_p), st(win_p[:, Sp - keep:]), st(dkv_p), st(ki_p),
            st(cmp_s), st(sel_s), st(new_win), st(dkv_s), st(ki_s))
```

```python
import functools
import math

import jax
import jax.numpy as jnp
import numpy as np
from jax import lax
from jax.experimental import pallas as pl
from jax.experimental.pallas import tpu as pltpu

F32 = jnp.float32
BF16 = jnp.bfloat16

D_MODEL = 2048
NSA_HEADS = 16
NSA_KV = 4
NSA_DH = 64
CMP_BLOCK = 64
N_SEL = 16
WINDOW = 512
DSA_HEADS = 16
DSA_KV = 4
DSA_DH = 64
IDX_HEADS = 16
IDX_DH = 64
DSA_TOPK_MAX = 256
NUM_BUCKETS = 32
REL_MAX_DIST = 128
N_EXPERTS = 32
TOP_K = 4
D_FF = 2048
SWIGLU_LIMIT = 7.0
SWIGLU_ALPHA = 1.702
PAGE_SIZE = 128
Q_BLOCK = 128
RMS_EPS = 1e-6
NEG = -1e30
NSA_WIDTH = NSA_HEADS * NSA_DH
DSA_WIDTH = DSA_HEADS * DSA_DH
KV_W = 2 * NSA_KV * NSA_DH

_SPLIT = (NSA_WIDTH, KV_W, KV_W, KV_W, 3 * NSA_HEADS, DSA_WIDTH, 2 * DSA_KV * DSA_DH,
          IDX_HEADS * IDX_DH, IDX_DH, IDX_HEADS, 2 * D_MODEL)
_OFF = tuple(int(v) for v in np.cumsum((0,) + _SPLIT))

MOE_TM = 512
VMEM_LIMIT = 48 * 1024 * 1024


def _cparams(sem):
    return pltpu.CompilerParams(dimension_semantics=sem, vmem_limit_bytes=VMEM_LIMIT)


def _norm_mod_kernel(x_ref, g_ref, sc_ref, sh_ref, o_ref):
    x = x_ref[0]
    y = x * lax.rsqrt(jnp.mean(x * x, axis=-1, keepdims=True) + RMS_EPS) * g_ref[...]
    o_ref[0] = (y * (1.0 + sc_ref[0]) + sh_ref[0]).astype(o_ref.dtype)


def norm_mod(x, g, scale, shift):
    B, S, D = x.shape
    ts = min(S, 512)
    row = pl.BlockSpec((1, ts, D), lambda b, s: (b, s, 0))
    per_b = pl.BlockSpec((1, 1, D), lambda b, s: (b, 0, 0))
    return pl.pallas_call(
        _norm_mod_kernel,
        out_shape=jax.ShapeDtypeStruct((B, S, D), BF16),
        grid=(B, S // ts),
        in_specs=[row, pl.BlockSpec((1, D), lambda b, s: (0, 0)), per_b, per_b],
        out_specs=row,
        compiler_params=_cparams(("parallel", "parallel")),
        name="norm_mod",
    )(x, g.reshape(1, D), scale.reshape(B, 1, D), shift.reshape(B, 1, D))


def _matmul_kernel(a_ref, w_ref, b_ref, o_ref):
    acc = jnp.dot(a_ref[...], w_ref[...].astype(BF16), preferred_element_type=F32)
    o_ref[...] = (acc + b_ref[...]).astype(o_ref.dtype)


def matmul(a, w, bias=None, out_dtype=F32, tm=512, tn=512):
    M, K = a.shape
    N = w.shape[1]
    tm, tn = min(tm, M), min(tn, N)
    assert M % tm == 0 and N % tn == 0
    if bias is None:
        bias = jnp.zeros((N,), F32)
    return pl.pallas_call(
        _matmul_kernel,
        out_shape=jax.ShapeDtypeStruct((M, N), out_dtype),
        grid=(M // tm, N // tn),
        in_specs=[pl.BlockSpec((tm, K), lambda i, j: (i, 0)),
                  pl.BlockSpec((K, tn), lambda i, j: (0, j)),
                  pl.BlockSpec((1, tn), lambda i, j: (0, j))],
        out_specs=pl.BlockSpec((tm, tn), lambda i, j: (i, j)),
        compiler_params=_cparams(("parallel", "parallel")),
        name="matmul",
    )(a, w, bias.reshape(1, N))


def _merge_kernel(on_ref, od_ref, ga_ref, gb_ref, wn_ref, wd_ref, o_ref):
    a = jnp.dot(on_ref[...], wn_ref[...], preferred_element_type=F32)
    b = jnp.dot(od_ref[...], wd_ref[...], preferred_element_type=F32)
    o_ref[...] = (jax.nn.sigmoid(ga_ref[...]) * a + jax.nn.sigmoid(gb_ref[...]) * b).astype(o_ref.dtype)


def merge_up(o_n, o_d, zg, w_up_nsa, w_up_dsa, tm=512, tn=512):
    M = o_n.shape[0]
    D = w_up_nsa.shape[1]
    tm = min(tm, M)
    nj = D // tn
    return pl.pallas_call(
        _merge_kernel,
        out_shape=jax.ShapeDtypeStruct((M, D), BF16),
        grid=(M // tm, nj),
        in_specs=[pl.BlockSpec((tm, o_n.shape[1]), lambda i, j: (i, 0)),
                  pl.BlockSpec((tm, o_d.shape[1]), lambda i, j: (i, 0)),
                  pl.BlockSpec((tm, tn), lambda i, j: (i, j)),
                  pl.BlockSpec((tm, tn), lambda i, j: (i, j + nj)),
                  pl.BlockSpec((o_n.shape[1], tn), lambda i, j: (0, j)),
                  pl.BlockSpec((o_d.shape[1], tn), lambda i, j: (0, j))],
        out_specs=pl.BlockSpec((tm, tn), lambda i, j: (i, j)),
        compiler_params=_cparams(("parallel", "parallel")),
        name="merge_up",
    )(o_n, o_d, zg, zg, w_up_nsa, w_up_dsa)


def _rms(x, g):
    return x * lax.rsqrt(jnp.mean(x * x, axis=-1, keepdims=True) + RMS_EPS) * g


def _out_proj_kernel(u_ref, x_ref, w_ref, gpost_ref, gpre_ref, ga_ref, sc_ref, sh_ref, wr_ref, br_ref,
                     x1_ref, h_ref, lg_ref):
    m = jnp.dot(u_ref[0], w_ref[...], preferred_element_type=F32)
    x1 = x_ref[0] + ga_ref[0] * _rms(m, gpost_ref[...])
    x1_ref[0] = x1
    h = _rms(x1, gpre_ref[...]) * (1.0 + sc_ref[0]) + sh_ref[0]
    h_ref[0] = h.astype(h_ref.dtype)
    lg_ref[0] = jnp.dot(h, wr_ref[...], preferred_element_type=F32,
                        precision=lax.Precision.HIGHEST) + br_ref[...]


def out_proj(u, x, w_out, g_post, g_pre, gate, scale, shift, w_router, b_router):
    B, S, D = x.shape
    ts = min(S, 256)
    E = w_router.shape[1]
    row = lambda d: pl.BlockSpec((1, ts, d), lambda b, s: (b, s, 0))
    per_b = pl.BlockSpec((1, 1, D), lambda b, s: (b, 0, 0))
    vec = lambda d: pl.BlockSpec((1, d), lambda b, s: (0, 0))
    return pl.pallas_call(
        _out_proj_kernel,
        out_shape=(jax.ShapeDtypeStruct((B, S, D), F32), jax.ShapeDtypeStruct((B, S, D), BF16),
                   jax.ShapeDtypeStruct((B, S, E), F32)),
        grid=(B, S // ts),
        in_specs=[row(D), row(D), pl.BlockSpec((D, D), lambda b, s: (0, 0)), vec(D), vec(D),
                  per_b, per_b, per_b, pl.BlockSpec((D, E), lambda b, s: (0, 0)), vec(E)],
        out_specs=(row(D), row(D), row(E)),
        compiler_params=_cparams(("parallel", "parallel")),
        name="out_proj",
    )(u.reshape(B, S, D), x, w_out, g_post.reshape(1, D), g_pre.reshape(1, D), gate.reshape(B, 1, D),
      scale.reshape(B, 1, D), shift.reshape(B, 1, D), w_router, b_router.reshape(1, E))


def _expert_changed(be_ref, b):
    return jnp.logical_or(b == 0, be_ref[b] != be_ref[jnp.maximum(b - 1, 0)])


def _moe_up_kernel(be_ref, nu_ref, x_ref, wg_ref, wl_ref, bg_ref, bl_ref, o_ref, wg_s, wl_s):
    b = pl.program_id(1)

    @pl.when(_expert_changed(be_ref, b))
    def _():
        wg_s[...] = wg_ref[0].astype(BF16)
        wl_s[...] = wl_ref[0].astype(BF16)

    @pl.when(b < nu_ref[0])
    def _():
        x = x_ref[...]
        g = jnp.dot(x, wg_s[...], preferred_element_type=F32) + bg_ref[0]
        lin = jnp.dot(x, wl_s[...], preferred_element_type=F32) + bl_ref[0]
        g = jnp.minimum(g, SWIGLU_LIMIT)
        lin = jnp.clip(lin, -SWIGLU_LIMIT, SWIGLU_LIMIT)
        o_ref[...] = (g * jax.nn.sigmoid(SWIGLU_ALPHA * g) * (lin + 1.0)).astype(o_ref.dtype)

    @pl.when(b >= nu_ref[0])
    def _():
        o_ref[...] = jnp.zeros_like(o_ref)


def _moe_down_kernel(be_ref, nu_ref, a_ref, w_ref, bias_ref, o_ref, w_s):
    b = pl.program_id(1)

    @pl.when(_expert_changed(be_ref, b))
    def _():
        w_s[...] = w_ref[0].astype(BF16)

    @pl.when(b < nu_ref[0])
    def _():
        o_ref[...] = jnp.dot(a_ref[...], w_s[...], preferred_element_type=F32) + bias_ref[0]

    @pl.when(b >= nu_ref[0])
    def _():
        o_ref[...] = jnp.zeros_like(o_ref)


def moe_experts(xs, blk_e, n_used, w_gu, b_gu, w_down, b_down, tf=512, tn=1024):
    R, D = xs.shape
    nb = R // MOE_TM
    E = w_gu.shape[0]
    nf = D_FF // tf
    act = pl.pallas_call(
        _moe_up_kernel,
        out_shape=jax.ShapeDtypeStruct((R, D_FF), BF16),
        grid_spec=pltpu.PrefetchScalarGridSpec(
            num_scalar_prefetch=2,
            grid=(nf, nb),
            in_specs=[pl.BlockSpec((MOE_TM, D), lambda f, b, be, nu: (b, 0)),
                      pl.BlockSpec((1, D, tf), lambda f, b, be, nu: (be[b], 0, f)),
                      pl.BlockSpec((1, D, tf), lambda f, b, be, nu: (be[b], 0, f + nf)),
                      pl.BlockSpec((1, 1, tf), lambda f, b, be, nu: (be[b], 0, f)),
                      pl.BlockSpec((1, 1, tf), lambda f, b, be, nu: (be[b], 0, f + nf))],
            out_specs=pl.BlockSpec((MOE_TM, tf), lambda f, b, be, nu: (b, f)),
            scratch_shapes=[pltpu.VMEM((D, tf), BF16), pltpu.VMEM((D, tf), BF16)]),
        compiler_params=_cparams(("arbitrary", "arbitrary")),
        name="moe_up",
    )(blk_e, n_used, xs, w_gu, w_gu, b_gu.reshape(E, 1, 2 * D_FF), b_gu.reshape(E, 1, 2 * D_FF))
    nn = D // tn
    return pl.pallas_call(
        _moe_down_kernel,
        out_shape=jax.ShapeDtypeStruct((R, D), F32),
        grid_spec=pltpu.PrefetchScalarGridSpec(
            num_scalar_prefetch=2,
            grid=(nn, nb),
            in_specs=[pl.BlockSpec((MOE_TM, D_FF), lambda j, b, be, nu: (b, 0)),
                      pl.BlockSpec((1, D_FF, tn), lambda j, b, be, nu: (be[b], 0, j)),
                      pl.BlockSpec((1, 1, tn), lambda j, b, be, nu: (be[b], 0, j))],
            out_specs=pl.BlockSpec((MOE_TM, tn), lambda j, b, be, nu: (b, j)),
            scratch_shapes=[pltpu.VMEM((D_FF, tn), BF16)]),
        compiler_params=_cparams(("arbitrary", "arbitrary")),
        name="moe_down",
    )(blk_e, n_used, act, w_down, b_down.reshape(E, 1, D))


def _combine_kernel(p_ref, x_ref, g_ref, ga_ref, *refs):
    o_ref = refs[-1]
    p = p_ref[0]
    f = refs[0][0] * p[:, 0:1]
    for k in range(1, TOP_K):
        f = f + refs[k][0] * p[:, k:k + 1]
    o_ref[0] = x_ref[0] + ga_ref[0] * _rms(f, g_ref[...])


def moe_combine(rows, probs, x, g_post, gate):
    B, S, D = x.shape
    ts = min(S, 256)
    row = pl.BlockSpec((1, ts, D), lambda b, s: (b, s, 0))
    return pl.pallas_call(
        _combine_kernel,
        out_shape=jax.ShapeDtypeStruct((B, S, D), F32),
        grid=(B, S // ts),
        in_specs=[pl.BlockSpec((1, ts, TOP_K), lambda b, s: (b, s, 0)), row,
                  pl.BlockSpec((1, D), lambda b, s: (0, 0)),
                  pl.BlockSpec((1, 1, D), lambda b, s: (b, 0, 0))] + [row] * TOP_K,
        out_specs=row,
        compiler_params=_cparams(("parallel", "parallel")),
        name="moe_combine",
    )(probs.reshape(B, S, TOP_K), x, g_post.reshape(1, D), gate.reshape(B, 1, D),
      *[r.reshape(B, S, D) for r in rows])


def moe_route(logits):
    T = logits.shape[0]
    top_v, top_e = lax.top_k(logits, TOP_K)
    probs = jax.nn.softmax(top_v, axis=-1)
    flat_e = top_e.reshape(-1)
    n_assign = flat_e.shape[0]
    order = jnp.argsort(flat_e)
    e_sorted = flat_e[order]
    counts = jnp.bincount(flat_e, length=N_EXPERTS)
    padded = (counts + MOE_TM - 1) // MOE_TM * MOE_TM
    pad_end = jnp.cumsum(padded)
    start = jnp.cumsum(counts) - counts
    dest = (pad_end - padded)[e_sorted] + jnp.arange(n_assign) - start[e_sorted]
    n_blocks = -(-n_assign // MOE_TM) + N_EXPERTS
    row_tok = jnp.zeros((n_blocks * MOE_TM,), jnp.int32).at[dest].set((order // TOP_K).astype(jnp.int32))
    blk_e = jnp.minimum(jnp.searchsorted(pad_end, jnp.arange(n_blocks) * MOE_TM, side='right'),
                        N_EXPERTS - 1).astype(jnp.int32)
    dest_of = jnp.zeros((n_assign,), jnp.int32).at[order].set(dest.astype(jnp.int32))
    n_used = (pad_end[-1] // MOE_TM).astype(jnp.int32).reshape(1)
    return probs, row_tok, blk_e, n_used, dest_of.reshape(T, TOP_K)


def _rel_bucket(dist):
    n = np.maximum(np.asarray(dist), 0)
    exact = NUM_BUCKETS // 2
    log_b = exact + (np.log(np.maximum(n, 1).astype(np.float32) / np.float32(exact))
                     / np.float32(math.log(REL_MAX_DIST / exact)) * (NUM_BUCKETS - exact)).astype(np.int32)
    return np.where(n < exact, n, np.minimum(log_b, NUM_BUCKETS - 1)).astype(np.int32)


def _bias_rows(bkt, table):
    onehot = jax.nn.one_hot(jnp.asarray(bkt), NUM_BUCKETS, dtype=F32)
    return jnp.tensordot(onehot, table, axes=1, precision=lax.Precision.HIGHEST)


TQ = Q_BLOCK
HPG = NSA_HEADS // NSA_KV
INT_MIN = -2 ** 31


def _flash_step(q2, k, v, bias, mask, scale, carry):
    m, l, acc = carry
    nk = k.shape[0]
    s = lax.dot_general(q2, k, (((1,), (1,)), ((), ())), preferred_element_type=F32)
    s = s.reshape(HPG, TQ, nk) * scale + bias
    s = jnp.where(mask[None], s, NEG)
    m_new = jnp.maximum(m, jnp.max(s, axis=-1, keepdims=True))
    p = jnp.where(mask[None], jnp.exp(s - m_new), 0.0)
    alpha = jnp.exp(m - m_new)
    l = alpha * l + jnp.sum(p, axis=-1, keepdims=True)
    pv = jnp.dot(p.reshape(HPG * TQ, nk).astype(BF16), v, preferred_element_type=F32)
    acc = alpha * acc + pv.reshape(HPG, TQ, v.shape[1])
    return m_new, l, acc


def _flash_init(dh):
    return (jnp.full((HPG, TQ, 1), NEG, F32), jnp.zeros((HPG, TQ, 1), F32), jnp.zeros((HPG, TQ, dh), F32))


def _flash_out(carry):
    m, l, acc = carry
    return acc / jnp.maximum(l, 1e-30)


def _causal_blocks(i, q2, k_ref, v_ref, mask_ref, tiles_ref, far_ref, scale):
    dh = k_ref.shape[-1]
    a = lax.broadcasted_iota(jnp.int32, (TQ, TQ), 0)
    b = lax.broadcasted_iota(jnp.int32, (TQ, TQ), 1)

    def blk(ref, jb):
        return ref[pl.ds(pl.multiple_of(jb * TQ, TQ), TQ), :]

    first = i == 0
    jp = jnp.maximum(i - 1, 0)
    causal = jnp.where(a >= b, 1.0, 0.0)
    mask_a = jnp.where(first, mask_ref[jp] * causal, mask_ref[jp])
    mask_b = jnp.where(first, 0.0, mask_ref[jp + 1] * causal)
    bias = jnp.concatenate([jnp.where(first, tiles_ref[:, 0], tiles_ref[:, 1]), tiles_ref[:, 0]], axis=-1)
    near = pl.ds(pl.multiple_of(jp * TQ, TQ), 2 * TQ)
    carry = _flash_step(q2, k_ref[near, :], v_ref[near, :], bias,
                        jnp.concatenate([mask_a, mask_b], axis=1) > 0.5, scale, _flash_init(dh))
    far = far_ref[...][:, :, :1]
    n_far = jnp.maximum(i - 1, 0)

    def pair(jp2, c):
        rows = pl.ds(pl.multiple_of(jp2 * (2 * TQ), 2 * TQ), 2 * TQ)
        mask = jnp.concatenate([mask_ref[2 * jp2], mask_ref[2 * jp2 + 1]], axis=1) > 0.5
        return _flash_step(q2, k_ref[rows, :], v_ref[rows, :], far, mask, scale, c)

    def single(_, c):
        jb = n_far - 1
        return _flash_step(q2, blk(k_ref, jb), blk(v_ref, jb), far, mask_ref[jb] > 0.5, scale, c)

    carry = lax.fori_loop(0, n_far // 2, pair, carry)
    return lax.fori_loop(0, n_far % 2, single, carry)


def _nsa_prompt_kernel(q_ref, kc_ref, vc_ref, ks_ref, vs_ref, kw_ref, vw_ref, pool_ref, bc_ref, tiles_ref,
                       far_ref, gate_ref, o_ref, kcs, vcs, exp_s, mask_s):
    i = pl.program_id(2)
    S = ks_ref.shape[3]
    nb = S // CMP_BLOCK
    scale = NSA_DH ** -0.5

    @pl.when(i == 0)
    def _():
        kr = kc_ref[0, 0, 0].reshape(nb, CMP_BLOCK, NSA_DH) * pool_ref[0][None]
        vr = vc_ref[0, 0, 0].reshape(nb, CMP_BLOCK, NSA_DH) * pool_ref[1][None]
        kcs[...] = jnp.sum(kr, axis=1).astype(BF16)
        vcs[...] = jnp.sum(vr, axis=1).astype(BF16)
        n_i = lax.broadcasted_iota(jnp.int32, (nb, S), 0)
        s_i = lax.broadcasted_iota(jnp.int32, (nb, S), 1)
        exp_s[...] = jnp.where(s_i // CMP_BLOCK == n_i, 1.0, 0.0).astype(BF16)

    q2 = q_ref[0].reshape(HPG * TQ, NSA_DH)
    t0 = i * TQ

    s_c = lax.dot_general(q2, kcs[...], (((1,), (1,)), ((), ())), preferred_element_type=F32)
    s_c = s_c.reshape(HPG, TQ, nb) * scale + bc_ref[...]
    t_i = t0 + lax.broadcasted_iota(jnp.int32, (TQ, nb), 0)
    n_i = lax.broadcasted_iota(jnp.int32, (TQ, nb), 1)
    vis = t_i >= n_i * CMP_BLOCK + (CMP_BLOCK - 1)
    s_c = jnp.where(vis[None], s_c, NEG)
    e_c = jnp.where(vis[None], jnp.exp(s_c - jnp.max(s_c, axis=-1, keepdims=True)), 0.0)
    p_c = e_c / jnp.maximum(jnp.sum(e_c, axis=-1, keepdims=True), 1e-30)
    o_c = jnp.dot(p_c.reshape(HPG * TQ, nb).astype(BF16), vcs[...], preferred_element_type=F32)

    nt = (((1,), (1,)), ((), ()))
    eye = lambda n: jnp.where(lax.broadcasted_iota(jnp.int32, (n, n), 0) == lax.broadcasted_iota(jnp.int32, (n, n), 1),
                              1.0, 0.0)
    imp_t = lax.dot_general(eye(nb), jnp.sum(p_c, axis=0), nt, preferred_element_type=F32,
                            precision=lax.Precision.HIGHEST)
    n_t = lax.broadcasted_iota(jnp.int32, (nb, TQ), 0)
    t_t = t0 + lax.broadcasted_iota(jnp.int32, (nb, TQ), 1)
    cur = n_t == t_t // CMP_BLOCK
    work = jnp.where(jnp.logical_and(t_t >= n_t * CMP_BLOCK + (CMP_BLOCK - 1), jnp.logical_not(cur)), imp_t, -1.0)
    sel_t = jnp.where(cur, 1.0, 0.0)
    n_f = n_t.astype(F32)
    for _ in range(min(N_SEL - 1, nb)):
        mx = jnp.max(work, axis=0, keepdims=True)
        first = jnp.min(jnp.where(work == mx, n_f, float(nb)), axis=0, keepdims=True)
        pick = n_f == first
        sel_t = jnp.where(jnp.logical_and(pick, mx >= 0.0), 1.0, sel_t)
        work = jnp.where(pick, -2.0, work)
    sel = lax.dot_general(eye(TQ).astype(BF16), sel_t.astype(BF16), nt, preferred_element_type=F32)
    selx = jnp.dot(sel.astype(BF16), exp_s[...], preferred_element_type=F32)
    for jb in range(S // TQ):
        mask_s[jb] = selx[:, jb * TQ:(jb + 1) * TQ]

    o_s = _flash_out(_causal_blocks(i, q2, ks_ref.at[0, 0, 0], vs_ref.at[0, 0, 0], mask_s, tiles_ref, far_ref,
                                    scale))

    nw = WINDOW + TQ
    rows = pl.ds(pl.multiple_of(t0, TQ), nw)
    s_w = lax.dot_general(q2, kw_ref[0, 0, 0, rows, :], (((1,), (1,)), ((), ())), preferred_element_type=F32)
    far_w = jnp.broadcast_to(far_ref[...][:, :, :1], (HPG, TQ, nw - 2 * TQ))
    s_w = s_w.reshape(HPG, TQ, nw) * scale + jnp.concatenate([far_w, tiles_ref[:, 1], tiles_ref[:, 0]], axis=-1)
    c_w = lax.broadcasted_iota(jnp.int32, (TQ, nw), 1)
    d_w = lax.broadcasted_iota(jnp.int32, (TQ, nw), 0) + WINDOW - c_w
    in_w = jnp.logical_and(jnp.logical_and(d_w >= 0, d_w < WINDOW), t0 - WINDOW + c_w >= 0)
    s_w = jnp.where(in_w[None], s_w, NEG)
    e_w = jnp.where(in_w[None], jnp.exp(s_w - jnp.max(s_w, axis=-1, keepdims=True)), 0.0)
    o_w = jnp.dot(e_w.reshape(HPG * TQ, nw).astype(BF16), vw_ref[0, 0, 0, rows, :], preferred_element_type=F32)
    o_w = o_w.reshape(HPG, TQ, NSA_DH) / jnp.maximum(jnp.sum(e_w, axis=-1, keepdims=True), 1e-30)

    g = jax.nn.sigmoid(gate_ref[0])
    o = g[..., 0:1] * o_c.reshape(HPG, TQ, NSA_DH) + g[..., 1:2] * o_s + g[..., 2:3] * o_w
    for j in range(HPG):
        o_ref[0, :, j * NSA_DH:(j + 1) * NSA_DH] = o[j].astype(o_ref.dtype)


def _bias_tables(rel_bias, S):
    assert 2 * TQ - (TQ - 1) >= REL_MAX_DIST
    nb = S // CMP_BLOCK
    t = np.arange(S)[:, None]
    bc = _bias_rows(_rel_bucket(t - (np.arange(nb)[None, :] * CMP_BLOCK + CMP_BLOCK - 1)), rel_bias)
    d = np.arange(2)[:, None, None] * TQ + np.arange(TQ)[None, :, None] - np.arange(TQ)[None, None, :]
    tiles = _bias_rows(_rel_bucket(d), rel_bias)
    far = jnp.broadcast_to(rel_bias[NUM_BUCKETS - 1][:, None, None], (rel_bias.shape[1], 1, TQ))
    return bc.transpose(2, 0, 1), tiles.transpose(3, 0, 1, 2), far


def nsa_prompt(q_n, cmp_t, sel_t, win_t, pool, g_n, bias_c, tiles, far):
    B, H, S, dh = q_n.shape
    assert S >= 2 * TQ
    G = NSA_KV
    nb = S // CMP_BLOCK
    assert win_t.shape[3] == S + WINDOW and WINDOW >= 2 * TQ
    kv = lambda c, n=S: pl.BlockSpec((1, 1, 1, n, dh), lambda b, g, i: (b, c, g, 0, 0))
    return pl.pallas_call(
        _nsa_prompt_kernel,
        out_shape=jax.ShapeDtypeStruct((B, S, H * dh), BF16),
        grid=(B, G, S // TQ),
        in_specs=[pl.BlockSpec((1, HPG, TQ, dh), lambda b, g, i: (b, g, i, 0)),
                  kv(0), kv(1), kv(0), kv(1), kv(0, S + WINDOW), kv(1, S + WINDOW),
                  pl.BlockSpec((2, CMP_BLOCK, dh), lambda b, g, i: (0, 0, 0)),
                  pl.BlockSpec((HPG, TQ, nb), lambda b, g, i: (g, i, 0)),
                  pl.BlockSpec((HPG, 2, TQ, TQ), lambda b, g, i: (g, 0, 0, 0)),
                  pl.BlockSpec((HPG, 1, TQ), lambda b, g, i: (g, 0, 0)),
                  pl.BlockSpec((1, HPG, TQ, 3), lambda b, g, i: (b, g, i, 0))],
        out_specs=pl.BlockSpec((1, TQ, HPG * dh), lambda b, g, i: (b, i, g)),
        scratch_shapes=[pltpu.VMEM((nb, dh), BF16), pltpu.VMEM((nb, dh), BF16), pltpu.VMEM((nb, S), BF16),
                        pltpu.VMEM((S // TQ, TQ, TQ), F32)],
        compiler_params=_cparams(("parallel", "parallel", "arbitrary")),
        name="nsa_prompt",
    )(q_n, cmp_t, cmp_t, sel_t, sel_t, win_t, win_t,
      jnp.broadcast_to(pool[:, :, None], (2, CMP_BLOCK, dh)), bias_c, tiles, far, g_n)


def _dsa_prompt_kernel(qi_ref, ki_ref, wi_ref, qd_ref, kd_ref, vd_ref, tiles_ref, far_ref, o_ref,
                       key_s, mask_s, *, topk):
    i = pl.program_id(1)
    S = ki_ref.shape[1]
    nkb = S // TQ
    t0 = i * TQ
    a = lax.broadcasted_iota(jnp.int32, (TQ, TQ), 0)
    b = lax.broadcasted_iota(jnp.int32, (TQ, TQ), 1)

    w = wi_ref[0] * (IDX_DH ** -0.5 * IDX_HEADS ** -0.5)
    w_cols = [jnp.broadcast_to(w[:, h:h + 1], (TQ, TQ)) for h in range(IDX_HEADS)]
    for jb in range(nkb):
        @pl.when(jb <= i)
        def _():
            kk = ki_ref[0, jb * TQ:(jb + 1) * TQ, :]
            acc = jnp.zeros((TQ, TQ), F32)
            for h in range(IDX_HEADS):
                s = lax.dot_general(qi_ref[0, h], kk, (((1,), (1,)), ((), ())), preferred_element_type=F32)
                acc = acc + jnp.maximum(s, 0.0) * w_cols[h]
            score = jnp.where(jb * TQ + b <= t0 + a, acc, NEG) + 0.0
            bits = pltpu.bitcast(score, jnp.int32)
            key_s[jb] = jnp.where(bits < 0, bits ^ 0x7FFFFFFF, bits)

        @pl.when(jb > i)
        def _():
            bits = pltpu.bitcast(jnp.full((TQ, TQ), NEG, F32), jnp.int32)
            key_s[jb] = bits ^ 0x7FFFFFFF

    half = nkb // 2
    if half * TQ >= topk:
        thr, last = lax.cond(i < half, lambda: _kth_threshold(key_s[:half], topk),
                             lambda: _kth_threshold(key_s[...], topk))
    else:
        thr, last = _kth_threshold(key_s[...], topk)
    keys = key_s[...]
    pos = (lax.broadcasted_iota(jnp.int32, (nkb, TQ, TQ), 0) * TQ
           + lax.broadcasted_iota(jnp.int32, (nkb, TQ, TQ), 2))
    chosen = jnp.logical_or(keys > thr, jnp.logical_and(keys == thr, pos <= last))
    mask_s[...] = jnp.where(chosen, 1.0, 0.0)

    scale = DSA_DH ** -0.5
    for g in range(DSA_KV):
        q2 = qd_ref[0, g * HPG:(g + 1) * HPG].reshape(HPG * TQ, DSA_DH)
        o = _flash_out(_causal_blocks(i, q2, kd_ref.at[0, 0, g], vd_ref.at[0, 0, g], mask_s,
                                      tiles_ref.at[g * HPG:(g + 1) * HPG], far_ref.at[g * HPG:(g + 1) * HPG],
                                      scale))
        for j in range(HPG):
            h = g * HPG + j
            o_ref[0, :, h * DSA_DH:(h + 1) * DSA_DH] = o[j].astype(o_ref.dtype)


def dsa_prompt(q_i, k_i, w_i, q_d, dkv_t, tiles, far):
    B, H, S, dh = q_d.shape
    assert S >= 2 * TQ
    G = DSA_KV
    topk = min(DSA_TOPK_MAX, S // 4)
    heads = pl.BlockSpec((1, H, TQ, dh), lambda b, i: (b, 0, i, 0))
    kv = lambda c: pl.BlockSpec((1, 1, G, S, dh), lambda b, i: (b, c, 0, 0, 0))
    return pl.pallas_call(
        functools.partial(_dsa_prompt_kernel, topk=topk),
        out_shape=jax.ShapeDtypeStruct((B, S, H * dh), BF16),
        grid=(B, S // TQ),
        in_specs=[heads,
                  pl.BlockSpec((1, S, k_i.shape[2]), lambda b, i: (b, 0, 0)),
                  pl.BlockSpec((1, TQ, H), lambda b, i: (b, i, 0)),
                  heads, kv(0), kv(1),
                  pl.BlockSpec((H, 2, TQ, TQ), lambda b, i: (0, 0, 0, 0)),
                  pl.BlockSpec((H, 1, TQ), lambda b, i: (0, 0, 0))],
        out_specs=pl.BlockSpec((1, TQ, H * dh), lambda b, i: (b, i, 0)),
        scratch_shapes=[pltpu.VMEM((S // TQ, TQ, TQ), jnp.int32), pltpu.VMEM((S // TQ, TQ, TQ), F32)],
        compiler_params=_cparams(("parallel", "arbitrary")),
        name="dsa_prompt",
    )(q_i, k_i, w_i, q_d, dkv_t, dkv_t, tiles, far)


def mixer_prompt(rel_bias, pool, q_n, cmp, sel, win, g_n, q_d, dkv, q_i, k_i, w_i):
    B, S = q_n.shape[:2]
    heads_first = lambda a: a.transpose(0, 2, 1, 3)
    kv_t = lambda a, dt: a.transpose(0, 2, 3, 1, 4).astype(dt)
    bias_c, tiles, far = _bias_tables(rel_bias, S)
    win_t = jnp.pad(kv_t(win, BF16), ((0, 0), (0, 0), (0, 0), (WINDOW, 0), (0, 0)))
    o_n = nsa_prompt(heads_first(q_n).astype(BF16), kv_t(cmp, F32), kv_t(sel, BF16), win_t, pool,
                     heads_first(g_n), bias_c[:NSA_HEADS], tiles[:NSA_HEADS], far[:NSA_HEADS])
    o_d = dsa_prompt(heads_first(q_i).astype(BF16), k_i.astype(BF16), w_i, heads_first(q_d).astype(BF16),
                     kv_t(dkv, BF16), tiles[NSA_HEADS:], far[NSA_HEADS:])
    return o_n, o_d


KVH = NSA_KV * NSA_DH
PAGES_PER_STEP = 16


def _pad_groups(q, n_heads):
    R, dh = q.shape
    G = n_heads // HPG
    col_g = lax.broadcasted_iota(jnp.int32, (R, G * dh), 1) // dh
    row_g = (lax.broadcasted_iota(jnp.int32, (R, G * dh), 0) % n_heads) // HPG
    return jnp.where(col_g == row_g, jnp.concatenate([q] * G, axis=1), 0.0).astype(BF16)


def _own_group(o_pad, n_heads):
    R = o_pad.shape[0]
    G = n_heads // HPG
    dh = o_pad.shape[1] // G
    row_g = (lax.broadcasted_iota(jnp.int32, (R, dh), 0) % n_heads) // HPG
    out = o_pad[:, :dh]
    for g in range(1, G):
        out = jnp.where(row_g == g, o_pad[:, g * dh:(g + 1) * dh], out)
    return out


def _order_key(x):
    bits = pltpu.bitcast(x, jnp.int32)
    return jnp.where(bits < 0, bits ^ 0x7FFFFFFF, bits)


def _kth_threshold(keys, topk):
    n, R, L = keys.shape

    def count(pred):
        c = jnp.sum(jnp.where(pred, 1.0, 0.0), axis=0)
        return jnp.sum(c, axis=-1, keepdims=True)[None]

    def bit_body(i, pat):
        cand = pat | lax.shift_left(jnp.int32(1), 31 - i)
        return jnp.where(count(keys >= (cand ^ INT_MIN)) >= topk, cand, pat)

    thr = lax.fori_loop(0, 32, bit_body, jnp.zeros((1, R, 1), jnp.int32)) ^ INT_MIN
    need = topk - count(keys > thr)
    pos = lax.broadcasted_iota(jnp.int32, (n, R, L), 0) * L + lax.broadcasted_iota(jnp.int32, (n, R, L), 2)
    tie = keys == thr
    nbits = (n * L - 1).bit_length()

    def tie_body(i, last):
        cand = last | lax.shift_left(jnp.int32(1), nbits - 1 - i)
        return jnp.where(count(jnp.logical_and(tie, pos < cand)) < need, cand, last)

    crowded = jnp.max(jnp.where(count(tie) > need, 1.0, 0.0)) > 0.5
    last = lax.cond(crowded,
                    lambda: lax.fori_loop(0, nbits, tie_body, jnp.zeros((1, R, 1), jnp.int32)),
                    lambda: jnp.full((1, R, 1), n * L, jnp.int32))
    return thr, last


def _cmp_pages_kernel(pt_ref, pw_ref, *refs):
    o_ref = refs[-1]
    Wd, ncol = o_ref.shape[2], o_ref.shape[3]
    bpp = PAGE_SIZE // CMP_BLOCK
    col_id = lax.broadcasted_iota(jnp.int32, (Wd, ncol), 1)
    blk_of_row = lax.broadcasted_iota(jnp.int32, (Wd, PAGE_SIZE), 1) // CMP_BLOCK
    out = jnp.zeros((Wd, ncol), F32)
    for k, page in enumerate(refs[:-1]):
        y = page[0] * pw_ref[...]
        for j in range(bpp):
            summary = jnp.sum(jnp.where(blk_of_row == j, y, 0.0), axis=-1, keepdims=True)
            out = jnp.where(col_id == k * bpp + j, summary, out)
    o_ref[0, 0] = out


def cmp_pages(pt_flat, cache_t, pool, B):
    n_pages = pt_flat.shape[0] // B
    pgs = min(PAGES_PER_STEP, n_pages)
    bpp = PAGE_SIZE // CMP_BLOCK
    Wd = cache_t.shape[1]
    pw = jnp.repeat(pool[:, jnp.arange(PAGE_SIZE) % CMP_BLOCK], Wd // 2, axis=0)
    page = lambda k: pl.BlockSpec((1, Wd, PAGE_SIZE), lambda b, s, pt: (pt[b * n_pages + s * pgs + k], 0, 0))
    out = pl.pallas_call(
        _cmp_pages_kernel,
        out_shape=jax.ShapeDtypeStruct((B, n_pages // pgs, Wd, pgs * bpp), F32),
        grid_spec=pltpu.PrefetchScalarGridSpec(
            num_scalar_prefetch=1, grid=(B, n_pages // pgs),
            in_specs=[pl.BlockSpec((Wd, PAGE_SIZE), lambda b, s, pt: (0, 0))]
                     + [page(k) for k in range(pgs)],
            out_specs=pl.BlockSpec((1, 1, Wd, pgs * bpp), lambda b, s, pt: (b, s, 0, 0))),
        compiler_params=_cparams(("parallel", "parallel")),
        name="cmp_pages",
    )(pt_flat, pw, *([cache_t] * pgs))
    return out.transpose(0, 2, 1, 3).reshape(B, Wd, n_pages * bpp)


def _nsa_sample_pre_kernel(q_ref, kvc_ref, wb_ref, new_ref, bc_ref, bw_ref, bn_ref, oc_ref, ow_ref, sel_ref,
                           *, past, n_new):
    H = NSA_HEADS
    R = q_ref.shape[1]
    nb = kvc_ref.shape[2]
    W = wb_ref.shape[2]
    scale = NSA_DH ** -0.5
    nt = (((1,), (1,)), ((), ()))
    qp = _pad_groups(q_ref[0], H)

    kvc = kvc_ref[0]
    s_c = jnp.dot(qp, kvc[:KVH].astype(BF16), preferred_element_type=F32) * scale + bc_ref[...]
    q_pos = past + lax.broadcasted_iota(jnp.int32, (R, nb), 0) // H
    n_i = lax.broadcasted_iota(jnp.int32, (R, nb), 1)
    vis = q_pos >= n_i * CMP_BLOCK + (CMP_BLOCK - 1)
    s_c = jnp.where(vis, s_c, NEG)
    e_c = jnp.where(vis, jnp.exp(s_c - jnp.max(s_c, axis=-1, keepdims=True)), 0.0)
    p_c = e_c / jnp.maximum(jnp.sum(e_c, axis=-1, keepdims=True), 1e-30)
    oc_ref[0] = _own_group(lax.dot_general(p_c.astype(BF16), kvc[KVH:].astype(BF16), nt,
                                           preferred_element_type=F32), H)

    same = (lax.broadcasted_iota(jnp.int32, (R, R), 0) // HPG == lax.broadcasted_iota(jnp.int32, (R, R), 1) // HPG)
    imp = jnp.dot(jnp.where(same, 1.0, 0.0), p_c, preferred_element_type=F32, precision=lax.Precision.HIGHEST)
    cur = n_i == q_pos // CMP_BLOCK
    work = jnp.where(jnp.logical_and(vis, jnp.logical_not(cur)), imp, -1.0)
    sel = jnp.where(cur, 1.0, 0.0)
    n_f = n_i.astype(F32)
    for _ in range(min(N_SEL - 1, nb)):
        mx = jnp.max(work, axis=-1, keepdims=True)
        first = jnp.min(jnp.where(work == mx, n_f, float(nb)), axis=-1, keepdims=True)
        pick = n_f == first
        sel = jnp.where(jnp.logical_and(pick, mx >= 0.0), 1.0, sel)
        work = jnp.where(pick, -2.0, work)
    sel_ref[0] = sel.astype(sel_ref.dtype)

    wb, new = wb_ref[0], new_ref[0]
    s1 = jnp.dot(qp, wb[:KVH].astype(BF16), preferred_element_type=F32) * scale + bw_ref[...]
    i1 = lax.broadcasted_iota(jnp.int32, (R, W), 1)
    d1 = W + lax.broadcasted_iota(jnp.int32, (R, W), 0) // H - i1
    m1 = jnp.logical_and(jnp.logical_and(d1 >= 0, d1 < WINDOW), past - W + i1 >= 0)
    L = new.shape[0]
    s2 = lax.dot_general(qp, new[:, :KVH].astype(BF16), nt, preferred_element_type=F32) * scale + bn_ref[...]
    j2 = lax.broadcasted_iota(jnp.int32, (R, L), 1)
    d2 = lax.broadcasted_iota(jnp.int32, (R, L), 0) // H - j2
    m2 = jnp.logical_and(jnp.logical_and(d2 >= 0, d2 < WINDOW), j2 < n_new)
    s1 = jnp.where(m1, s1, NEG)
    s2 = jnp.where(m2, s2, NEG)
    mx = jnp.maximum(jnp.max(s1, axis=-1, keepdims=True), jnp.max(s2, axis=-1, keepdims=True))
    e1 = jnp.where(m1, jnp.exp(s1 - mx), 0.0)
    e2 = jnp.where(m2, jnp.exp(s2 - mx), 0.0)
    den = jnp.sum(e1, axis=-1, keepdims=True) + jnp.sum(e2, axis=-1, keepdims=True)
    o_w = (lax.dot_general(e1.astype(BF16), wb[KVH:].astype(BF16), nt, preferred_element_type=F32)
           + jnp.dot(e2.astype(BF16), new[:, KVH:].astype(BF16), preferred_element_type=F32))
    ow_ref[0] = _own_group(o_w / jnp.maximum(den, 1e-30), H)


def nsa_sample_pre(q, kvc, wbuf, new_win, bias_c, bias_w, bias_new, past, n_new):
    B, R, dh = q.shape
    nb, W, L = kvc.shape[2], wbuf.shape[2], new_win.shape[1]
    per_b = lambda *s: pl.BlockSpec((1,) + s, lambda b: (b,) + (0,) * len(s))
    const = lambda *s: pl.BlockSpec(s, lambda b: (0,) * len(s))
    return pl.pallas_call(
        functools.partial(_nsa_sample_pre_kernel, past=past, n_new=n_new),
        out_shape=(jax.ShapeDtypeStruct((B, R, dh), F32), jax.ShapeDtypeStruct((B, R, dh), F32),
                   jax.ShapeDtypeStruct((B, R, nb), BF16)),
        grid=(B,),
        in_specs=[per_b(R, dh), per_b(2 * KVH, nb), per_b(2 * KVH, W), per_b(L, 2 * KVH),
                  const(R, nb), const(R, W), const(R, L)],
        out_specs=(per_b(R, dh), per_b(R, dh), per_b(R, nb)),
        compiler_params=_cparams(("parallel",)),
        name="nsa_sample_pre",
    )(q, kvc, wbuf, new_win, bias_c, bias_w, bias_new)


def _index_scores(qi, w, keys, n_heads, feature_major):
    contract = (((1,), (0,)), ((), ())) if feature_major else (((1,), (1,)), ((), ()))
    s = lax.dot_general(qi, keys.astype(BF16), contract, preferred_element_type=F32)
    v = jnp.maximum(s * IDX_DH ** -0.5, 0.0) * w
    T = qi.shape[0] // n_heads
    return jnp.sum(v.reshape(T, n_heads, s.shape[1]), axis=1) * n_heads ** -0.5 + 0.0


def _idx_scores_kernel(pt_ref, q_ref, w_ref, *refs):
    o_ref = refs[-1]
    qi = q_ref[0].astype(BF16)
    for k, page in enumerate(refs[:-1]):
        o_ref[0, k] = _index_scores(qi, w_ref[0], page[0], IDX_HEADS, True)


def idx_scores(pt_flat, q_i, w_i, cache_t):
    B, R, dh = q_i.shape
    T = R // IDX_HEADS
    n_pages = pt_flat.shape[0] // B
    pgs = min(2 * PAGES_PER_STEP, n_pages)
    page = lambda k: pl.BlockSpec((1, dh, PAGE_SIZE), lambda b, s, pt: (pt[b * n_pages + s * pgs + k], 0, 0))
    return pl.pallas_call(
        _idx_scores_kernel,
        out_shape=jax.ShapeDtypeStruct((B, n_pages, T, PAGE_SIZE), F32),
        grid_spec=pltpu.PrefetchScalarGridSpec(
            num_scalar_prefetch=1, grid=(B, n_pages // pgs),
            in_specs=[pl.BlockSpec((1, R, dh), lambda b, s, pt: (b, 0, 0)),
                      pl.BlockSpec((1, R, 1), lambda b, s, pt: (b, 0, 0))] + [page(k) for k in range(pgs)],
            out_specs=pl.BlockSpec((1, pgs, T, PAGE_SIZE), lambda b, s, pt: (b, s, 0, 0))),
        compiler_params=_cparams(("parallel", "parallel")),
        name="idx_scores",
    )(pt_flat, q_i, w_i, *([cache_t] * pgs))


def _dsa_sample_pre_kernel(sp_ref, q_ref, w_ref, kn_ref, sn_ref, thr_ref, last_ref, *, topk, n_new):
    T, L = sp_ref.shape[2], sp_ref.shape[3]
    sc = _index_scores(q_ref[0].astype(BF16), w_ref[0], kn_ref[0], IDX_HEADS, False)
    j = lax.broadcasted_iota(jnp.int32, (T, L), 1)
    t = lax.broadcasted_iota(jnp.int32, (T, L), 0)
    sc = jnp.where(jnp.logical_and(j <= t, j < n_new), sc, NEG)
    sn_ref[0, 0] = sc
    keys = jnp.concatenate([_order_key(sp_ref[0]), _order_key(sc)[None]], axis=0)
    thr, last = _kth_threshold(keys, topk)
    thr_ref[0] = jnp.broadcast_to(thr[0], (T, L))
    last_ref[0] = jnp.broadcast_to(last[0], (T, L))


def dsa_sample_pre(scores_past, q_i, w_i, new_ki, topk, n_new):
    B, n_pages, T, L = scores_past.shape
    R, dh = q_i.shape[1:]
    per_b = lambda *s: pl.BlockSpec((1,) + s, lambda b: (b,) + (0,) * len(s))
    return pl.pallas_call(
        functools.partial(_dsa_sample_pre_kernel, topk=topk, n_new=n_new),
        out_shape=(jax.ShapeDtypeStruct((B, 1, T, L), F32), jax.ShapeDtypeStruct((B, T, L), jnp.int32),
                   jax.ShapeDtypeStruct((B, T, L), jnp.int32)),
        grid=(B,),
        in_specs=[per_b(n_pages, T, L), per_b(R, dh), per_b(R, 1), per_b(L, dh)],
        out_specs=(per_b(1, T, L), per_b(T, L), per_b(T, L)),
        compiler_params=_cparams(("parallel",)),
        name="dsa_sample_pre",
    )(scores_past, q_i, w_i, new_ki)


def _paged_attn_kernel(pt_ref, *refs, mode, pgs, n_pages, n_heads, n_new, dh):
    if mode == "blocks":
        q_ref, new_ref, far_ref, tl_ref, tn_ref, sel_ref, oc_ref, ow_ref, gate_ref = refs[:9]
        rest = refs[9:]
    else:
        q_ref, new_ref, far_ref, tl_ref, tn_ref, sp_ref, sn_ref, thr_ref, last_ref = refs[:9]
        rest = refs[9:]
    pages, o_ref = rest[:pgs], rest[pgs]
    qp_s, m_s, l_s, acc_s = rest[pgs + 1:]
    step, n_steps = pl.program_id(1), pl.num_programs(1)
    R = q_ref.shape[1]
    T = R // n_heads
    L = PAGE_SIZE
    scale = dh ** -0.5

    @pl.when(step == 0)
    def _():
        qp_s[...] = _pad_groups(q_ref[0], n_heads)
        m_s[...] = jnp.full(m_s.shape, NEG, F32)
        l_s[...] = jnp.zeros(l_s.shape, F32)
        acc_s[...] = jnp.zeros(acc_s.shape, F32)

    def update_new(kv, bias, mask):
        s = lax.dot_general(qp_s[...], kv[:, :KVH].astype(BF16), (((1,), (1,)), ((), ())),
                            preferred_element_type=F32)
        s = jnp.where(mask, s * scale + bias, NEG)
        m_old = m_s[...]
        m_new = jnp.maximum(m_old, jnp.max(s, axis=-1, keepdims=True))
        p = jnp.where(mask, jnp.exp(s - m_new), 0.0)
        alpha = jnp.exp(m_old - m_new)
        l_s[...] = alpha * l_s[...] + jnp.sum(p, axis=-1, keepdims=True)
        acc_s[...] = alpha * acc_s[...] + jnp.dot(p.astype(BF16), kv[:, KVH:].astype(BF16),
                                                  preferred_element_type=F32)
        m_s[...] = m_new

    def rows_of_tokens(x):
        return jnp.broadcast_to(x[:, None, :], (T, n_heads, L)).reshape(R, L)

    def chosen(scores, page):
        key = _order_key(scores)
        pos = page * L + lax.broadcasted_iota(jnp.int32, (T, L), 1)
        thr = thr_ref[0]
        ch = jnp.logical_or(key > thr, jnp.logical_and(key == thr, pos <= last_ref[0]))
        return rows_of_tokens(jnp.where(ch, 1.0, 0.0)) > 0.5

    qp = qp_s[...]
    scores, masks = [], []
    for k in range(pgs):
        page = step * pgs + k
        bias = jnp.where(page == n_pages - 1, tl_ref[...], far_ref[...])
        if mode == "blocks":
            nb = sel_ref.shape[2]
            blk = page * (L // CMP_BLOCK) + lax.broadcasted_iota(jnp.int32, (nb, L), 1) // CMP_BLOCK
            expand = jnp.where(lax.broadcasted_iota(jnp.int32, (nb, L), 0) == blk, 1.0, 0.0).astype(BF16)
            mask = jnp.dot(sel_ref[0], expand, preferred_element_type=F32) > 0.5
        else:
            mask = chosen(sp_ref[0, k], page)
        s = jnp.dot(qp, pages[k][0, :KVH].astype(BF16), preferred_element_type=F32)
        scores.append(jnp.where(mask, s * scale + bias, NEG))
        masks.append(mask)
    m_old = m_s[...]
    m_new = m_old
    for s in scores:
        m_new = jnp.maximum(m_new, jnp.max(s, axis=-1, keepdims=True))
    alpha = jnp.exp(m_old - m_new)
    l_new = alpha * l_s[...]
    acc = alpha * acc_s[...]
    for k in range(pgs):
        p = jnp.where(masks[k], jnp.exp(scores[k] - m_new), 0.0)
        l_new = l_new + jnp.sum(p, axis=-1, keepdims=True)
        acc = acc + lax.dot_general(p.astype(BF16), pages[k][0, KVH:].astype(BF16), (((1,), (1,)), ((), ())),
                                    preferred_element_type=F32)
    m_s[...] = m_new
    l_s[...] = l_new
    acc_s[...] = acc

    @pl.when(step == n_steps - 1)
    def _():
        j = lax.broadcasted_iota(jnp.int32, (R, L), 1)
        t = lax.broadcasted_iota(jnp.int32, (R, L), 0) // n_heads
        mask = jnp.logical_and(j <= t, j < n_new)
        if mode != "blocks":
            mask = jnp.logical_and(mask, chosen(sn_ref[0, 0], n_pages))
        update_new(new_ref[0], tn_ref[...], mask)
        o = _own_group(acc_s[...] / jnp.maximum(l_s[...], 1e-30), n_heads)
        if mode == "blocks":
            g = jax.nn.sigmoid(gate_ref[0])
            o = g[:, 0:1] * oc_ref[0] + g[:, 1:2] * o + g[:, 2:3] * ow_ref[0]
        o_ref[0] = o.astype(o_ref.dtype)


def paged_attn(mode, pt_flat, cache_t, q, new_rows, far, tile_last, tile_new, extra, n_heads, n_new):
    B, R, dh = q.shape
    n_pages = pt_flat.shape[0] // B
    pgs = min(PAGES_PER_STEP, n_pages)
    L = PAGE_SIZE
    Wd = cache_t.shape[1]
    per_b = lambda *s: pl.BlockSpec((1,) + s, lambda b, st, pt: (b,) + (0,) * len(s))
    const = lambda *s: pl.BlockSpec(s, lambda b, st, pt: (0,) * len(s))
    page = lambda k: pl.BlockSpec((1, Wd, L), lambda b, st, pt: (pt[b * n_pages + st * pgs + k], 0, 0))
    if mode == "blocks":
        sel = extra[0]
        extra_specs = [per_b(R, sel.shape[2]), per_b(R, dh), per_b(R, dh), per_b(R, 3)]
    else:
        T = R // n_heads
        extra_specs = [pl.BlockSpec((1, pgs, T, L), lambda b, st, pt: (b, st, 0, 0)), per_b(1, T, L),
                       per_b(T, L), per_b(T, L)]
    return pl.pallas_call(
        functools.partial(_paged_attn_kernel, mode=mode, pgs=pgs, n_pages=n_pages, n_heads=n_heads, n_new=n_new,
                          dh=dh),
        out_shape=jax.ShapeDtypeStruct((B, R, dh), BF16),
        grid_spec=pltpu.PrefetchScalarGridSpec(
            num_scalar_prefetch=1, grid=(B, n_pages // pgs),
            in_specs=[per_b(R, dh), per_b(L, Wd), const(R, 1), const(R, L), const(R, L)] + extra_specs
                     + [page(k) for k in range(pgs)],
            out_specs=per_b(R, dh),
            scratch_shapes=[pltpu.VMEM((R, KVH), BF16), pltpu.VMEM((R, 1), F32), pltpu.VMEM((R, 1), F32),
                            pltpu.VMEM((R, KVH), F32)]),
        compiler_params=_cparams(("parallel", "arbitrary")),
        name="paged_attn_" + mode,
    )(pt_flat, q, new_rows, far, tile_last, tile_new, *extra, *([cache_t] * pgs))


def _sample_bias(rel_bias, head0, n_heads, T, past, W, nb):
    assert PAGE_SIZE + 1 >= REL_MAX_DIST
    R = T * n_heads
    t = (np.arange(R) // n_heads)[:, None]
    per_row = jnp.tile(rel_bias[:, head0:head0 + n_heads], (1, T))
    onehot = lambda dist: jax.nn.one_hot(jnp.asarray(_rel_bucket(dist)), NUM_BUCKETS, dtype=F32)
    tab = lambda dist: jnp.einsum('rxk,kr->rx', onehot(dist), per_row, precision=lax.Precision.HIGHEST)
    lane = np.arange(PAGE_SIZE)[None, :]
    return dict(far=per_row[NUM_BUCKETS - 1][:, None], last=tab(PAGE_SIZE + t - lane), new=tab(t - lane),
                cmp=tab(past + t - (np.arange(nb)[None, :] * CMP_BLOCK + CMP_BLOCK - 1)),
                win=tab(W + t - np.arange(W)[None, :]))


def mixer_sample(cache_cmp, cache_sel, cache_win, cache_dkv, cache_idx, page_table, rel_bias, pool,
                 q_n, cmp, sel, win, g_n, q_d, dkv, q_i, k_i, w_i):
    B, T = q_n.shape[:2]
    n_pages = page_table.shape[1]
    past = n_pages * PAGE_SIZE
    W = cache_win.shape[1]
    assert T < CMP_BLOCK and past % CMP_BLOCK == 0 and T <= PAGE_SIZE
    topk = min(DSA_TOPK_MAX, (past + T) // 4)
    n_phys = cache_sel.shape[0]
    nb = past // CMP_BLOCK
    pt_flat = page_table.reshape(-1)
    flat = lambda c: jnp.moveaxis(c, 1, -1).reshape(c.shape[0], -1, c.shape[1])
    rows = lambda a: a.reshape(B, T * a.shape[2], -1)
    pad_new = lambda a: jnp.pad(a.reshape(B, T, -1), ((0, 0), (0, PAGE_SIZE - T), (0, 0)))
    bn = _sample_bias(rel_bias, 0, NSA_HEADS, T, past, W, nb)
    bd = _sample_bias(rel_bias, NSA_HEADS, DSA_HEADS, T, past, W, nb)

    kvc = cmp_pages(pt_flat, flat(cache_cmp), pool, B)
    o_c, o_w, selm = nsa_sample_pre(rows(q_n), kvc, flat(cache_win), pad_new(win), bn["cmp"], bn["win"], bn["new"],
                                    past, T)
    o_n = paged_attn("blocks", pt_flat, flat(cache_sel), rows(q_n), pad_new(sel), bn["far"], bn["last"], bn["new"],
                     (selm, o_c, o_w, g_n.reshape(B, T * NSA_HEADS, 3)), NSA_HEADS, T)

    qi, wi = rows(q_i), w_i.reshape(B, T * IDX_HEADS, 1)
    sp = idx_scores(pt_flat, qi, wi, flat(cache_idx))
    sn, thr, last = dsa_sample_pre(sp, qi, wi, pad_new(k_i), topk, T)
    o_d = paged_attn("thr", pt_flat, flat(cache_dkv), rows(q_d), pad_new(dkv), bd["far"], bd["last"], bd["new"],
                     (sp, sn, thr, last), DSA_HEADS, T)
    new_win = jnp.concatenate([cache_win[:, T:], win], axis=1)
    return o_n.reshape(B, T, -1), o_d.reshape(B, T, -1), new_win


def _project(h, w_q, w_kv, w_s, w_g):
    B, S, D = h.shape
    h2 = h.reshape(B * S, D)
    zq = matmul(h2, w_q, out_dtype=F32)
    zkv = matmul(h2, w_kv, out_dtype=F32)
    zs = matmul(h2, w_s, out_dtype=F32)
    zg = matmul(h2, w_g, out_dtype=F32)
    kvn = (B, S, 2, NSA_KV, NSA_DH)
    q_n = zq[:, :NSA_WIDTH].reshape(B, S, NSA_HEADS, NSA_DH)
    q_d = zq[:, NSA_WIDTH:NSA_WIDTH + DSA_WIDTH].reshape(B, S, DSA_HEADS, DSA_DH)
    q_i = zq[:, NSA_WIDTH + DSA_WIDTH:].reshape(B, S, IDX_HEADS, IDX_DH)
    cmp = zkv[:, :KV_W].reshape(kvn)
    sel = zkv[:, KV_W:2 * KV_W].reshape(kvn)
    win = zkv[:, 2 * KV_W:3 * KV_W].reshape(kvn)
    dkv = zkv[:, 3 * KV_W:].reshape(B, S, 2, DSA_KV, DSA_DH)
    k_i = zs[:, :IDX_DH].reshape(B, S, IDX_DH)
    w_i = zs[:, IDX_DH:IDX_DH + IDX_HEADS].reshape(B, S, IDX_HEADS)
    g_n = zs[:, IDX_DH + IDX_HEADS:].reshape(B, S, NSA_HEADS, 3)
    return (q_n, cmp, sel, win, g_n, q_d, dkv, q_i, k_i, w_i), zg


def kernel(x_prompt, x_sample, c_prompt, c_sample, cache_nsa_cmp, cache_nsa_sel, cache_nsa_win, cache_dsa_kv,
           cache_dsa_idx, page_table, rel_bias, w_mod, b_mod, g_pre_mix, g_post_mix, g_pre_ffn, g_post_ffn,
           w_in, cmp_pool, w_up_nsa, w_up_dsa, w_out, w_router, b_router, w_gu, b_gu, w_down, b_down):
    l = 0
    D = D_MODEL
    Bp, Sp, _ = x_prompt.shape
    Bs, Ss, _ = x_sample.shape

    c = jnp.concatenate([c_prompt, c_sample], 0)
    n_c = c.shape[0]
    c_pad = jnp.pad(jax.nn.silu(c), ((0, -n_c % 16), (0, 0))).astype(BF16)
    mod = matmul(c_pad, w_mod[l], b_mod[l], tn=1536)[:n_c]
    mod_p, mod_s = mod[:Bp], mod[Bp:]

    wi = w_in[l]
    seg = lambda k: wi[:, _OFF[k]:_OFF[k + 1]]
    w_q = jnp.concatenate([seg(0), seg(5), seg(7)], 1).astype(BF16)
    w_kv = jnp.concatenate([seg(1), seg(2), seg(3), seg(6)], 1).astype(BF16)
    w_s = jnp.concatenate([seg(8), seg(9), seg(4)], 1).astype(BF16)
    w_g = seg(10).astype(BF16)
    wun, wud, wo = w_up_nsa[l].astype(BF16), w_up_dsa[l].astype(BF16), w_out[l].astype(BF16)

    def mix_front(x, mod_g):
        sh1, sc1 = mod_g[:, :D], mod_g[:, D:2 * D]
        h = norm_mod(x, g_pre_mix[l], sc1, sh1)
        return _project(h, w_q, w_kv, w_s, w_g)

    def mix_back(x, mod_g, o_n, o_d, zg):
        B, S, _ = x.shape
        ga1, sh2, sc2 = mod_g[:, 2 * D:3 * D], mod_g[:, 3 * D:4 * D], mod_g[:, 4 * D:5 * D]
        u = merge_up(o_n.reshape(B * S, -1).astype(BF16), o_d.reshape(B * S, -1).astype(BF16), zg, wun, wud)
        return out_proj(u, x, wo, g_post_mix[l], g_pre_ffn[l], ga1, sc2, sh2, w_router[l], b_router[l])

    parts_p, zg_p = mix_front(x_prompt, mod_p)
    o_n, o_d = mixer_prompt(rel_bias, cmp_pool[l], *parts_p)
    x1_p, h_p, lg_p = mix_back(x_prompt, mod_p, o_n, o_d, zg_p)

    parts_s, zg_s = mix_front(x_sample, mod_s)
    o_n, o_d, new_win = mixer_sample(cache_nsa_cmp[l], cache_nsa_sel[l], cache_nsa_win[l], cache_dsa_kv[l],
                                     cache_dsa_idx[l], page_table, rel_bias, cmp_pool[l], *parts_s)
    x1_s, h_s, lg_s = mix_back(x_sample, mod_s, o_n, o_d, zg_s)

    Tp, Ts = Bp * Sp, Bs * Ss
    h_all = jnp.concatenate([h_p.reshape(Tp, D), h_s.reshape(Ts, D)], 0)
    logits = jnp.concatenate([lg_p.reshape(Tp, N_EXPERTS), lg_s.reshape(Ts, N_EXPERTS)], 0)
    probs, row_tok, blk_e, n_used, dest_of = moe_route(logits)
    out_rows = moe_experts(h_all[row_tok], blk_e, n_used, w_gu[l], b_gu[l], w_down[l], b_down[l])
    picked = lambda lo, hi: [out_rows[dest_of[lo:hi, k]] for k in range(TOP_K)]
    y_p = moe_combine(picked(0, Tp), probs[:Tp], x1_p, g_post_ffn[l], mod_p[:, 5 * D:])
    y_s = moe_combine(picked(Tp, Tp + Ts), probs[Tp:], x1_s, g_post_ffn[l], mod_s[:, 5 * D:])

    st = lambda a: a[None]
    q_n, cmp_p, sel_p, win_p, g_n, q_d, dkv_p, q_i, ki_p, w_i = parts_p
    q_n, cmp_s, sel_s, win_s, g_n, q_d, dkv_s, q_i, ki_s, w_i = parts_s
    keep = min(WINDOW, Sp)
    return (y_p, y_s, st(cmp_p), st(sel_p), st(win_p[:, Sp - keep:]), st(dkv_p), st(ki_p),
            st(cmp_s), st(sel_s), st(new_win), st(dkv_s), st(ki_s))
```

```python
import functools
import math

import jax
import jax.numpy as jnp
import numpy as np
from jax import lax
from jax.experimental import pallas as pl
from jax.experimental.pallas import tpu as pltpu

F32 = jnp.float32
BF16 = jnp.bfloat16

D_MODEL = 2048
NSA_HEADS = 16
NSA_KV = 4
NSA_DH = 64
CMP_BLOCK = 64
N_SEL = 16
WINDOW = 512
DSA_HEADS = 16
DSA_KV = 4
DSA_DH = 64
IDX_HEADS = 16
IDX_DH = 64
DSA_TOPK_MAX = 256
NUM_BUCKETS = 32
REL_MAX_DIST = 128
N_EXPERTS = 32
TOP_K = 4
D_FF = 2048
SWIGLU_LIMIT = 7.0
SWIGLU_ALPHA = 1.702
PAGE_SIZE = 128
Q_BLOCK = 128
RMS_EPS = 1e-6
NEG = -1e30
NSA_WIDTH = NSA_HEADS * NSA_DH
DSA_WIDTH = DSA_HEADS * DSA_DH
KV_W = 2 * NSA_KV * NSA_DH

_SPLIT = (NSA_WIDTH, KV_W, KV_W, KV_W, 3 * NSA_HEADS, DSA_WIDTH, 2 * DSA_KV * DSA_DH,
          IDX_HEADS * IDX_DH, IDX_DH, IDX_HEADS, 2 * D_MODEL)
_OFF = tuple(int(v) for v in np.cumsum((0,) + _SPLIT))

MOE_TM = 512
VMEM_LIMIT = 48 * 1024 * 1024


def _cparams(sem):
    return pltpu.CompilerParams(dimension_semantics=sem, vmem_limit_bytes=VMEM_LIMIT)


def _norm_mod_kernel(x_ref, g_ref, sc_ref, sh_ref, o_ref):
    x = x_ref[0]
    y = x * lax.rsqrt(jnp.mean(x * x, axis=-1, keepdims=True) + RMS_EPS) * g_ref[...]
    o_ref[0] = (y * (1.0 + sc_ref[0]) + sh_ref[0]).astype(o_ref.dtype)


def norm_mod(x, g, scale, shift):
    B, S, D = x.shape
    ts = min(S, 512)
    row = pl.BlockSpec((1, ts, D), lambda b, s: (b, s, 0))
    per_b = pl.BlockSpec((1, 1, D), lambda b, s: (b, 0, 0))
    return pl.pallas_call(
        _norm_mod_kernel,
        out_shape=jax.ShapeDtypeStruct((B, S, D), BF16),
        grid=(B, S // ts),
        in_specs=[row, pl.BlockSpec((1, D), lambda b, s: (0, 0)), per_b, per_b],
        out_specs=row,
        compiler_params=_cparams(("parallel", "parallel")),
        name="norm_mod",
    )(x, g.reshape(1, D), scale.reshape(B, 1, D), shift.reshape(B, 1, D))


def _matmul_kernel(a_ref, w_ref, b_ref, o_ref):
    acc = jnp.dot(a_ref[...], w_ref[...].astype(BF16), preferred_element_type=F32)
    o_ref[...] = (acc + b_ref[...]).astype(o_ref.dtype)


def matmul(a, w, bias=None, out_dtype=F32, tm=512, tn=512):
    M, K = a.shape
    N = w.shape[1]
    tm, tn = min(tm, M), min(tn, N)
    assert M % tm == 0 and N % tn == 0
    if bias is None:
        bias = jnp.zeros((N,), F32)
    return pl.pallas_call(
        _matmul_kernel,
        out_shape=jax.ShapeDtypeStruct((M, N), out_dtype),
        grid=(M // tm, N // tn),
        in_specs=[pl.BlockSpec((tm, K), lambda i, j: (i, 0)),
                  pl.BlockSpec((K, tn), lambda i, j: (0, j)),
                  pl.BlockSpec((1, tn), lambda i, j: (0, j))],
        out_specs=pl.BlockSpec((tm, tn), lambda i, j: (i, j)),
        compiler_params=_cparams(("parallel", "parallel")),
        name="matmul",
    )(a, w, bias.reshape(1, N))


def _merge_kernel(on_ref, od_ref, h_ref, wga_ref, wgb_ref, wn_ref, wd_ref, o_ref):
    h = h_ref[...]
    ga = jnp.dot(h, wga_ref[...], preferred_element_type=F32)
    gb = jnp.dot(h, wgb_ref[...], preferred_element_type=F32)
    a = jnp.dot(on_ref[...], wn_ref[...], preferred_element_type=F32)
    b = jnp.dot(od_ref[...], wd_ref[...], preferred_element_type=F32)
    o_ref[...] = (jax.nn.sigmoid(ga) * a + jax.nn.sigmoid(gb) * b).astype(o_ref.dtype)


def merge_up(o_n, o_d, h, w_g, w_up_nsa, w_up_dsa, tm=512, tn=512):
    M, K = h.shape
    D = w_up_nsa.shape[1]
    tm = min(tm, M)
    nj = D // tn
    return pl.pallas_call(
        _merge_kernel,
        out_shape=jax.ShapeDtypeStruct((M, D), BF16),
        grid=(M // tm, nj),
        in_specs=[pl.BlockSpec((tm, o_n.shape[1]), lambda i, j: (i, 0)),
                  pl.BlockSpec((tm, o_d.shape[1]), lambda i, j: (i, 0)),
                  pl.BlockSpec((tm, K), lambda i, j: (i, 0)),
                  pl.BlockSpec((K, tn), lambda i, j: (0, j)),
                  pl.BlockSpec((K, tn), lambda i, j: (0, j + nj)),
                  pl.BlockSpec((o_n.shape[1], tn), lambda i, j: (0, j)),
                  pl.BlockSpec((o_d.shape[1], tn), lambda i, j: (0, j))],
        out_specs=pl.BlockSpec((tm, tn), lambda i, j: (i, j)),
        compiler_params=_cparams(("parallel", "parallel")),
        name="merge_up",
    )(o_n, o_d, h, w_g, w_g, w_up_nsa, w_up_dsa)


def _rms(x, g):
    return x * lax.rsqrt(jnp.mean(x * x, axis=-1, keepdims=True) + RMS_EPS) * g


def _out_proj_kernel(u_ref, x_ref, w_ref, gpost_ref, gpre_ref, ga_ref, sc_ref, sh_ref, wr_ref, br_ref,
                     x1_ref, h_ref, lg_ref):
    m = jnp.dot(u_ref[0], w_ref[...], preferred_element_type=F32)
    x1 = x_ref[0] + ga_ref[0] * _rms(m, gpost_ref[...])
    x1_ref[0] = x1
    h = _rms(x1, gpre_ref[...]) * (1.0 + sc_ref[0]) + sh_ref[0]
    h_ref[0] = h.astype(h_ref.dtype)
    lg_ref[0] = jnp.dot(h, wr_ref[...], preferred_element_type=F32,
                        precision=lax.Precision.HIGHEST) + br_ref[...]


def out_proj(u, x, w_out, g_post, g_pre, gate, scale, shift, w_router, b_router):
    B, S, D = x.shape
    ts = min(S, 256)
    E = w_router.shape[1]
    row = lambda d: pl.BlockSpec((1, ts, d), lambda b, s: (b, s, 0))
    per_b = pl.BlockSpec((1, 1, D), lambda b, s: (b, 0, 0))
    vec = lambda d: pl.BlockSpec((1, d), lambda b, s: (0, 0))
    return pl.pallas_call(
        _out_proj_kernel,
        out_shape=(jax.ShapeDtypeStruct((B, S, D), F32), jax.ShapeDtypeStruct((B, S, D), BF16),
                   jax.ShapeDtypeStruct((B, S, E), F32)),
        grid=(B, S // ts),
        in_specs=[row(D), row(D), pl.BlockSpec((D, D), lambda b, s: (0, 0)), vec(D), vec(D),
                  per_b, per_b, per_b, pl.BlockSpec((D, E), lambda b, s: (0, 0)), vec(E)],
        out_specs=(row(D), row(D), row(E)),
        compiler_params=_cparams(("parallel", "parallel")),
        name="out_proj",
    )(u.reshape(B, S, D), x, w_out, g_post.reshape(1, D), g_pre.reshape(1, D), gate.reshape(B, 1, D),
      scale.reshape(B, 1, D), shift.reshape(B, 1, D), w_router, b_router.reshape(1, E))


def _expert_changed(be_ref, b):
    return jnp.logical_or(b == 0, be_ref[b] != be_ref[jnp.maximum(b - 1, 0)])


def _moe_up_kernel(be_ref, nu_ref, x_ref, wg_ref, wl_ref, bg_ref, bl_ref, o_ref, wg_s, wl_s):
    b = pl.program_id(1)

    @pl.when(_expert_changed(be_ref, b))
    def _():
        wg_s[...] = wg_ref[0].astype(BF16)
        wl_s[...] = wl_ref[0].astype(BF16)

    @pl.when(b < nu_ref[0])
    def _():
        x = x_ref[...]
        g = jnp.dot(x, wg_s[...], preferred_element_type=F32) + bg_ref[0]
        lin = jnp.dot(x, wl_s[...], preferred_element_type=F32) + bl_ref[0]
        g = jnp.minimum(g, SWIGLU_LIMIT)
        lin = jnp.clip(lin, -SWIGLU_LIMIT, SWIGLU_LIMIT)
        o_ref[...] = (g * jax.nn.sigmoid(SWIGLU_ALPHA * g) * (lin + 1.0)).astype(o_ref.dtype)

    @pl.when(b >= nu_ref[0])
    def _():
        o_ref[...] = jnp.zeros_like(o_ref)


def _moe_down_kernel(be_ref, nu_ref, a_ref, w_ref, bias_ref, o_ref, w_s):
    b = pl.program_id(1)

    @pl.when(_expert_changed(be_ref, b))
    def _():
        w_s[...] = w_ref[0].astype(BF16)

    @pl.when(b < nu_ref[0])
    def _():
        o_ref[...] = jnp.dot(a_ref[...], w_s[...], preferred_element_type=F32) + bias_ref[0]

    @pl.when(b >= nu_ref[0])
    def _():
        o_ref[...] = jnp.zeros_like(o_ref)


def moe_experts(xs, blk_e, n_used, w_gu, b_gu, w_down, b_down, tf=512, tn=1024):
    R, D = xs.shape
    nb = R // MOE_TM
    E = w_gu.shape[0]
    nf = D_FF // tf
    act = pl.pallas_call(
        _moe_up_kernel,
        out_shape=jax.ShapeDtypeStruct((R, D_FF), BF16),
        grid_spec=pltpu.PrefetchScalarGridSpec(
            num_scalar_prefetch=2,
            grid=(nf, nb),
            in_specs=[pl.BlockSpec((MOE_TM, D), lambda f, b, be, nu: (b, 0)),
                      pl.BlockSpec((1, D, tf), lambda f, b, be, nu: (be[b], 0, f)),
                      pl.BlockSpec((1, D, tf), lambda f, b, be, nu: (be[b], 0, f + nf)),
                      pl.BlockSpec((1, 1, tf), lambda f, b, be, nu: (be[b], 0, f)),
                      pl.BlockSpec((1, 1, tf), lambda f, b, be, nu: (be[b], 0, f + nf))],
            out_specs=pl.BlockSpec((MOE_TM, tf), lambda f, b, be, nu: (b, f)),
            scratch_shapes=[pltpu.VMEM((D, tf), BF16), pltpu.VMEM((D, tf), BF16)]),
        compiler_params=_cparams(("arbitrary", "arbitrary")),
        name="moe_up",
    )(blk_e, n_used, xs, w_gu, w_gu, b_gu.reshape(E, 1, 2 * D_FF), b_gu.reshape(E, 1, 2 * D_FF))
    nn = D // tn
    return pl.pallas_call(
        _moe_down_kernel,
        out_shape=jax.ShapeDtypeStruct((R, D), F32),
        grid_spec=pltpu.PrefetchScalarGridSpec(
            num_scalar_prefetch=2,
            grid=(nn, nb),
            in_specs=[pl.BlockSpec((MOE_TM, D_FF), lambda j, b, be, nu: (b, 0)),
                      pl.BlockSpec((1, D_FF, tn), lambda j, b, be, nu: (be[b], 0, j)),
                      pl.BlockSpec((1, 1, tn), lambda j, b, be, nu: (be[b], 0, j))],
            out_specs=pl.BlockSpec((MOE_TM, tn), lambda j, b, be, nu: (b, j)),
            scratch_shapes=[pltpu.VMEM((D_FF, tn), BF16)]),
        compiler_params=_cparams(("arbitrary", "arbitrary")),
        name="moe_down",
    )(blk_e, n_used, act, w_down, b_down.reshape(E, 1, D))


def _combine_kernel(p_ref, x_ref, g_ref, ga_ref, *refs):
    o_ref = refs[-1]
    p = p_ref[0]
    f = refs[0][0] * p[:, 0:1]
    for k in range(1, TOP_K):
        f = f + refs[k][0] * p[:, k:k + 1]
    o_ref[0] = x_ref[0] + ga_ref[0] * _rms(f, g_ref[...])


def moe_combine(rows, probs, x, g_post, gate):
    B, S, D = x.shape
    ts = min(S, 256)
    row = pl.BlockSpec((1, ts, D), lambda b, s: (b, s, 0))
    return pl.pallas_call(
        _combine_kernel,
        out_shape=jax.ShapeDtypeStruct((B, S, D), F32),
        grid=(B, S // ts),
        in_specs=[pl.BlockSpec((1, ts, TOP_K), lambda b, s: (b, s, 0)), row,
                  pl.BlockSpec((1, D), lambda b, s: (0, 0)),
                  pl.BlockSpec((1, 1, D), lambda b, s: (b, 0, 0))] + [row] * TOP_K,
        out_specs=row,
        compiler_params=_cparams(("parallel", "parallel")),
        name="moe_combine",
    )(probs.reshape(B, S, TOP_K), x, g_post.reshape(1, D), gate.reshape(B, 1, D),
      *[r.reshape(B, S, D) for r in rows])


def moe_route(logits):
    T = logits.shape[0]
    top_v, top_e = lax.top_k(logits, TOP_K)
    probs = jax.nn.softmax(top_v, axis=-1)
    flat_e = top_e.reshape(-1)
    n_assign = flat_e.shape[0]
    order = jnp.argsort(flat_e)
    e_sorted = flat_e[order]
    counts = jnp.bincount(flat_e, length=N_EXPERTS)
    padded = (counts + MOE_TM - 1) // MOE_TM * MOE_TM
    pad_end = jnp.cumsum(padded)
    start = jnp.cumsum(counts) - counts
    dest = (pad_end - padded)[e_sorted] + jnp.arange(n_assign) - start[e_sorted]
    n_blocks = -(-n_assign // MOE_TM) + N_EXPERTS
    row_tok = jnp.zeros((n_blocks * MOE_TM,), jnp.int32).at[dest].set((order // TOP_K).astype(jnp.int32))
    blk_e = jnp.minimum(jnp.searchsorted(pad_end, jnp.arange(n_blocks) * MOE_TM, side='right'),
                        N_EXPERTS - 1).astype(jnp.int32)
    dest_of = jnp.zeros((n_assign,), jnp.int32).at[order].set(dest.astype(jnp.int32))
    n_used = (pad_end[-1] // MOE_TM).astype(jnp.int32).reshape(1)
    return probs, row_tok, blk_e, n_used, dest_of.reshape(T, TOP_K)


def _rel_bucket(dist):
    n = np.maximum(np.asarray(dist), 0)
    exact = NUM_BUCKETS // 2
    log_b = exact + (np.log(np.maximum(n, 1).astype(np.float32) / np.float32(exact))
                     / np.float32(math.log(REL_MAX_DIST / exact)) * (NUM_BUCKETS - exact)).astype(np.int32)
    return np.where(n < exact, n, np.minimum(log_b, NUM_BUCKETS - 1)).astype(np.int32)


def _bias_rows(bkt, table):
    onehot = jax.nn.one_hot(jnp.asarray(bkt), NUM_BUCKETS, dtype=F32)
    return jnp.tensordot(onehot, table, axes=1, precision=lax.Precision.HIGHEST)


TQ = Q_BLOCK
HPG = NSA_HEADS // NSA_KV
INT_MIN = -2 ** 31


def _flash_step(q2, k, v, bias, mask, scale, carry):
    m, l, acc = carry
    nk = k.shape[0]
    s = lax.dot_general(q2, k, (((1,), (1,)), ((), ())), preferred_element_type=F32)
    s = s.reshape(HPG, TQ, nk) * scale + bias
    s = jnp.where(mask[None], s, NEG)
    m_new = jnp.maximum(m, jnp.max(s, axis=-1, keepdims=True))
    p = jnp.where(mask[None], jnp.exp(s - m_new), 0.0)
    alpha = jnp.exp(m - m_new)
    l = alpha * l + jnp.sum(p, axis=-1, keepdims=True)
    pv = jnp.dot(p.reshape(HPG * TQ, nk).astype(BF16), v, preferred_element_type=F32)
    acc = alpha * acc + pv.reshape(HPG, TQ, v.shape[1])
    return m_new, l, acc


def _flash_init(dh):
    return (jnp.full((HPG, TQ, 1), NEG, F32), jnp.zeros((HPG, TQ, 1), F32), jnp.zeros((HPG, TQ, dh), F32))


def _flash_out(carry):
    m, l, acc = carry
    return acc / jnp.maximum(l, 1e-30)


def _causal_blocks(i, q2, k_ref, v_ref, mask_ref, tiles_ref, far_ref, scale):
    dh = k_ref.shape[-1]
    a = lax.broadcasted_iota(jnp.int32, (TQ, TQ), 0)
    b = lax.broadcasted_iota(jnp.int32, (TQ, TQ), 1)

    def blk(ref, jb):
        return ref[pl.ds(pl.multiple_of(jb * TQ, TQ), TQ), :]

    first = i == 0
    jp = jnp.maximum(i - 1, 0)
    causal = jnp.where(a >= b, 1.0, 0.0)
    mask_a = jnp.where(first, mask_ref[jp] * causal, mask_ref[jp])
    mask_b = jnp.where(first, 0.0, mask_ref[jp + 1] * causal)
    bias = jnp.concatenate([jnp.where(first, tiles_ref[:, 0], tiles_ref[:, 1]), tiles_ref[:, 0]], axis=-1)
    near = pl.ds(pl.multiple_of(jp * TQ, TQ), 2 * TQ)
    carry = _flash_step(q2, k_ref[near, :], v_ref[near, :], bias,
                        jnp.concatenate([mask_a, mask_b], axis=1) > 0.5, scale, _flash_init(dh))
    far = far_ref[...][:, :, :1]
    n_far = jnp.maximum(i - 1, 0)

    def pair(jp2, c):
        rows = pl.ds(pl.multiple_of(jp2 * (2 * TQ), 2 * TQ), 2 * TQ)
        mask = jnp.concatenate([mask_ref[2 * jp2], mask_ref[2 * jp2 + 1]], axis=1) > 0.5
        return _flash_step(q2, k_ref[rows, :], v_ref[rows, :], far, mask, scale, c)

    def single(_, c):
        jb = n_far - 1
        return _flash_step(q2, blk(k_ref, jb), blk(v_ref, jb), far, mask_ref[jb] > 0.5, scale, c)

    carry = lax.fori_loop(0, n_far // 2, pair, carry)
    return lax.fori_loop(0, n_far % 2, single, carry)


def _nsa_prompt_kernel(q_ref, kc_ref, vc_ref, ks_ref, vs_ref, kw_ref, vw_ref, pool_ref, bc_ref, tiles_ref,
                       far_ref, gate_ref, o_ref, kcs, vcs, exp_s, mask_s):
    i = pl.program_id(2)
    S = ks_ref.shape[3]
    nb = S // CMP_BLOCK
    scale = NSA_DH ** -0.5

    @pl.when(i == 0)
    def _():
        kr = kc_ref[0, 0, 0].reshape(nb, CMP_BLOCK, NSA_DH) * pool_ref[0][None]
        vr = vc_ref[0, 0, 0].reshape(nb, CMP_BLOCK, NSA_DH) * pool_ref[1][None]
        kcs[...] = jnp.sum(kr, axis=1).astype(BF16)
        vcs[...] = jnp.sum(vr, axis=1).astype(BF16)
        n_i = lax.broadcasted_iota(jnp.int32, (nb, S), 0)
        s_i = lax.broadcasted_iota(jnp.int32, (nb, S), 1)
        exp_s[...] = jnp.where(s_i // CMP_BLOCK == n_i, 1.0, 0.0).astype(BF16)

    q2 = q_ref[0].reshape(HPG * TQ, NSA_DH)
    t0 = i * TQ

    s_c = lax.dot_general(q2, kcs[...], (((1,), (1,)), ((), ())), preferred_element_type=F32)
    s_c = s_c.reshape(HPG, TQ, nb) * scale + bc_ref[...]
    t_i = t0 + lax.broadcasted_iota(jnp.int32, (TQ, nb), 0)
    n_i = lax.broadcasted_iota(jnp.int32, (TQ, nb), 1)
    vis = t_i >= n_i * CMP_BLOCK + (CMP_BLOCK - 1)
    s_c = jnp.where(vis[None], s_c, NEG)
    e_c = jnp.where(vis[None], jnp.exp(s_c - jnp.max(s_c, axis=-1, keepdims=True)), 0.0)
    p_c = e_c / jnp.maximum(jnp.sum(e_c, axis=-1, keepdims=True), 1e-30)
    o_c = jnp.dot(p_c.reshape(HPG * TQ, nb).astype(BF16), vcs[...], preferred_element_type=F32)

    nt = (((1,), (1,)), ((), ()))
    eye = lambda n: jnp.where(lax.broadcasted_iota(jnp.int32, (n, n), 0) == lax.broadcasted_iota(jnp.int32, (n, n), 1),
                              1.0, 0.0)
    imp_t = lax.dot_general(eye(nb), jnp.sum(p_c, axis=0), nt, preferred_element_type=F32,
                            precision=lax.Precision.HIGHEST)
    n_t = lax.broadcasted_iota(jnp.int32, (nb, TQ), 0)
    t_t = t0 + lax.broadcasted_iota(jnp.int32, (nb, TQ), 1)
    cur = n_t == t_t // CMP_BLOCK
    work = jnp.where(jnp.logical_and(t_t >= n_t * CMP_BLOCK + (CMP_BLOCK - 1), jnp.logical_not(cur)), imp_t, -1.0)
    sel_t = jnp.where(cur, 1.0, 0.0)
    n_f = n_t.astype(F32)
    for _ in range(min(N_SEL - 1, nb)):
        mx = jnp.max(work, axis=0, keepdims=True)
        first = jnp.min(jnp.where(work == mx, n_f, float(nb)), axis=0, keepdims=True)
        pick = n_f == first
        sel_t = jnp.where(jnp.logical_and(pick, mx >= 0.0), 1.0, sel_t)
        work = jnp.where(pick, -2.0, work)
    sel = lax.dot_general(eye(TQ).astype(BF16), sel_t.astype(BF16), nt, preferred_element_type=F32)
    selx = jnp.dot(sel.astype(BF16), exp_s[...], preferred_element_type=F32)
    for jb in range(S // TQ):
        mask_s[jb] = selx[:, jb * TQ:(jb + 1) * TQ]

    o_s = _flash_out(_causal_blocks(i, q2, ks_ref.at[0, 0, 0], vs_ref.at[0, 0, 0], mask_s, tiles_ref, far_ref,
                                    scale))

    nw = WINDOW + TQ
    rows = pl.ds(pl.multiple_of(t0, TQ), nw)
    s_w = lax.dot_general(q2, kw_ref[0, 0, 0, rows, :], (((1,), (1,)), ((), ())), preferred_element_type=F32)
    far_w = jnp.broadcast_to(far_ref[...][:, :, :1], (HPG, TQ, nw - 2 * TQ))
    s_w = s_w.reshape(HPG, TQ, nw) * scale + jnp.concatenate([far_w, tiles_ref[:, 1], tiles_ref[:, 0]], axis=-1)
    c_w = lax.broadcasted_iota(jnp.int32, (TQ, nw), 1)
    d_w = lax.broadcasted_iota(jnp.int32, (TQ, nw), 0) + WINDOW - c_w
    in_w = jnp.logical_and(jnp.logical_and(d_w >= 0, d_w < WINDOW), t0 - WINDOW + c_w >= 0)
    s_w = jnp.where(in_w[None], s_w, NEG)
    e_w = jnp.where(in_w[None], jnp.exp(s_w - jnp.max(s_w, axis=-1, keepdims=True)), 0.0)
    o_w = jnp.dot(e_w.reshape(HPG * TQ, nw).astype(BF16), vw_ref[0, 0, 0, rows, :], preferred_element_type=F32)
    o_w = o_w.reshape(HPG, TQ, NSA_DH) / jnp.maximum(jnp.sum(e_w, axis=-1, keepdims=True), 1e-30)

    g = jax.nn.sigmoid(gate_ref[0])
    o = g[..., 0:1] * o_c.reshape(HPG, TQ, NSA_DH) + g[..., 1:2] * o_s + g[..., 2:3] * o_w
    for j in range(HPG):
        o_ref[0, :, j * NSA_DH:(j + 1) * NSA_DH] = o[j].astype(o_ref.dtype)


def _bias_tables(rel_bias, S):
    assert 2 * TQ - (TQ - 1) >= REL_MAX_DIST
    nb = S // CMP_BLOCK
    t = np.arange(S)[:, None]
    bc = _bias_rows(_rel_bucket(t - (np.arange(nb)[None, :] * CMP_BLOCK + CMP_BLOCK - 1)), rel_bias)
    d = np.arange(2)[:, None, None] * TQ + np.arange(TQ)[None, :, None] - np.arange(TQ)[None, None, :]
    tiles = _bias_rows(_rel_bucket(d), rel_bias)
    far = jnp.broadcast_to(rel_bias[NUM_BUCKETS - 1][:, None, None], (rel_bias.shape[1], 1, TQ))
    return bc.transpose(2, 0, 1), tiles.transpose(3, 0, 1, 2), far


def nsa_prompt(q_n, cmp_t, sel_t, win_t, pool, g_n, bias_c, tiles, far):
    B, H, S, dh = q_n.shape
    assert S >= 2 * TQ
    G = NSA_KV
    nb = S // CMP_BLOCK
    assert win_t.shape[3] == S + WINDOW and WINDOW >= 2 * TQ
    kv = lambda c, n=S: pl.BlockSpec((1, 1, 1, n, dh), lambda b, g, i: (b, c, g, 0, 0))
    return pl.pallas_call(
        _nsa_prompt_kernel,
        out_shape=jax.ShapeDtypeStruct((B, S, H * dh), BF16),
        grid=(B, G, S // TQ),
        in_specs=[pl.BlockSpec((1, HPG, TQ, dh), lambda b, g, i: (b, g, i, 0)),
                  kv(0), kv(1), kv(0), kv(1), kv(0, S + WINDOW), kv(1, S + WINDOW),
                  pl.BlockSpec((2, CMP_BLOCK, dh), lambda b, g, i: (0, 0, 0)),
                  pl.BlockSpec((HPG, TQ, nb), lambda b, g, i: (g, i, 0)),
                  pl.BlockSpec((HPG, 2, TQ, TQ), lambda b, g, i: (g, 0, 0, 0)),
                  pl.BlockSpec((HPG, 1, TQ), lambda b, g, i: (g, 0, 0)),
                  pl.BlockSpec((1, HPG, TQ, 3), lambda b, g, i: (b, g, i, 0))],
        out_specs=pl.BlockSpec((1, TQ, HPG * dh), lambda b, g, i: (b, i, g)),
        scratch_shapes=[pltpu.VMEM((nb, dh), BF16), pltpu.VMEM((nb, dh), BF16), pltpu.VMEM((nb, S), BF16),
                        pltpu.VMEM((S // TQ, TQ, TQ), F32)],
        compiler_params=_cparams(("parallel", "parallel", "arbitrary")),
        name="nsa_prompt",
    )(q_n, cmp_t, cmp_t, sel_t, sel_t, win_t, win_t,
      jnp.broadcast_to(pool[:, :, None], (2, CMP_BLOCK, dh)), bias_c, tiles, far, g_n)


def _dsa_prompt_kernel(qi_ref, ki_ref, wi_ref, qd_ref, kd_ref, vd_ref, tiles_ref, far_ref, o_ref,
                       key_s, mask_s, *, topk):
    i = pl.program_id(1)
    S = ki_ref.shape[1]
    nkb = S // TQ
    t0 = i * TQ
    a = lax.broadcasted_iota(jnp.int32, (TQ, TQ), 0)
    b = lax.broadcasted_iota(jnp.int32, (TQ, TQ), 1)

    w = wi_ref[0] * (IDX_DH ** -0.5 * IDX_HEADS ** -0.5)
    w_cols = [jnp.broadcast_to(w[:, h:h + 1], (TQ, TQ)) for h in range(IDX_HEADS)]
    for jb in range(nkb):
        @pl.when(jb <= i)
        def _():
            kk = ki_ref[0, jb * TQ:(jb + 1) * TQ, :]
            acc = jnp.zeros((TQ, TQ), F32)
            for h in range(IDX_HEADS):
                s = lax.dot_general(qi_ref[0, h], kk, (((1,), (1,)), ((), ())), preferred_element_type=F32)
                acc = acc + jnp.maximum(s, 0.0) * w_cols[h]
            score = jnp.where(jb * TQ + b <= t0 + a, acc, NEG) + 0.0
            bits = pltpu.bitcast(score, jnp.int32)
            key_s[jb] = jnp.where(bits < 0, bits ^ 0x7FFFFFFF, bits)

        @pl.when(jb > i)
        def _():
            bits = pltpu.bitcast(jnp.full((TQ, TQ), NEG, F32), jnp.int32)
            key_s[jb] = bits ^ 0x7FFFFFFF

    half = nkb // 2
    if half * TQ >= topk:
        thr, last = lax.cond(i < half, lambda: _kth_threshold(key_s[:half], topk),
                             lambda: _kth_threshold(key_s[...], topk))
    else:
        thr, last = _kth_threshold(key_s[...], topk)
    keys = key_s[...]
    pos = (lax.broadcasted_iota(jnp.int32, (nkb, TQ, TQ), 0) * TQ
           + lax.broadcasted_iota(jnp.int32, (nkb, TQ, TQ), 2))
    chosen = jnp.logical_or(keys > thr, jnp.logical_and(keys == thr, pos <= last))
    mask_s[...] = jnp.where(chosen, 1.0, 0.0)

    scale = DSA_DH ** -0.5
    for g in range(DSA_KV):
        q2 = qd_ref[0, g * HPG:(g + 1) * HPG].reshape(HPG * TQ, DSA_DH)
        o = _flash_out(_causal_blocks(i, q2, kd_ref.at[0, 0, g], vd_ref.at[0, 0, g], mask_s,
                                      tiles_ref.at[g * HPG:(g + 1) * HPG], far_ref.at[g * HPG:(g + 1) * HPG],
                                      scale))
        for j in range(HPG):
            h = g * HPG + j
            o_ref[0, :, h * DSA_DH:(h + 1) * DSA_DH] = o[j].astype(o_ref.dtype)


def dsa_prompt(q_i, k_i, w_i, q_d, dkv_t, tiles, far):
    B, H, S, dh = q_d.shape
    assert S >= 2 * TQ
    G = DSA_KV
    topk = min(DSA_TOPK_MAX, S // 4)
    heads = pl.BlockSpec((1, H, TQ, dh), lambda b, i: (b, 0, i, 0))
    kv = lambda c: pl.BlockSpec((1, 1, G, S, dh), lambda b, i: (b, c, 0, 0, 0))
    return pl.pallas_call(
        functools.partial(_dsa_prompt_kernel, topk=topk),
        out_shape=jax.ShapeDtypeStruct((B, S, H * dh), BF16),
        grid=(B, S // TQ),
        in_specs=[heads,
                  pl.BlockSpec((1, S, k_i.shape[2]), lambda b, i: (b, 0, 0)),
                  pl.BlockSpec((1, TQ, H), lambda b, i: (b, i, 0)),
                  heads, kv(0), kv(1),
                  pl.BlockSpec((H, 2, TQ, TQ), lambda b, i: (0, 0, 0, 0)),
                  pl.BlockSpec((H, 1, TQ), lambda b, i: (0, 0, 0))],
        out_specs=pl.BlockSpec((1, TQ, H * dh), lambda b, i: (b, i, 0)),
        scratch_shapes=[pltpu.VMEM((S // TQ, TQ, TQ), jnp.int32), pltpu.VMEM((S // TQ, TQ, TQ), F32)],
        compiler_params=_cparams(("parallel", "arbitrary")),
        name="dsa_prompt",
    )(q_i, k_i, w_i, q_d, dkv_t, dkv_t, tiles, far)


def mixer_prompt(rel_bias, pool, q_n, cmp, sel, win, g_n, q_d, dkv, q_i, k_i, w_i):
    B, S = q_n.shape[:2]
    heads_first = lambda a: a.transpose(0, 2, 1, 3)
    kv_t = lambda a, dt: a.transpose(0, 2, 3, 1, 4).astype(dt)
    bias_c, tiles, far = _bias_tables(rel_bias, S)
    win_t = jnp.pad(kv_t(win, BF16), ((0, 0), (0, 0), (0, 0), (WINDOW, 0), (0, 0)))
    o_n = nsa_prompt(heads_first(q_n).astype(BF16), kv_t(cmp, F32), kv_t(sel, BF16), win_t, pool,
                     heads_first(g_n), bias_c[:NSA_HEADS], tiles[:NSA_HEADS], far[:NSA_HEADS])
    o_d = dsa_prompt(heads_first(q_i).astype(BF16), k_i.astype(BF16), w_i, heads_first(q_d).astype(BF16),
                     kv_t(dkv, BF16), tiles[NSA_HEADS:], far[NSA_HEADS:])
    return o_n, o_d


KVH = NSA_KV * NSA_DH
PAGES_PER_STEP = 16


def _pad_groups(q, n_heads):
    R, dh = q.shape
    G = n_heads // HPG
    col_g = lax.broadcasted_iota(jnp.int32, (R, G * dh), 1) // dh
    row_g = (lax.broadcasted_iota(jnp.int32, (R, G * dh), 0) % n_heads) // HPG
    return jnp.where(col_g == row_g, jnp.concatenate([q] * G, axis=1), 0.0).astype(BF16)


def _own_group(o_pad, n_heads):
    R = o_pad.shape[0]
    G = n_heads // HPG
    dh = o_pad.shape[1] // G
    row_g = (lax.broadcasted_iota(jnp.int32, (R, dh), 0) % n_heads) // HPG
    out = o_pad[:, :dh]
    for g in range(1, G):
        out = jnp.where(row_g == g, o_pad[:, g * dh:(g + 1) * dh], out)
    return out


def _order_key(x):
    bits = pltpu.bitcast(x, jnp.int32)
    return jnp.where(bits < 0, bits ^ 0x7FFFFFFF, bits)


def _kth_threshold(keys, topk):
    n, R, L = keys.shape

    def count(pred):
        c = jnp.sum(jnp.where(pred, 1.0, 0.0), axis=0)
        return jnp.sum(c, axis=-1, keepdims=True)[None]

    def bit_body(i, pat):
        cand = pat | lax.shift_left(jnp.int32(1), 31 - i)
        return jnp.where(count(keys >= (cand ^ INT_MIN)) >= topk, cand, pat)

    thr = lax.fori_loop(0, 32, bit_body, jnp.zeros((1, R, 1), jnp.int32)) ^ INT_MIN
    need = topk - count(keys > thr)
    pos = lax.broadcasted_iota(jnp.int32, (n, R, L), 0) * L + lax.broadcasted_iota(jnp.int32, (n, R, L), 2)
    tie = keys == thr
    nbits = (n * L - 1).bit_length()

    def tie_body(i, last):
        cand = last | lax.shift_left(jnp.int32(1), nbits - 1 - i)
        return jnp.where(count(jnp.logical_and(tie, pos < cand)) < need, cand, last)

    crowded = jnp.max(jnp.where(count(tie) > need, 1.0, 0.0)) > 0.5
    last = lax.cond(crowded,
                    lambda: lax.fori_loop(0, nbits, tie_body, jnp.zeros((1, R, 1), jnp.int32)),
                    lambda: jnp.full((1, R, 1), n * L, jnp.int32))
    return thr, last


def _cmp_pages_kernel(pt_ref, pw_ref, *refs):
    o_ref = refs[-1]
    Wd, ncol = o_ref.shape[2], o_ref.shape[3]
    bpp = PAGE_SIZE // CMP_BLOCK
    col_id = lax.broadcasted_iota(jnp.int32, (Wd, ncol), 1)
    blk_of_row = lax.broadcasted_iota(jnp.int32, (Wd, PAGE_SIZE), 1) // CMP_BLOCK
    out = jnp.zeros((Wd, ncol), F32)
    for k, page in enumerate(refs[:-1]):
        y = page[0] * pw_ref[...]
        for j in range(bpp):
            summary = jnp.sum(jnp.where(blk_of_row == j, y, 0.0), axis=-1, keepdims=True)
            out = jnp.where(col_id == k * bpp + j, summary, out)
    o_ref[0, 0] = out


def cmp_pages(pt_flat, cache_t, pool, B):
    n_pages = pt_flat.shape[0] // B
    pgs = min(PAGES_PER_STEP, n_pages)
    bpp = PAGE_SIZE // CMP_BLOCK
    Wd = cache_t.shape[1]
    pw = jnp.repeat(pool[:, jnp.arange(PAGE_SIZE) % CMP_BLOCK], Wd // 2, axis=0)
    page = lambda k: pl.BlockSpec((1, Wd, PAGE_SIZE), lambda b, s, pt: (pt[b * n_pages + s * pgs + k], 0, 0))
    out = pl.pallas_call(
        _cmp_pages_kernel,
        out_shape=jax.ShapeDtypeStruct((B, n_pages // pgs, Wd, pgs * bpp), F32),
        grid_spec=pltpu.PrefetchScalarGridSpec(
            num_scalar_prefetch=1, grid=(B, n_pages // pgs),
            in_specs=[pl.BlockSpec((Wd, PAGE_SIZE), lambda b, s, pt: (0, 0))]
                     + [page(k) for k in range(pgs)],
            out_specs=pl.BlockSpec((1, 1, Wd, pgs * bpp), lambda b, s, pt: (b, s, 0, 0))),
        compiler_params=_cparams(("parallel", "parallel")),
        name="cmp_pages",
    )(pt_flat, pw, *([cache_t] * pgs))
    return out.transpose(0, 2, 1, 3).reshape(B, Wd, n_pages * bpp)


def _nsa_sample_pre_kernel(q_ref, kvc_ref, wb_ref, new_ref, bc_ref, bw_ref, bn_ref, oc_ref, ow_ref, sel_ref,
                           *, past, n_new):
    H = NSA_HEADS
    R = q_ref.shape[1]
    nb = kvc_ref.shape[2]
    W = wb_ref.shape[2]
    scale = NSA_DH ** -0.5
    nt = (((1,), (1,)), ((), ()))
    qp = _pad_groups(q_ref[0], H)

    kvc = kvc_ref[0]
    s_c = jnp.dot(qp, kvc[:KVH].astype(BF16), preferred_element_type=F32) * scale + bc_ref[...]
    q_pos = past + lax.broadcasted_iota(jnp.int32, (R, nb), 0) // H
    n_i = lax.broadcasted_iota(jnp.int32, (R, nb), 1)
    vis = q_pos >= n_i * CMP_BLOCK + (CMP_BLOCK - 1)
    s_c = jnp.where(vis, s_c, NEG)
    e_c = jnp.where(vis, jnp.exp(s_c - jnp.max(s_c, axis=-1, keepdims=True)), 0.0)
    p_c = e_c / jnp.maximum(jnp.sum(e_c, axis=-1, keepdims=True), 1e-30)
    oc_ref[0] = _own_group(lax.dot_general(p_c.astype(BF16), kvc[KVH:].astype(BF16), nt,
                                           preferred_element_type=F32), H)

    same = (lax.broadcasted_iota(jnp.int32, (R, R), 0) // HPG == lax.broadcasted_iota(jnp.int32, (R, R), 1) // HPG)
    imp = jnp.dot(jnp.where(same, 1.0, 0.0), p_c, preferred_element_type=F32, precision=lax.Precision.HIGHEST)
    cur = n_i == q_pos // CMP_BLOCK
    work = jnp.where(jnp.logical_and(vis, jnp.logical_not(cur)), imp, -1.0)
    sel = jnp.where(cur, 1.0, 0.0)
    n_f = n_i.astype(F32)
    for _ in range(min(N_SEL - 1, nb)):
        mx = jnp.max(work, axis=-1, keepdims=True)
        first = jnp.min(jnp.where(work == mx, n_f, float(nb)), axis=-1, keepdims=True)
        pick = n_f == first
        sel = jnp.where(jnp.logical_and(pick, mx >= 0.0), 1.0, sel)
        work = jnp.where(pick, -2.0, work)
    sel_ref[0] = sel.astype(sel_ref.dtype)

    wb, new = wb_ref[0], new_ref[0]
    s1 = jnp.dot(qp, wb[:KVH].astype(BF16), preferred_element_type=F32) * scale + bw_ref[...]
    i1 = lax.broadcasted_iota(jnp.int32, (R, W), 1)
    d1 = W + lax.broadcasted_iota(jnp.int32, (R, W), 0) // H - i1
    m1 = jnp.logical_and(jnp.logical_and(d1 >= 0, d1 < WINDOW), past - W + i1 >= 0)
    L = new.shape[0]
    s2 = lax.dot_general(qp, new[:, :KVH].astype(BF16), nt, preferred_element_type=F32) * scale + bn_ref[...]
    j2 = lax.broadcasted_iota(jnp.int32, (R, L), 1)
    d2 = lax.broadcasted_iota(jnp.int32, (R, L), 0) // H - j2
    m2 = jnp.logical_and(jnp.logical_and(d2 >= 0, d2 < WINDOW), j2 < n_new)
    s1 = jnp.where(m1, s1, NEG)
    s2 = jnp.where(m2, s2, NEG)
    mx = jnp.maximum(jnp.max(s1, axis=-1, keepdims=True), jnp.max(s2, axis=-1, keepdims=True))
    e1 = jnp.where(m1, jnp.exp(s1 - mx), 0.0)
    e2 = jnp.where(m2, jnp.exp(s2 - mx), 0.0)
    den = jnp.sum(e1, axis=-1, keepdims=True) + jnp.sum(e2, axis=-1, keepdims=True)
    o_w = (lax.dot_general(e1.astype(BF16), wb[KVH:].astype(BF16), nt, preferred_element_type=F32)
           + jnp.dot(e2.astype(BF16), new[:, KVH:].astype(BF16), preferred_element_type=F32))
    ow_ref[0] = _own_group(o_w / jnp.maximum(den, 1e-30), H)


def nsa_sample_pre(q, kvc, wbuf, new_win, bias_c, bias_w, bias_new, past, n_new):
    B, R, dh = q.shape
    nb, W, L = kvc.shape[2], wbuf.shape[2], new_win.shape[1]
    per_b = lambda *s: pl.BlockSpec((1,) + s, lambda b: (b,) + (0,) * len(s))
    const = lambda *s: pl.BlockSpec(s, lambda b: (0,) * len(s))
    return pl.pallas_call(
        functools.partial(_nsa_sample_pre_kernel, past=past, n_new=n_new),
        out_shape=(jax.ShapeDtypeStruct((B, R, dh), F32), jax.ShapeDtypeStruct((B, R, dh), F32),
                   jax.ShapeDtypeStruct((B, R, nb), BF16)),
        grid=(B,),
        in_specs=[per_b(R, dh), per_b(2 * KVH, nb), per_b(2 * KVH, W), per_b(L, 2 * KVH),
                  const(R, nb), const(R, W), const(R, L)],
        out_specs=(per_b(R, dh), per_b(R, dh), per_b(R, nb)),
        compiler_params=_cparams(("parallel",)),
        name="nsa_sample_pre",
    )(q, kvc, wbuf, new_win, bias_c, bias_w, bias_new)


def _index_scores(qi, w, keys, n_heads, feature_major):
    contract = (((1,), (0,)), ((), ())) if feature_major else (((1,), (1,)), ((), ()))
    s = lax.dot_general(qi, keys.astype(BF16), contract, preferred_element_type=F32)
    v = jnp.maximum(s * IDX_DH ** -0.5, 0.0) * w
    T = qi.shape[0] // n_heads
    return jnp.sum(v.reshape(T, n_heads, s.shape[1]), axis=1) * n_heads ** -0.5 + 0.0


def _idx_scores_kernel(pt_ref, q_ref, w_ref, *refs):
    o_ref = refs[-1]
    qi = q_ref[0].astype(BF16)
    for k, page in enumerate(refs[:-1]):
        o_ref[0, k] = _index_scores(qi, w_ref[0], page[0], IDX_HEADS, True)


def idx_scores(pt_flat, q_i, w_i, cache_t):
    B, R, dh = q_i.shape
    T = R // IDX_HEADS
    n_pages = pt_flat.shape[0] // B
    pgs = min(2 * PAGES_PER_STEP, n_pages)
    page = lambda k: pl.BlockSpec((1, dh, PAGE_SIZE), lambda b, s, pt: (pt[b * n_pages + s * pgs + k], 0, 0))
    return pl.pallas_call(
        _idx_scores_kernel,
        out_shape=jax.ShapeDtypeStruct((B, n_pages, T, PAGE_SIZE), F32),
        grid_spec=pltpu.PrefetchScalarGridSpec(
            num_scalar_prefetch=1, grid=(B, n_pages // pgs),
            in_specs=[pl.BlockSpec((1, R, dh), lambda b, s, pt: (b, 0, 0)),
                      pl.BlockSpec((1, R, 1), lambda b, s, pt: (b, 0, 0))] + [page(k) for k in range(pgs)],
            out_specs=pl.BlockSpec((1, pgs, T, PAGE_SIZE), lambda b, s, pt: (b, s, 0, 0))),
        compiler_params=_cparams(("parallel", "parallel")),
        name="idx_scores",
    )(pt_flat, q_i, w_i, *([cache_t] * pgs))


def _dsa_sample_pre_kernel(sp_ref, q_ref, w_ref, kn_ref, sn_ref, thr_ref, last_ref, *, topk, n_new):
    T, L = sp_ref.shape[2], sp_ref.shape[3]
    sc = _index_scores(q_ref[0].astype(BF16), w_ref[0], kn_ref[0], IDX_HEADS, False)
    j = lax.broadcasted_iota(jnp.int32, (T, L), 1)
    t = lax.broadcasted_iota(jnp.int32, (T, L), 0)
    sc = jnp.where(jnp.logical_and(j <= t, j < n_new), sc, NEG)
    sn_ref[0, 0] = sc
    keys = jnp.concatenate([_order_key(sp_ref[0]), _order_key(sc)[None]], axis=0)
    thr, last = _kth_threshold(keys, topk)
    thr_ref[0] = jnp.broadcast_to(thr[0], (T, L))
    last_ref[0] = jnp.broadcast_to(last[0], (T, L))


def dsa_sample_pre(scores_past, q_i, w_i, new_ki, topk, n_new):
    B, n_pages, T, L = scores_past.shape
    R, dh = q_i.shape[1:]
    per_b = lambda *s: pl.BlockSpec((1,) + s, lambda b: (b,) + (0,) * len(s))
    return pl.pallas_call(
        functools.partial(_dsa_sample_pre_kernel, topk=topk, n_new=n_new),
        out_shape=(jax.ShapeDtypeStruct((B, 1, T, L), F32), jax.ShapeDtypeStruct((B, T, L), jnp.int32),
                   jax.ShapeDtypeStruct((B, T, L), jnp.int32)),
        grid=(B,),
        in_specs=[per_b(n_pages, T, L), per_b(R, dh), per_b(R, 1), per_b(L, dh)],
        out_specs=(per_b(1, T, L), per_b(T, L), per_b(T, L)),
        compiler_params=_cparams(("parallel",)),
        name="dsa_sample_pre",
    )(scores_past, q_i, w_i, new_ki)


def _paged_attn_kernel(pt_ref, *refs, mode, pgs, n_pages, n_heads, n_new, dh):
    if mode == "blocks":
        q_ref, new_ref, far_ref, tl_ref, tn_ref, sel_ref, oc_ref, ow_ref, gate_ref = refs[:9]
        rest = refs[9:]
    else:
        q_ref, new_ref, far_ref, tl_ref, tn_ref, sp_ref, sn_ref, thr_ref, last_ref = refs[:9]
        rest = refs[9:]
    pages, o_ref = rest[:pgs], rest[pgs]
    qp_s, m_s, l_s, acc_s = rest[pgs + 1:]
    step, n_steps = pl.program_id(1), pl.num_programs(1)
    R = q_ref.shape[1]
    T = R // n_heads
    L = PAGE_SIZE
    scale = dh ** -0.5

    @pl.when(step == 0)
    def _():
        qp_s[...] = _pad_groups(q_ref[0], n_heads)
        m_s[...] = jnp.full(m_s.shape, NEG, F32)
        l_s[...] = jnp.zeros(l_s.shape, F32)
        acc_s[...] = jnp.zeros(acc_s.shape, F32)

    def update_new(kv, bias, mask):
        s = lax.dot_general(qp_s[...], kv[:, :KVH].astype(BF16), (((1,), (1,)), ((), ())),
                            preferred_element_type=F32)
        s = jnp.where(mask, s * scale + bias, NEG)
        m_old = m_s[...]
        m_new = jnp.maximum(m_old, jnp.max(s, axis=-1, keepdims=True))
        p = jnp.where(mask, jnp.exp(s - m_new), 0.0)
        alpha = jnp.exp(m_old - m_new)
        l_s[...] = alpha * l_s[...] + jnp.sum(p, axis=-1, keepdims=True)
        acc_s[...] = alpha * acc_s[...] + jnp.dot(p.astype(BF16), kv[:, KVH:].astype(BF16),
                                                  preferred_element_type=F32)
        m_s[...] = m_new

    def rows_of_tokens(x):
        return jnp.broadcast_to(x[:, None, :], (T, n_heads, L)).reshape(R, L)

    def chosen(scores, page):
        key = _order_key(scores)
        pos = page * L + lax.broadcasted_iota(jnp.int32, (T, L), 1)
        thr = thr_ref[0]
        ch = jnp.logical_or(key > thr, jnp.logical_and(key == thr, pos <= last_ref[0]))
        return rows_of_tokens(jnp.where(ch, 1.0, 0.0)) > 0.5

    qp = qp_s[...]
    scores, masks = [], []
    for k in range(pgs):
        page = step * pgs + k
        bias = jnp.where(page == n_pages - 1, tl_ref[...], far_ref[...])
        if mode == "blocks":
            nb = sel_ref.shape[2]
            blk = page * (L // CMP_BLOCK) + lax.broadcasted_iota(jnp.int32, (nb, L), 1) // CMP_BLOCK
            expand = jnp.where(lax.broadcasted_iota(jnp.int32, (nb, L), 0) == blk, 1.0, 0.0).astype(BF16)
            mask = jnp.dot(sel_ref[0], expand, preferred_element_type=F32) > 0.5
        else:
            mask = chosen(sp_ref[0, k], page)
        s = jnp.dot(qp, pages[k][0, :KVH].astype(BF16), preferred_element_type=F32)
        scores.append(jnp.where(mask, s * scale + bias, NEG))
        masks.append(mask)
    m_old = m_s[...]
    m_new = m_old
    for s in scores:
        m_new = jnp.maximum(m_new, jnp.max(s, axis=-1, keepdims=True))
    alpha = jnp.exp(m_old - m_new)
    l_new = alpha * l_s[...]
    acc = alpha * acc_s[...]
    for k in range(pgs):
        p = jnp.where(masks[k], jnp.exp(scores[k] - m_new), 0.0)
        l_new = l_new + jnp.sum(p, axis=-1, keepdims=True)
        acc = acc + lax.dot_general(p.astype(BF16), pages[k][0, KVH:].astype(BF16), (((1,), (1,)), ((), ())),
                                    preferred_element_type=F32)
    m_s[...] = m_new
    l_s[...] = l_new
    acc_s[...] = acc

    @pl.when(step == n_steps - 1)
    def _():
        j = lax.broadcasted_iota(jnp.int32, (R, L), 1)
        t = lax.broadcasted_iota(jnp.int32, (R, L), 0) // n_heads
        mask = jnp.logical_and(j <= t, j < n_new)
        if mode != "blocks":
            mask = jnp.logical_and(mask, chosen(sn_ref[0, 0], n_pages))
        update_new(new_ref[0], tn_ref[...], mask)
        o = _own_group(acc_s[...] / jnp.maximum(l_s[...], 1e-30), n_heads)
        if mode == "blocks":
            g = jax.nn.sigmoid(gate_ref[0])
            o = g[:, 0:1] * oc_ref[0] + g[:, 1:2] * o + g[:, 2:3] * ow_ref[0]
        o_ref[0] = o.astype(o_ref.dtype)


def paged_attn(mode, pt_flat, cache_t, q, new_rows, far, tile_last, tile_new, extra, n_heads, n_new):
    B, R, dh = q.shape
    n_pages = pt_flat.shape[0] // B
    pgs = min(PAGES_PER_STEP, n_pages)
    L = PAGE_SIZE
    Wd = cache_t.shape[1]
    per_b = lambda *s: pl.BlockSpec((1,) + s, lambda b, st, pt: (b,) + (0,) * len(s))
    const = lambda *s: pl.BlockSpec(s, lambda b, st, pt: (0,) * len(s))
    page = lambda k: pl.BlockSpec((1, Wd, L), lambda b, st, pt: (pt[b * n_pages + st * pgs + k], 0, 0))
    if mode == "blocks":
        sel = extra[0]
        extra_specs = [per_b(R, sel.shape[2]), per_b(R, dh), per_b(R, dh), per_b(R, 3)]
    else:
        T = R // n_heads
        extra_specs = [pl.BlockSpec((1, pgs, T, L), lambda b, st, pt: (b, st, 0, 0)), per_b(1, T, L),
                       per_b(T, L), per_b(T, L)]
    return pl.pallas_call(
        functools.partial(_paged_attn_kernel, mode=mode, pgs=pgs, n_pages=n_pages, n_heads=n_heads, n_new=n_new,
                          dh=dh),
        out_shape=jax.ShapeDtypeStruct((B, R, dh), BF16),
        grid_spec=pltpu.PrefetchScalarGridSpec(
            num_scalar_prefetch=1, grid=(B, n_pages // pgs),
            in_specs=[per_b(R, dh), per_b(L, Wd), const(R, 1), const(R, L), const(R, L)] + extra_specs
                     + [page(k) for k in range(pgs)],
            out_specs=per_b(R, dh),
            scratch_shapes=[pltpu.VMEM((R, KVH), BF16), pltpu.VMEM((R, 1), F32), pltpu.VMEM((R, 1), F32),
                            pltpu.VMEM((R, KVH), F32)]),
        compiler_params=_cparams(("parallel", "arbitrary")),
        name="paged_attn_" + mode,
    )(pt_flat, q, new_rows, far, tile_last, tile_new, *extra, *([cache_t] * pgs))


def _sample_bias(rel_bias, head0, n_heads, T, past, W, nb):
    assert PAGE_SIZE + 1 >= REL_MAX_DIST
    R = T * n_heads
    t = (np.arange(R) // n_heads)[:, None]
    per_row = jnp.tile(rel_bias[:, head0:head0 + n_heads], (1, T))
    onehot = lambda dist: jax.nn.one_hot(jnp.asarray(_rel_bucket(dist)), NUM_BUCKETS, dtype=F32)
    tab = lambda dist: jnp.einsum('rxk,kr->rx', onehot(dist), per_row, precision=lax.Precision.HIGHEST)
    lane = np.arange(PAGE_SIZE)[None, :]
    return dict(far=per_row[NUM_BUCKETS - 1][:, None], last=tab(PAGE_SIZE + t - lane), new=tab(t - lane),
                cmp=tab(past + t - (np.arange(nb)[None, :] * CMP_BLOCK + CMP_BLOCK - 1)),
                win=tab(W + t - np.arange(W)[None, :]))


def mixer_sample(cache_cmp, cache_sel, cache_win, cache_dkv, cache_idx, page_table, rel_bias, pool,
                 q_n, cmp, sel, win, g_n, q_d, dkv, q_i, k_i, w_i):
    B, T = q_n.shape[:2]
    n_pages = page_table.shape[1]
    past = n_pages * PAGE_SIZE
    W = cache_win.shape[1]
    assert T < CMP_BLOCK and past % CMP_BLOCK == 0 and T <= PAGE_SIZE
    topk = min(DSA_TOPK_MAX, (past + T) // 4)
    n_phys = cache_sel.shape[0]
    nb = past // CMP_BLOCK
    pt_flat = page_table.reshape(-1)
    flat = lambda c: jnp.moveaxis(c, 1, -1).reshape(c.shape[0], -1, c.shape[1])
    rows = lambda a: a.reshape(B, T * a.shape[2], -1)
    pad_new = lambda a: jnp.pad(a.reshape(B, T, -1), ((0, 0), (0, PAGE_SIZE - T), (0, 0)))
    bn = _sample_bias(rel_bias, 0, NSA_HEADS, T, past, W, nb)
    bd = _sample_bias(rel_bias, NSA_HEADS, DSA_HEADS, T, past, W, nb)

    kvc = cmp_pages(pt_flat, flat(cache_cmp), pool, B)
    o_c, o_w, selm = nsa_sample_pre(rows(q_n), kvc, flat(cache_win), pad_new(win), bn["cmp"], bn["win"], bn["new"],
                                    past, T)
    o_n = paged_attn("blocks", pt_flat, flat(cache_sel), rows(q_n), pad_new(sel), bn["far"], bn["last"], bn["new"],
                     (selm, o_c, o_w, g_n.reshape(B, T * NSA_HEADS, 3)), NSA_HEADS, T)

    qi, wi = rows(q_i), w_i.reshape(B, T * IDX_HEADS, 1)
    sp = idx_scores(pt_flat, qi, wi, flat(cache_idx))
    sn, thr, last = dsa_sample_pre(sp, qi, wi, pad_new(k_i), topk, T)
    o_d = paged_attn("thr", pt_flat, flat(cache_dkv), rows(q_d), pad_new(dkv), bd["far"], bd["last"], bd["new"],
                     (sp, sn, thr, last), DSA_HEADS, T)
    new_win = jnp.concatenate([cache_win[:, T:], win], axis=1)
    return o_n.reshape(B, T, -1), o_d.reshape(B, T, -1), new_win


def _project(h, w_q, w_kv, w_s, w_g):
    B, S, D = h.shape
    h2 = h.reshape(B * S, D)
    zq = matmul(h2, w_q, out_dtype=F32)
    zkv = matmul(h2, w_kv, out_dtype=F32)
    zs = matmul(h2, w_s, out_dtype=F32)
    kvn = (B, S, 2, NSA_KV, NSA_DH)
    q_n = zq[:, :NSA_WIDTH].reshape(B, S, NSA_HEADS, NSA_DH)
    q_d = zq[:, NSA_WIDTH:NSA_WIDTH + DSA_WIDTH].reshape(B, S, DSA_HEADS, DSA_DH)
    q_i = zq[:, NSA_WIDTH + DSA_WIDTH:].reshape(B, S, IDX_HEADS, IDX_DH)
    cmp = zkv[:, :KV_W].reshape(kvn)
    sel = zkv[:, KV_W:2 * KV_W].reshape(kvn)
    win = zkv[:, 2 * KV_W:3 * KV_W].reshape(kvn)
    dkv = zkv[:, 3 * KV_W:].reshape(B, S, 2, DSA_KV, DSA_DH)
    k_i = zs[:, :IDX_DH].reshape(B, S, IDX_DH)
    w_i = zs[:, IDX_DH:IDX_DH + IDX_HEADS].reshape(B, S, IDX_HEADS)
    g_n = zs[:, IDX_DH + IDX_HEADS:].reshape(B, S, NSA_HEADS, 3)
    return (q_n, cmp, sel, win, g_n, q_d, dkv, q_i, k_i, w_i), h2


def kernel(x_prompt, x_sample, c_prompt, c_sample, cache_nsa_cmp, cache_nsa_sel, cache_nsa_win, cache_dsa_kv,
           cache_dsa_idx, page_table, rel_bias, w_mod, b_mod, g_pre_mix, g_post_mix, g_pre_ffn, g_post_ffn,
           w_in, cmp_pool, w_up_nsa, w_up_dsa, w_out, w_router, b_router, w_gu, b_gu, w_down, b_down):
    l = 0
    D = D_MODEL
    Bp, Sp, _ = x_prompt.shape
    Bs, Ss, _ = x_sample.shape

    c = jnp.concatenate([c_prompt, c_sample], 0)
    n_c = c.shape[0]
    c_pad = jnp.pad(jax.nn.silu(c), ((0, -n_c % 16), (0, 0))).astype(BF16)
    mod = matmul(c_pad, w_mod[l], b_mod[l], tn=1536)[:n_c]
    mod_p, mod_s = mod[:Bp], mod[Bp:]

    wi = w_in[l]
    seg = lambda k: wi[:, _OFF[k]:_OFF[k + 1]]
    w_q = jnp.concatenate([seg(0), seg(5), seg(7)], 1).astype(BF16)
    w_kv = jnp.concatenate([seg(1), seg(2), seg(3), seg(6)], 1).astype(BF16)
    w_s = jnp.concatenate([seg(8), seg(9), seg(4)], 1).astype(BF16)
    w_g = seg(10).astype(BF16)
    wun, wud, wo = w_up_nsa[l].astype(BF16), w_up_dsa[l].astype(BF16), w_out[l].astype(BF16)

    def mix_front(x, mod_g):
        sh1, sc1 = mod_g[:, :D], mod_g[:, D:2 * D]
        h = norm_mod(x, g_pre_mix[l], sc1, sh1)
        return _project(h, w_q, w_kv, w_s, w_g)

    def mix_back(x, mod_g, o_n, o_d, h2):
        B, S, _ = x.shape
        ga1, sh2, sc2 = mod_g[:, 2 * D:3 * D], mod_g[:, 3 * D:4 * D], mod_g[:, 4 * D:5 * D]
        u = merge_up(o_n.reshape(B * S, -1).astype(BF16), o_d.reshape(B * S, -1).astype(BF16), h2, w_g, wun, wud)
        return out_proj(u, x, wo, g_post_mix[l], g_pre_ffn[l], ga1, sc2, sh2, w_router[l], b_router[l])

    parts_p, zg_p = mix_front(x_prompt, mod_p)
    o_n, o_d = mixer_prompt(rel_bias, cmp_pool[l], *parts_p)
    x1_p, h_p, lg_p = mix_back(x_prompt, mod_p, o_n, o_d, zg_p)

    parts_s, zg_s = mix_front(x_sample, mod_s)
    o_n, o_d, new_win = mixer_sample(cache_nsa_cmp[l], cache_nsa_sel[l], cache_nsa_win[l], cache_dsa_kv[l],
                                     cache_dsa_idx[l], page_table, rel_bias, cmp_pool[l], *parts_s)
    x1_s, h_s, lg_s = mix_back(x_sample, mod_s, o_n, o_d, zg_s)

    Tp, Ts = Bp * Sp, Bs * Ss
    h_all = jnp.concatenate([h_p.reshape(Tp, D), h_s.reshape(Ts, D)], 0)
    logits = jnp.concatenate([lg_p.reshape(Tp, N_EXPERTS), lg_s.reshape(Ts, N_EXPERTS)], 0)
    probs, row_tok, blk_e, n_used, dest_of = moe_route(logits)
    out_rows = moe_experts(h_all[row_tok], blk_e, n_used, w_gu[l], b_gu[l], w_down[l], b_down[l])
    picked = lambda lo, hi: [out_rows[dest_of[lo:hi, k]] for k in range(TOP_K)]
    y_p = moe_combine(picked(0, Tp), probs[:Tp], x1_p, g_post_ffn[l], mod_p[:, 5 * D:])
    y_s = moe_combine(picked(Tp, Tp + Ts), probs[Tp:], x1_s, g_post_ffn[l], mod_s[:, 5 * D:])

    st = lambda a: a[None]
    q_n, cmp_p, sel_p, win_p, g_n, q_d, dkv_p, q_i, ki_p, w_i = parts_p
    q_n, cmp_s, sel_s, win_s, g_n, q_d, dkv_s, q_i, ki_s, w_i = parts_s
    keep = min(WINDOW, Sp)
    return (y_p, y_s, st(cmp_p), st(sel_p), st(win_p[:, Sp - keep:]), st(dkv_p), st(ki_p),
            st(cmp_s), st(sel_s), st(new_win), st(dkv_s), st(ki_s))
```
